```python
import math, functools
import jax, jax.numpy as jnp
from jax import lax
import numpy as np

D_MODEL = 4096
BATCH = 4
SEQ = 4096
DEPTH = 1

MEM_LEN = 256
RMS_EPS = 1e-6
GLA_HEADS = 4
GLA_DV = D_MODEL // 2 // GLA_HEADS
GLA_DK = GLA_DV // 2
GLA_LOWRANK = 16
GLA_TAU = 16.0
GLA_CHUNK = 64
DIL_HEAD_DIM = 128
DIL_HEADS = D_MODEL // 2 // DIL_HEAD_DIM
DIL_CONFIGS = ((128, 1), (512, 4), (2048, 16))
REL_BUCKETS = 32
REL_MAX_DIST = 2048
XATTN_HEADS = 4
XATTN_HEAD_DIM = 128
XATTN_WIDTH = XATTN_HEADS * XATTN_HEAD_DIM
D_FF = ((8 * D_MODEL // 3 + 255) // 256) * 256
CONV_WIDTH = 3
MIX_WIDTH = GLA_HEADS * GLA_DV + DIL_HEADS * DIL_HEAD_DIM
NEG_INF = -1e30

kernel_name = 'hybrid_gla_dilated_parallel_heads'


def _proj_sizes():
    return (GLA_HEADS * GLA_DK,
            GLA_HEADS * GLA_DK,
            GLA_HEADS * GLA_DV,
            GLA_LOWRANK,
            GLA_HEADS * GLA_DV,
            DIL_HEADS * DIL_HEAD_DIM,
            DIL_HEADS * DIL_HEAD_DIM,
            DIL_HEADS * DIL_HEAD_DIM)


def rmsnorm(x, g):
    xf = x.astype(jnp.float32)
    y = xf * lax.rsqrt(jnp.mean(xf * xf, axis=-1, keepdims=True) + RMS_EPS)
    return (y * g.astype(jnp.float32)).astype(x.dtype)


def t5_bucket(dist):
    max_exact = REL_BUCKETS // 2
    d_f = jnp.maximum(dist, 1).astype(jnp.float32)
    large = max_exact + (jnp.log(d_f / max_exact) / math.log(REL_MAX_DIST / max_exact)
                         * (REL_BUCKETS - max_exact)).astype(jnp.int32)
    large = jnp.minimum(large, REL_BUCKETS - 1)
    return jnp.where(dist < max_exact, dist, large)


def gla_chunked(q, k, v, g_log):
    B, S, H, DK = q.shape
    DV = v.shape[-1]
    C = GLA_CHUNK
    N = S // C

    def chunk(t):
        return t.reshape(B, N, C, H, -1).transpose(1, 0, 3, 2, 4)

    q, k, v, g = map(chunk, (q * DK ** -0.5, k, v, g_log))
    b = jnp.cumsum(g, axis=3)
    b_last = b[:, :, :, -1:, :]
    b_mid = b[:, :, :, C // 2:C // 2 + 1, :]
    q_start = q * jnp.exp(b)
    k_end = k * jnp.exp(b_last - b)
    chunk_decay = jnp.exp(b_last[:, :, :, 0, :])

    def step(state, inp):
        q_c, k_c, v_c, dec = inp
        o_c = jnp.einsum('bhcd,bhde->bhce', q_c, state)
        state = state * dec[..., None] + jnp.einsum('bhcd,bhce->bhde', k_c, v_c)
        return state, o_c

    state0 = jnp.zeros((B, H, DK, DV), jnp.float32)
    _, o_inter = lax.scan(step, state0, (q_start, k_end, v, chunk_decay))

    att = jnp.einsum('nbhcd,nbhjd->nbhcj', q * jnp.exp(b - b_mid), k * jnp.exp(b_mid - b))
    att = jnp.where(jnp.tril(jnp.ones((C, C), bool)), att, 0.0)
    o = o_inter + jnp.einsum('nbhcj,nbhje->nbhce', att, v)
    return o.transpose(1, 0, 3, 2, 4).reshape(B, S, H, DV)


def dilated_branch(q, k, v, dilation, steps, rel_bias):
    B, S, H, E = q.shape
    L = S // dilation
    nb = -(-L // steps)
    Lp = nb * steps

    def to_blocks(t):
        t = t.reshape(B, L, dilation, H, E)
        t = jnp.pad(t, ((0, 0), (0, Lp - L), (0, 0), (0, 0), (0, 0)))
        return t.reshape(B, nb, steps, dilation, H, E)

    def with_prev(t):
        prev = jnp.pad(t, ((0, 0), (1, 0), (0, 0), (0, 0), (0, 0), (0, 0)))[:, :-1]
        return jnp.concatenate([prev, t], axis=2)

    qb, kb, vb = to_blocks(q), to_blocks(k), to_blocks(v)
    kw, vw = with_prev(kb), with_prev(vb)
    logits = jnp.einsum('bnqrhe,bnkrhe->bnrhqk', qb, kw,
                        preferred_element_type=jnp.float32) * (E ** -0.5)

    qi = jnp.arange(steps)[:, None]
    kj = jnp.arange(2 * steps)[None, :]
    rel = qi + steps - kj
    band = (rel >= 0) & (rel <= steps)
    key_exists = (jnp.arange(nb)[:, None, None] * steps + kj[None] - steps) >= 0
    valid = band[None] & key_exists
    bias = rel_bias[t5_bucket(jnp.clip(rel, 0, steps) * dilation)].astype(jnp.float32)
    logits = logits + jnp.transpose(bias, (2, 0, 1))[None, None, None]
    logits = jnp.where(valid[None, :, None, None], logits, NEG_INF)

    m = jnp.max(logits, axis=-1)
    p = jnp.exp(logits - m[..., None])
    s = jnp.sum(p, axis=-1)
    o = jnp.einsum('bnrhqk,bnkrhe->bnqrhe', p.astype(v.dtype), vw,
                   preferred_element_type=jnp.float32)

    o = o.reshape(B, Lp, dilation, H, E)[:, :L].reshape(B, S, H, E)
    m = jnp.transpose(m, (0, 1, 4, 2, 3)).reshape(B, Lp, dilation, H)[:, :L].reshape(B, S, H)
    s = jnp.transpose(s, (0, 1, 4, 2, 3)).reshape(B, Lp, dilation, H)[:, :L].reshape(B, S, H)
    return o, m, s


def dilated_attention(q, k, v, rel_bias):
    outs = [dilated_branch(q, k, v, d, w // d, rel_bias) for (w, d) in DIL_CONFIGS]
    m_max = functools.reduce(jnp.maximum, [m for _, m, _ in outs])
    num = jnp.zeros(q.shape, jnp.float32)
    den = jnp.zeros(q.shape[:-1], jnp.float32)
    for o, m, s in outs:
        wgt = jnp.exp(m - m_max)
        num = num + wgt[..., None] * o
        den = den + wgt * s
    return num / den[..., None]


def memory_cross_attention(hn, memn, w_q, w_k, w_v, w_o):
    B, S, _ = hn.shape
    M = memn.shape[1]
    q = (hn @ w_q).reshape(B, S, XATTN_HEADS, XATTN_HEAD_DIM)
    k = (memn @ w_k).reshape(B, M, XATTN_HEADS, XATTN_HEAD_DIM)
    v = (memn @ w_v).reshape(B, M, XATTN_HEADS, XATTN_HEAD_DIM)
    logits = jnp.einsum('bshe,bmhe->bhsm', q, k,
                        preferred_element_type=jnp.float32) * (XATTN_HEAD_DIM ** -0.5)
    p = jax.nn.softmax(logits, axis=-1)
    o = jnp.einsum('bhsm,bmhe->bshe', p.astype(v.dtype), v).reshape(B, S, XATTN_WIDTH)
    return o @ w_o


def causal_dwconv(u, w, b):
    K = w.shape[0]
    S = u.shape[1]
    up = jnp.pad(u, ((0, 0), (K - 1, 0), (0, 0)))
    y = b
    for i in range(K):
        y = y + up[:, i:i + S] * w[i]
    return y


def hybrid_layer(x, mem, rel_bias, norm_mix_g, w_in, gla_w_gate2, gla_b_gate, gla_norm_g,
                 w_out, norm_xattn_g, mem_norm_g, w_xq, w_xk, w_xv, w_xo, norm_ffn_g,
                 w_ffn_gate, w_ffn_up, ffn_conv_w, ffn_conv_b, w_ffn_down):
    B, S, _ = x.shape
    f32 = jnp.float32

    hn = rmsnorm(x, norm_mix_g)
    proj = hn @ w_in
    split_at = np.cumsum(_proj_sizes())[:-1].tolist()
    gq, gk, gv, glr, gr, dq, dk, dv = jnp.split(proj, split_at, axis=-1)

    g_log = jax.nn.log_sigmoid((glr @ gla_w_gate2 + gla_b_gate).astype(f32)) / GLA_TAU
    o_gla = gla_chunked(gq.reshape(B, S, GLA_HEADS, GLA_DK).astype(f32),
                        gk.reshape(B, S, GLA_HEADS, GLA_DK).astype(f32),
                        gv.reshape(B, S, GLA_HEADS, GLA_DV).astype(f32),
                        g_log.reshape(B, S, GLA_HEADS, GLA_DK))
    o_gla = rmsnorm(o_gla, gla_norm_g) * jax.nn.silu(
        gr.astype(f32)).reshape(B, S, GLA_HEADS, GLA_DV)
    o_gla = o_gla.reshape(B, S, GLA_HEADS * GLA_DV)

    o_dil = dilated_attention(dq.reshape(B, S, DIL_HEADS, DIL_HEAD_DIM),
                              dk.reshape(B, S, DIL_HEADS, DIL_HEAD_DIM),
                              dv.reshape(B, S, DIL_HEADS, DIL_HEAD_DIM), rel_bias)
    o_dil = o_dil.reshape(B, S, DIL_HEADS * DIL_HEAD_DIM)

    mix = jnp.concatenate([o_gla, o_dil], axis=-1).astype(x.dtype) @ w_out
    h = x + mix

    h = h + memory_cross_attention(rmsnorm(h, norm_xattn_g), rmsnorm(mem, mem_norm_g),
                                   w_xq, w_xk, w_xv, w_xo)

    hn = rmsnorm(h, norm_ffn_g)
    gate = causal_dwconv(hn @ w_ffn_gate, ffn_conv_w, ffn_conv_b)
    h = h + (jax.nn.silu(gate) * (hn @ w_ffn_up)) @ w_ffn_down
    return h


def setup_inputs(seed: int = 0) -> dict:
    key = jax.random.key(seed)
    ks = jax.random.split(key, 22)
    f32 = jnp.float32

    def nrm(k, shape, scale):
        return jax.random.normal(k, shape, f32) * scale

    def gain(k, shape):
        return 1.0 + 0.05 * jax.random.normal(k, shape, f32)

    L = DEPTH
    n_cols = sum(_proj_sizes())
    return {
        'x': nrm(ks[0], (BATCH, SEQ, D_MODEL), 1.0),
        'mem': nrm(ks[1], (BATCH, MEM_LEN, D_MODEL), 1.0),
        'rel_bias': nrm(ks[2], (REL_BUCKETS, DIL_HEADS), 0.5),
        'norm_mix_g': gain(ks[3], (L, D_MODEL)),
        'w_in': nrm(ks[4], (L, D_MODEL, n_cols), D_MODEL ** -0.5),
        'gla_w_gate2': nrm(ks[5], (L, GLA_LOWRANK, GLA_HEADS * GLA_DK), GLA_LOWRANK ** -0.5),
        'gla_b_gate': nrm(ks[6], (L, GLA_HEADS * GLA_DK), 0.1),
        'gla_norm_g': gain(ks[7], (L, GLA_DV)),
        'w_out': nrm(ks[8], (L, MIX_WIDTH, D_MODEL), MIX_WIDTH ** -0.5),
        'norm_xattn_g': gain(ks[9], (L, D_MODEL)),
        'mem_norm_g': gain(ks[10], (L, D_MODEL)),
        'w_xq': nrm(ks[11], (L, D_MODEL, XATTN_WIDTH), D_MODEL ** -0.5),
        'w_xk': nrm(ks[12], (L, D_MODEL, XATTN_WIDTH), D_MODEL ** -0.5),
        'w_xv': nrm(ks[13], (L, D_MODEL, XATTN_WIDTH), D_MODEL ** -0.5),
        'w_xo': nrm(ks[14], (L, XATTN_WIDTH, D_MODEL), XATTN_WIDTH ** -0.5),
        'norm_ffn_g': gain(ks[15], (L, D_MODEL)),
        'w_ffn_gate': nrm(ks[16], (L, D_MODEL, D_FF), D_MODEL ** -0.5),
        'w_ffn_up': nrm(ks[17], (L, D_MODEL, D_FF), D_MODEL ** -0.5),
        'ffn_conv_w': nrm(ks[18], (L, CONV_WIDTH, D_FF), CONV_WIDTH ** -0.5),
        'ffn_conv_b': nrm(ks[19], (L, D_FF), 0.02),
        'w_ffn_down': nrm(ks[20], (L, D_FF, D_MODEL), D_FF ** -0.5),
        'final_norm_g': gain(ks[21], (D_MODEL,)),
    }


def reference(x, mem, rel_bias, norm_mix_g, w_in, gla_w_gate2, gla_b_gate, gla_norm_g,
              w_out, norm_xattn_g, mem_norm_g, w_xq, w_xk, w_xv, w_xo, norm_ffn_g,
              w_ffn_gate, w_ffn_up, ffn_conv_w, ffn_conv_b, w_ffn_down, final_norm_g):
    h = x
    for i in range(DEPTH):
        h = hybrid_layer(h, mem, rel_bias, norm_mix_g[i], w_in[i], gla_w_gate2[i],
                         gla_b_gate[i], gla_norm_g[i], w_out[i], norm_xattn_g[i],
                         mem_norm_g[i], w_xq[i], w_xk[i], w_xv[i], w_xo[i], norm_ffn_g[i],
                         w_ffn_gate[i], w_ffn_up[i], ffn_conv_w[i], ffn_conv_b[i],
                         w_ffn_down[i])
    return rmsnorm(h, final_norm_g)
```

```python
import functools
import math

import numpy as np
import jax
import jax.numpy as jnp
from jax import lax
from jax.experimental import pallas as pl
from jax.experimental.pallas import tpu as pltpu

F32 = jnp.float32
BF16 = jnp.bfloat16

D_MODEL = 4096
RMS_EPS = 1e-6
GLA_HEADS = 4
GLA_DV = 512
GLA_DK = 256
GLA_LOWRANK = 16
GLA_TAU = 16.0
GLA_CHUNK = 64
DIL_HEAD_DIM = 128
DIL_HEADS = 16
DIL_CONFIGS = ((128, 1), (512, 4), (2048, 16))
DIL_STEPS = 128
REL_BUCKETS = 32
REL_MAX_DIST = 2048
XATTN_HEADS = 4
XATTN_HEAD_DIM = 128
XATTN_WIDTH = 512
D_FF = 11008
CONV_WIDTH = 3
NEG_INF = -1e30

LANES = 128
SUBLANES = 8
VMEM_LIMIT = 56 * 1024 * 1024

D_FF_PAD = 11264

_GQ, _GK, _GV, _GR = 0, 1024, 2048, 4096
_DQ, _DK, _DV = 6144, 8192, 10240
PROJ_COLS = 12288


def _cparams(sem):
    return pltpu.CompilerParams(dimension_semantics=sem, vmem_limit_bytes=VMEM_LIMIT)


def _rmsnorm_kernel(x_ref, g_ref, o_ref):
    x = x_ref[...]
    ms = jnp.mean(x * x, axis=-1, keepdims=True)
    o_ref[...] = ((x * lax.rsqrt(ms + RMS_EPS)) * g_ref[...]).astype(o_ref.dtype)


def _rmsnorm(x, g, tm=256):
    m, d = x.shape
    return pl.pallas_call(
        _rmsnorm_kernel,
        grid=(m // tm,),
        in_specs=[pl.BlockSpec((tm, d), lambda i: (i, 0)),
                  pl.BlockSpec((1, d), lambda i: (0, 0))],
        out_specs=pl.BlockSpec((tm, d), lambda i: (i, 0)),
        out_shape=jax.ShapeDtypeStruct((m, d), BF16),
        compiler_params=_cparams(("parallel",)),
    )(x, g.reshape(1, d))


def _mm_kernel(x_ref, w_ref, o_ref):
    o_ref[...] = jnp.dot(x_ref[...], w_ref[...],
                         preferred_element_type=F32).astype(o_ref.dtype)


def _matmul(x, w, bm, bn, out_dtype):
    m, k = x.shape
    n = w.shape[1]
    return pl.pallas_call(
        _mm_kernel,
        grid=(m // bm, n // bn),
        in_specs=[pl.BlockSpec((bm, k), lambda i, j: (i, 0)),
                  pl.BlockSpec((k, bn), lambda i, j: (0, j))],
        out_specs=pl.BlockSpec((bm, bn), lambda i, j: (i, j)),
        out_shape=jax.ShapeDtypeStruct((m, n), out_dtype),
        compiler_params=_cparams(("parallel", "parallel")),
    )(x, w)


GLA_T = 512


def _split_bf16(x):
    hi = x.astype(BF16)
    lo = (x - hi.astype(F32)).astype(BF16)
    return hi, lo


def _gla_kernel(q_ref, k_ref, v_ref, r_ref, glr_ref, w2_ref, b2_ref, gn_ref,
                o_ref, state_ref, g_ref):
    c_sz = GLA_CHUNK

    @pl.when(pl.program_id(2) == 0)
    def _():
        state_ref[...] = jnp.zeros_like(state_ref)

    x_hi, x_lo = _split_bf16(glr_ref[...])
    w_hi, w_lo = _split_bf16(w2_ref[...])
    z = (jnp.dot(x_hi, w_hi, preferred_element_type=F32)
         + jnp.dot(x_lo, w_hi, preferred_element_type=F32)
         + jnp.dot(x_hi, w_lo, preferred_element_type=F32)) + b2_ref[...]
    log_sig = jnp.minimum(z, 0.0) - jnp.log1p(jnp.exp(-jnp.abs(z)))
    g_ref[...] = log_sig / GLA_TAU

    row = lax.broadcasted_iota(jnp.int32, (c_sz, c_sz), 0)
    col = lax.broadcasted_iota(jnp.int32, (c_sz, c_sz), 1)
    tril = row >= col
    tril_bf = jnp.where(tril, 1.0, 0.0).astype(BF16)
    gn = gn_ref[...]
    nt = (((1,), (1,)), ((), ()))
    tn = (((0,), (0,)), ((), ()))

    def chunk(c, carry):
        rows = pl.ds(pl.multiple_of(c * c_sz, c_sz), c_sz)
        g = g_ref[rows, :]
        g_hi, g_lo = _split_bf16(g)
        b = (jnp.dot(tril_bf, g_hi, preferred_element_type=F32)
             + jnp.dot(tril_bf, g_lo, preferred_element_type=F32))
        b_last = b[c_sz - 1:c_sz, :]
        b_mid = b[c_sz // 2:c_sz // 2 + 1, :]
        q = q_ref[rows, :].astype(F32) * (GLA_DK ** -0.5)
        k = k_ref[rows, :].astype(F32)
        v = v_ref[rows, :]
        q_start = (q * jnp.exp(b)).astype(BF16)
        k_end = (k * jnp.exp(b_last - b)).astype(BF16)
        q_mid = (q * jnp.exp(b - b_mid)).astype(BF16)
        k_mid = (k * jnp.exp(b_mid - b)).astype(BF16)
        st = state_ref[...]
        o_inter = lax.dot_general(q_start, st.astype(BF16), nt,
                                  preferred_element_type=F32)
        att = lax.dot_general(q_mid, k_mid, nt, preferred_element_type=F32)
        att = jnp.where(tril, att, 0.0)
        o = o_inter + jnp.dot(att.astype(BF16), v, preferred_element_type=F32)
        kv_t = lax.dot_general(v, k_end, tn, preferred_element_type=F32)
        state_ref[...] = st * jnp.exp(b_last) + kv_t
        ms = jnp.mean(o * o, axis=-1, keepdims=True)
        on = (o * lax.rsqrt(ms + RMS_EPS)) * gn
        r = r_ref[rows, :].astype(F32)
        gate = r / (1.0 + jnp.exp(-r))
        o_ref[rows, :] = (on * gate).astype(o_ref.dtype)
        return carry

    lax.fori_loop(0, GLA_T // c_sz, chunk, 0, unroll=2)


def _gla(proj, glr, w2_pad, b2, gn, batch, seq):
    t = GLA_T
    nt_ = seq // t
    m = batch * seq
    qb, kb = _GQ // GLA_DK, _GK // GLA_DK
    vb, rb = _GV // GLA_DV, _GR // GLA_DV
    return pl.pallas_call(
        _gla_kernel,
        grid=(batch, GLA_HEADS, nt_),
        in_specs=[
            pl.BlockSpec((t, GLA_DK), lambda b, h, s: (b * nt_ + s, qb + h)),
            pl.BlockSpec((t, GLA_DK), lambda b, h, s: (b * nt_ + s, kb + h)),
            pl.BlockSpec((t, GLA_DV), lambda b, h, s: (b * nt_ + s, vb + h)),
            pl.BlockSpec((t, GLA_DV), lambda b, h, s: (b * nt_ + s, rb + h)),
            pl.BlockSpec((t, LANES), lambda b, h, s: (b * nt_ + s, 0)),
            pl.BlockSpec((LANES, GLA_DK), lambda b, h, s: (0, h)),
            pl.BlockSpec((1, GLA_DK), lambda b, h, s: (0, h)),
            pl.BlockSpec((1, GLA_DV), lambda b, h, s: (0, 0)),
        ],
        out_specs=pl.BlockSpec((t, GLA_DV), lambda b, h, s: (b * nt_ + s, h)),
        out_shape=jax.ShapeDtypeStruct((m, GLA_HEADS * GLA_DV), BF16),
        scratch_shapes=[pltpu.VMEM((GLA_DV, GLA_DK), F32),
                        pltpu.VMEM((t, GLA_DK), F32)],
        compiler_params=_cparams(("parallel", "parallel", "arbitrary")),
    )(proj, proj, proj, proj, glr, w2_pad, b2, gn)


def _t5_bucket_np(dist):
    max_exact = REL_BUCKETS // 2
    d_f = np.maximum(dist, 1).astype(np.float32)
    large = max_exact + (np.log(d_f / np.float32(max_exact))
                         / np.float32(math.log(REL_MAX_DIST / max_exact))
                         * np.float32(REL_BUCKETS - max_exact)).astype(np.int32)
    large = np.minimum(large, REL_BUCKETS - 1)
    return np.where(dist < max_exact, dist, large)


def _dil_bucket_index():
    steps = DIL_STEPS
    qi = np.arange(steps)[:, None]
    kj = np.arange(2 * steps)[None, :]
    rel = qi + steps - kj
    band = (rel >= 0) & (rel <= steps)
    out = []
    for _, dil in DIL_CONFIGS:
        bucket = _t5_bucket_np(np.clip(rel, 0, steps) * dil)
        out.append(np.where(band, bucket, REL_BUCKETS))
    return np.stack(out).astype(np.int32)


def _dil_kernel(relb_ref, idx_ref, q_ref, k_ref, v_ref, o_ref,
                qf, kf, vf, m_sc, l_sc, acc_sc, bias_sc, *, seq):
    steps = DIL_STEPS
    e = DIL_HEAD_DIM
    pad = steps * DIL_CONFIGS[-1][1]
    head = pl.program_id(1)
    scale = e ** -0.5
    nt = (((1,), (1,)), ((), ()))

    qf[...] = q_ref[0].astype(F32)
    kf[0:pad, :] = jnp.zeros((pad, e), F32)
    vf[0:pad, :] = jnp.zeros((pad, e), F32)
    kf[pad:pad + seq, :] = k_ref[0].astype(F32)
    vf[pad:pad + seq, :] = v_ref[0].astype(F32)

    col = lax.broadcasted_iota(jnp.int32, (steps, 2 * steps), 1)
    for c in range(len(DIL_CONFIGS)):
        idx = idx_ref[c]
        bias = jnp.full((steps, 2 * steps), NEG_INF, F32)
        for bkt in range(REL_BUCKETS):
            bias = jnp.where(idx == bkt, relb_ref[bkt, head], bias)
        bias_sc[c, 0] = bias
        bias_sc[c, 1] = jnp.where(col < steps, NEG_INF, bias)

    def rows_of(start, size, dil):
        if dil == 1:
            return pl.ds(start, size)
        return pl.ds(start, size, stride=dil)

    for c, (_, dil) in enumerate(DIL_CONFIGS):
        nb = seq // dil // steps
        first_cfg = c == 0
        last_cfg = c == len(DIL_CONFIGS) - 1

        def block(j, carry, c=c, dil=dil, nb=nb, first_cfg=first_cfg, last_cfg=last_cfg):
            r = j // nb
            n = j % nb
            q_rows = rows_of(n * (steps * dil) + r, steps, dil)
            k_rows = rows_of(pad + (n - 1) * (steps * dil) + r, 2 * steps, dil)
            q = qf[q_rows, :].astype(BF16)
            k = kf[k_rows, :].astype(BF16)
            v = vf[k_rows, :].astype(BF16)
            s = lax.dot_general(q, k, nt, preferred_element_type=F32) * scale
            s = s + bias_sc[c, jnp.where(n == 0, 1, 0)]
            m_blk = jnp.max(s, axis=-1, keepdims=True)
            if first_cfg:
                m_new = jnp.broadcast_to(m_blk, (steps, e))
            else:
                m_prev = m_sc[q_rows, :]
                m_new = jnp.maximum(m_prev, m_blk)
            p = jnp.exp(s - jnp.concatenate([m_new, m_new], axis=1))
            l_blk = jnp.sum(p, axis=-1, keepdims=True)
            pv = jnp.dot(p.astype(BF16), v, preferred_element_type=F32)
            if first_cfg:
                l_new = jnp.broadcast_to(l_blk, (steps, e))
                acc_new = pv
            else:
                alpha = jnp.exp(m_prev - m_new)
                l_new = alpha * l_sc[q_rows, :] + l_blk
                acc_new = alpha * acc_sc[q_rows, :] + pv
            if last_cfg:
                acc_sc[q_rows, :] = acc_new / l_new
            else:
                m_sc[q_rows, :] = m_new
                l_sc[q_rows, :] = l_new
                acc_sc[q_rows, :] = acc_new
            return carry

        lax.fori_loop(0, seq // steps, block, 0, unroll=2)

    o_ref[0] = acc_sc[...].astype(o_ref.dtype)


def _dilated(proj3, rel_bias, batch, seq):
    e = DIL_HEAD_DIM
    steps = DIL_STEPS
    pad = steps * DIL_CONFIGS[-1][1]
    idx = jnp.asarray(_dil_bucket_index())
    qb, kb, vb = _DQ // e, _DK // e, _DV // e
    ncfg = len(DIL_CONFIGS)
    return pl.pallas_call(
        functools.partial(_dil_kernel, seq=seq),
        grid=(batch, DIL_HEADS),
        in_specs=[
            pl.BlockSpec(memory_space=pltpu.SMEM),
            pl.BlockSpec((ncfg, steps, 2 * steps), lambda b, h: (0, 0, 0)),
            pl.BlockSpec((1, seq, e), lambda b, h: (b, 0, qb + h)),
            pl.BlockSpec((1, seq, e), lambda b, h: (b, 0, kb + h)),
            pl.BlockSpec((1, seq, e), lambda b, h: (b, 0, vb + h)),
        ],
        out_specs=pl.BlockSpec((1, seq, e), lambda b, h: (b, 0, h)),
        out_shape=jax.ShapeDtypeStruct((batch, seq, DIL_HEADS * e), BF16),
        scratch_shapes=[
            pltpu.VMEM((seq, e), F32),
            pltpu.VMEM((pad + seq, e), F32),
            pltpu.VMEM((pad + seq, e), F32),
            pltpu.VMEM((seq, e), F32),
            pltpu.VMEM((seq, e), F32),
            pltpu.VMEM((seq, e), F32),
            pltpu.VMEM((ncfg, 2, steps, 2 * steps), F32),
        ],
        compiler_params=_cparams(("parallel", "parallel")),
    )(rel_bias, idx, proj3, proj3, proj3)


def _mix_out_kernel(a_ref, b_ref, wa_ref, wb_ref, x_ref, o_ref):
    acc = jnp.dot(a_ref[...], wa_ref[...], preferred_element_type=F32)
    acc = acc + jnp.dot(b_ref[...], wb_ref[...], preferred_element_type=F32)
    o_ref[...] = x_ref[...] + acc


def _mix_out(o_gla, o_dil, w_a, w_b, x, bm=512, bn=1024):
    m, ka = o_gla.shape
    kb = o_dil.shape[1]
    n = w_a.shape[1]
    return pl.pallas_call(
        _mix_out_kernel,
        grid=(m // bm, n // bn),
        in_specs=[pl.BlockSpec((bm, ka), lambda i, j: (i, 0)),
                  pl.BlockSpec((bm, kb), lambda i, j: (i, 0)),
                  pl.BlockSpec((ka, bn), lambda i, j: (0, j)),
                  pl.BlockSpec((kb, bn), lambda i, j: (0, j)),
                  pl.BlockSpec((bm, bn), lambda i, j: (i, j))],
        out_specs=pl.BlockSpec((bm, bn), lambda i, j: (i, j)),
        out_shape=jax.ShapeDtypeStruct((m, n), F32),
        compiler_params=_cparams(("parallel", "parallel")),
    )(o_gla, o_dil, w_a, w_b, x)


def _xattn_kernel(h_ref, gx_ref, wq_ref, k_ref, v_ref, wo_ref, gf_ref, h2_ref, hn_ref):
    e = XATTN_HEAD_DIM
    nt = (((1,), (1,)), ((), ()))
    h = h_ref[...]
    ms = jnp.mean(h * h, axis=-1, keepdims=True)
    hn = ((h * lax.rsqrt(ms + RMS_EPS)) * gx_ref[...]).astype(BF16)
    q = jnp.dot(hn, wq_ref[...], preferred_element_type=F32).astype(BF16)
    outs = []
    for hh in range(XATTN_HEADS):
        qh = q[:, hh * e:(hh + 1) * e]
        kh = k_ref[0, :, hh * e:(hh + 1) * e]
        vh = v_ref[0, :, hh * e:(hh + 1) * e]
        s = lax.dot_general(qh, kh, nt, preferred_element_type=F32) * (e ** -0.5)
        m = jnp.max(s, axis=-1, keepdims=True)
        p = jnp.exp(s - m)
        p = p / jnp.sum(p, axis=-1, keepdims=True)
        outs.append(jnp.dot(p.astype(BF16), vh, preferred_element_type=F32))
    o = jnp.concatenate(outs, axis=1).astype(BF16)
    h2 = h + jnp.dot(o, wo_ref[...], preferred_element_type=F32)
    h2_ref[...] = h2
    ms2 = jnp.mean(h2 * h2, axis=-1, keepdims=True)
    hn_ref[...] = ((h2 * lax.rsqrt(ms2 + RMS_EPS)) * gf_ref[...]).astype(hn_ref.dtype)


def _xattn(h1, gx, wq, kx, vx, wo, gf, seq, tm=256):
    m, d = h1.shape
    mem_len = kx.shape[1]
    per_seq = seq // tm
    return pl.pallas_call(
        _xattn_kernel,
        grid=(m // tm,),
        in_specs=[pl.BlockSpec((tm, d), lambda i: (i, 0)),
                  pl.BlockSpec((1, d), lambda i: (0, 0)),
                  pl.BlockSpec((d, XATTN_WIDTH), lambda i: (0, 0)),
                  pl.BlockSpec((1, mem_len, XATTN_WIDTH), lambda i: (i // per_seq, 0, 0)),
                  pl.BlockSpec((1, mem_len, XATTN_WIDTH), lambda i: (i // per_seq, 0, 0)),
                  pl.BlockSpec((XATTN_WIDTH, d), lambda i: (0, 0)),
                  pl.BlockSpec((1, d), lambda i: (0, 0))],
        out_specs=[pl.BlockSpec((tm, d), lambda i: (i, 0)),
                   pl.BlockSpec((tm, d), lambda i: (i, 0))],
        out_shape=[jax.ShapeDtypeStruct((m, d), F32),
                   jax.ShapeDtypeStruct((m, d), BF16)],
        compiler_params=_cparams(("parallel",)),
    )(h1, gx.reshape(1, d), wq, kx, vx, wo, gf.reshape(1, d))


def _ffn_in_kernel(x_ref, wg_ref, wu_ref, g_ref, u_ref):
    x = x_ref[...]
    g_ref[...] = jnp.dot(x, wg_ref[...], preferred_element_type=F32).astype(g_ref.dtype)
    u_ref[...] = jnp.dot(x, wu_ref[...], preferred_element_type=F32).astype(u_ref.dtype)


def _ffn_in(x, wg, wu, bm=1024, bn=512):
    m, k = x.shape
    n = wg.shape[1]
    return pl.pallas_call(
        _ffn_in_kernel,
        grid=(m // bm, n // bn),
        in_specs=[pl.BlockSpec((bm, k), lambda i, j: (i, 0)),
                  pl.BlockSpec((k, bn), lambda i, j: (0, j)),
                  pl.BlockSpec((k, bn), lambda i, j: (0, j))],
        out_specs=[pl.BlockSpec((bm, bn), lambda i, j: (i, j)),
                   pl.BlockSpec((bm, bn), lambda i, j: (i, j))],
        out_shape=[jax.ShapeDtypeStruct((m, n), BF16),
                   jax.ShapeDtypeStruct((m, n), BF16)],
        compiler_params=_cparams(("parallel", "parallel")),
    )(x, wg, wu)


def _ffn_out_kernel(g_ref, halo_ref, u_ref, cw_ref, cb_ref, wd_ref, h_ref, fg_ref,
                    o_ref, *, tiles_per_seq):
    i = pl.program_id(0)
    kk = pl.program_id(1)

    @pl.when(kk == 0)
    def _():
        o_ref[...] = h_ref[...]

    g = g_ref[...].astype(F32)
    halo = halo_ref[...].astype(F32)
    halo = jnp.where(i % tiles_per_seq == 0, 0.0, halo)
    row = lax.broadcasted_iota(jnp.int32, g.shape, 0)
    prev1 = halo[SUBLANES - 1:SUBLANES, :]
    prev2 = halo[SUBLANES - 2:SUBLANES - 1, :]
    g_m1 = jnp.where(row == 0, prev1, pltpu.roll(g, 1, axis=0))
    g_m2 = jnp.where(row == 0, prev2, jnp.where(row == 1, prev1, pltpu.roll(g, 2, axis=0)))
    cw = cw_ref[...]
    y = cb_ref[...] + g_m2 * cw[0:1, :]
    y = y + g_m1 * cw[1:2, :]
    y = y + g * cw[2:3, :]
    act = (y / (1.0 + jnp.exp(-y))) * u_ref[...].astype(F32)
    o_ref[...] += jnp.dot(act.astype(BF16), wd_ref[...], preferred_element_type=F32)

    @pl.when(kk == pl.num_programs(1) - 1)
    def _():
        h3 = o_ref[...]
        ms = jnp.mean(h3 * h3, axis=-1, keepdims=True)
        o_ref[...] = (h3 * lax.rsqrt(ms + RMS_EPS)) * fg_ref[...]


def _ffn_out(g, u, cw, cb, wd, h2, fg, seq, tm=512, tk=512):
    m, ff = g.shape
    d = wd.shape[1]
    halo_blocks = tm // SUBLANES
    return pl.pallas_call(
        functools.partial(_ffn_out_kernel, tiles_per_seq=seq // tm),
        grid=(m // tm, ff // tk),
        in_specs=[pl.BlockSpec((tm, tk), lambda i, k: (i, k)),
                  pl.BlockSpec((SUBLANES, tk),
                               lambda i, k: (jnp.maximum(i * halo_blocks - 1, 0), k)),
                  pl.BlockSpec((tm, tk), lambda i, k: (i, k)),
                  pl.BlockSpec((CONV_WIDTH, tk), lambda i, k: (0, k)),
                  pl.BlockSpec((1, tk), lambda i, k: (0, k)),
                  pl.BlockSpec((tk, d), lambda i, k: (k, 0)),
                  pl.BlockSpec((tm, d), lambda i, k: (i, 0), pipeline_mode=pl.Buffered(1)),
                  pl.BlockSpec((1, d), lambda i, k: (0, 0))],
        out_specs=pl.BlockSpec((tm, d), lambda i, k: (i, 0)),
        out_shape=jax.ShapeDtypeStruct((m, d), F32),
        compiler_params=_cparams(("parallel", "arbitrary")),
    )(g, g, u, cw, cb, wd, h2, fg.reshape(1, d))


def _layer(x2, mem2, rel_bias, batch, seq, mem_len, norm_mix_g, w_in, gla_w_gate2,
           gla_b_gate, gla_norm_g, w_out, norm_xattn_g, mem_norm_g, w_xq, w_xk, w_xv,
           w_xo, norm_ffn_g, w_ffn_gate, w_ffn_up, ffn_conv_w, ffn_conv_b, w_ffn_down,
           out_g):
    d = D_MODEL
    s_q, s_k, s_v, s_lr, s_r = 0, 1024, 2048, 4096, 4112
    s_dq = s_r + 2048
    w_main = jnp.concatenate([w_in[:, s_q:s_lr], w_in[:, s_r:]], axis=1).astype(BF16)
    w_glr = jnp.pad(w_in[:, s_lr:s_r], ((0, 0), (0, LANES - GLA_LOWRANK))).astype(BF16)
    del s_k, s_v, s_dq
    w2_pad = jnp.pad(gla_w_gate2, ((0, LANES - GLA_LOWRANK), (0, 0)))
    ff_pad = D_FF_PAD - D_FF
    wg = jnp.pad(w_ffn_gate.astype(BF16), ((0, 0), (0, ff_pad)))
    wu = jnp.pad(w_ffn_up.astype(BF16), ((0, 0), (0, ff_pad)))
    wd = jnp.pad(w_ffn_down.astype(BF16), ((0, ff_pad), (0, 0)))
    cw = jnp.pad(ffn_conv_w, ((0, 0), (0, ff_pad)))
    cb = jnp.pad(ffn_conv_b, ((0, ff_pad),)).reshape(1, D_FF_PAD)
    w_out_b = w_out.astype(BF16)
    w_kv = jnp.concatenate([w_xk, w_xv], axis=1).astype(BF16)

    hn = _rmsnorm(x2, norm_mix_g)
    proj = _matmul(hn, w_main, 1024, 1024, BF16)
    glr = _matmul(hn, w_glr, 1024, LANES, F32)
    o_gla = _gla(proj, glr, w2_pad, gla_b_gate.reshape(1, -1), gla_norm_g.reshape(1, -1),
                 batch, seq)
    o_dil = _dilated(proj.reshape(batch, seq, PROJ_COLS), rel_bias, batch, seq)
    o_dil = o_dil.reshape(batch * seq, DIL_HEADS * DIL_HEAD_DIM)
    half = GLA_HEADS * GLA_DV
    h1 = _mix_out(o_gla, o_dil, w_out_b[:half], w_out_b[half:], x2)

    memn = _rmsnorm(mem2, mem_norm_g)
    kv = _matmul(memn, w_kv, 512, 512, BF16)
    kx = kv[:, :XATTN_WIDTH].reshape(batch, mem_len, XATTN_WIDTH)
    vx = kv[:, XATTN_WIDTH:].reshape(batch, mem_len, XATTN_WIDTH)
    h2, hn3 = _xattn(h1, norm_xattn_g, w_xq.astype(BF16), kx, vx, w_xo.astype(BF16),
                     norm_ffn_g, seq)

    g_pre, up = _ffn_in(hn3, wg, wu)
    return _ffn_out(g_pre, up, cw, cb, wd, h2, out_g, seq)


def kernel(x, mem, rel_bias, norm_mix_g, w_in, gla_w_gate2, gla_b_gate, gla_norm_g, w_out,
           norm_xattn_g, mem_norm_g, w_xq, w_xk, w_xv, w_xo, norm_ffn_g, w_ffn_gate,
           w_ffn_up, ffn_conv_w, ffn_conv_b, w_ffn_down, final_norm_g):
    batch, seq, d = x.shape
    mem_len = mem.shape[1]
    depth = w_in.shape[0]
    assert depth == 1, "the fused final rmsnorm assumes a single layer"
    out = _layer(x.reshape(batch * seq, d), mem.reshape(batch * mem_len, d), rel_bias,
                 batch, seq, mem_len, norm_mix_g[0], w_in[0], gla_w_gate2[0],
                 gla_b_gate[0], gla_norm_g[0], w_out[0], norm_xattn_g[0], mem_norm_g[0],
                 w_xq[0], w_xk[0], w_xv[0], w_xo[0], norm_ffn_g[0], w_ffn_gate[0],
                 w_ffn_up[0], ffn_conv_w[0], ffn_conv_b[0], w_ffn_down[0], final_norm_g)
    return out.reshape(batch, seq, d)
```

```python
import functools
import math

import numpy as np
import jax
import jax.numpy as jnp
from jax import lax
from jax.experimental import pallas as pl
from jax.experimental.pallas import tpu as pltpu

F32 = jnp.float32
BF16 = jnp.bfloat16

D_MODEL = 4096
RMS_EPS = 1e-6
GLA_HEADS = 4
GLA_DV = 512
GLA_DK = 256
GLA_LOWRANK = 16
GLA_TAU = 16.0
GLA_CHUNK = 64
DIL_HEAD_DIM = 128
DIL_HEADS = 16
DIL_CONFIGS = ((128, 1), (512, 4), (2048, 16))
DIL_STEPS = 128
REL_BUCKETS = 32
REL_MAX_DIST = 2048
XATTN_HEADS = 4
XATTN_HEAD_DIM = 128
XATTN_WIDTH = 512
D_FF = 11008
CONV_WIDTH = 3
NEG_INF = -1e30

LANES = 128
SUBLANES = 8
VMEM_LIMIT = 56 * 1024 * 1024

D_FF_PAD = 11264
FFN_BM = 1024

_W_GQKV = 0
_W_GLR = 4096
_W_GR = 4112
_W_DQKV = 6160
_W_END = 12304


def _cparams(sem):
    return pltpu.CompilerParams(dimension_semantics=sem, vmem_limit_bytes=VMEM_LIMIT)


def _rmsnorm_kernel(x_ref, g_ref, o_ref):
    x = x_ref[...]
    ms = jnp.mean(x * x, axis=-1, keepdims=True)
    o_ref[...] = ((x * lax.rsqrt(ms + RMS_EPS)) * g_ref[...]).astype(o_ref.dtype)


def _rmsnorm(x, g, tm=256):
    m, d = x.shape
    return pl.pallas_call(
        _rmsnorm_kernel,
        grid=(m // tm,),
        in_specs=[pl.BlockSpec((tm, d), lambda i: (i, 0)),
                  pl.BlockSpec((1, d), lambda i: (0, 0))],
        out_specs=pl.BlockSpec((tm, d), lambda i: (i, 0)),
        out_shape=jax.ShapeDtypeStruct((m, d), BF16),
        compiler_params=_cparams(("parallel",)),
    )(x, g.reshape(1, d))


def _cast_pad_cols_kernel(w_ref, o_ref):
    n = w_ref.shape[1]
    o_ref[:, :n] = w_ref[...].astype(o_ref.dtype)
    o_ref[:, n:] = jnp.zeros((o_ref.shape[0], o_ref.shape[1] - n), o_ref.dtype)


def _cast_pad_cols(w, n_pad, tm=256):
    k, n = w.shape
    return pl.pallas_call(
        _cast_pad_cols_kernel,
        grid=(k // tm,),
        in_specs=[pl.BlockSpec((tm, n), lambda i: (i, 0))],
        out_specs=pl.BlockSpec((tm, n_pad), lambda i: (i, 0)),
        out_shape=jax.ShapeDtypeStruct((k, n_pad), BF16),
        compiler_params=_cparams(("parallel",)),
    )(w)


def _cast_pad_rows_kernel(w_ref, o_ref, *, n_valid):
    valid = pl.program_id(0) < n_valid
    o_ref[...] = jnp.where(valid, w_ref[...], 0.0).astype(o_ref.dtype)


def _cast_pad_rows(w, k_pad, tm=256):
    k, n = w.shape
    n_valid = k // tm
    return pl.pallas_call(
        functools.partial(_cast_pad_rows_kernel, n_valid=n_valid),
        grid=(k_pad // tm,),
        in_specs=[pl.BlockSpec((tm, n), lambda i: (jnp.minimum(i, n_valid - 1), 0))],
        out_specs=pl.BlockSpec((tm, n), lambda i: (i, 0)),
        out_shape=jax.ShapeDtypeStruct((k_pad, n), BF16),
        compiler_params=_cparams(("parallel",)),
    )(w)


def _mm_kernel(x_ref, w_ref, o_ref):
    o_ref[...] = jnp.dot(x_ref[...], w_ref[...],
                         preferred_element_type=F32).astype(o_ref.dtype)


def _matmul(x, w, bm, bn, out_dtype):
    m, k = x.shape
    n = w.shape[1]
    return pl.pallas_call(
        _mm_kernel,
        grid=(m // bm, n // bn),
        in_specs=[pl.BlockSpec((bm, k), lambda i, j: (i, 0)),
                  pl.BlockSpec((k, bn), lambda i, j: (0, j))],
        out_specs=pl.BlockSpec((bm, bn), lambda i, j: (i, j)),
        out_shape=jax.ShapeDtypeStruct((m, n), out_dtype),
        compiler_params=_cparams(("parallel", "parallel")),
    )(x, w)


GLA_T = 512
GLA_GROUP = 4


def _split_bf16(x):
    hi = x.astype(BF16)
    lo = (x - hi.astype(F32)).astype(BF16)
    return hi, lo


def _gla_kernel(q_ref, k_ref, v_ref, r_ref, glr_ref, w2_ref, b2_ref, gn_ref,
                o_ref, state_ref, b_ref):
    c_sz = GLA_CHUNK

    @pl.when(pl.program_id(2) == 0)
    def _():
        state_ref[...] = jnp.zeros_like(state_ref)

    x_hi, x_lo = _split_bf16(glr_ref[...])
    w_hi, w_lo = _split_bf16(w2_ref[...])
    z = (jnp.dot(x_hi, w_hi, preferred_element_type=F32)
         + jnp.dot(x_lo, w_hi, preferred_element_type=F32)
         + jnp.dot(x_hi, w_lo, preferred_element_type=F32)) + b2_ref[...]
    log_sig = jnp.minimum(z, 0.0) - jnp.log1p(jnp.exp(-jnp.abs(z)))
    g = log_sig / GLA_TAU

    t_sz = g.shape[0]
    row_t = lax.broadcasted_iota(jnp.int32, (t_sz, t_sz), 0)
    col_t = lax.broadcasted_iota(jnp.int32, (t_sz, t_sz), 1)
    same_chunk = (row_t // c_sz) == (col_t // c_sz)
    tri_blk = jnp.where(same_chunk, jnp.where(row_t >= col_t, 1.0, 0.0), 0.0).astype(BF16)
    g_hi, g_lo = _split_bf16(g)
    b_ref[...] = (jnp.dot(tri_blk, g_hi, preferred_element_type=F32)
                  + jnp.dot(tri_blk, g_lo, preferred_element_type=F32))

    row = lax.broadcasted_iota(jnp.int32, (c_sz, c_sz), 0)
    col = lax.broadcasted_iota(jnp.int32, (c_sz, c_sz), 1)
    tril = row >= col
    gn = gn_ref[...]
    nt = (((1,), (1,)), ((), ()))
    tn = (((0,), (0,)), ((), ()))

    def group(gi, carry):
        st = state_ref[...]
        outs = []
        for u in range(GLA_GROUP):
            rows = pl.ds(pl.multiple_of((gi * GLA_GROUP + u) * c_sz, c_sz), c_sz)
            b = b_ref[rows, :]
            b_last = b[c_sz - 1:c_sz, :]
            b_mid = b[c_sz // 2:c_sz // 2 + 1, :]
            q = q_ref[rows, :].astype(F32) * (GLA_DK ** -0.5)
            k = k_ref[rows, :].astype(F32)
            v = v_ref[rows, :]
            q_start = (q * jnp.exp(b)).astype(BF16)
            k_end = (k * jnp.exp(b_last - b)).astype(BF16)
            q_mid = (q * jnp.exp(b - b_mid)).astype(BF16)
            k_mid = (k * jnp.exp(b_mid - b)).astype(BF16)
            o_inter = lax.dot_general(q_start, st.astype(BF16), nt,
                                      preferred_element_type=F32)
            att = lax.dot_general(q_mid, k_mid, nt, preferred_element_type=F32)
            att = jnp.where(tril, att, 0.0)
            o = o_inter + jnp.dot(att.astype(BF16), v, preferred_element_type=F32)
            kv_t = lax.dot_general(v, k_end, tn, preferred_element_type=F32)
            st = st * jnp.exp(b_last) + kv_t
            ms = jnp.mean(o * o, axis=-1, keepdims=True)
            on = (o * lax.rsqrt(ms + RMS_EPS)) * gn
            r = r_ref[rows, :].astype(F32)
            gate = r / (1.0 + jnp.exp(-r))
            outs.append((rows, (on * gate).astype(o_ref.dtype)))
        for rows, o in outs:
            o_ref[rows, :] = o
        state_ref[...] = st
        return carry

    lax.fori_loop(0, GLA_T // (c_sz * GLA_GROUP), group, 0)


def _gla(qkv, gr, glr, w2_pad, b2, gn, batch, seq):
    t = GLA_T
    nt_ = seq // t
    m = batch * seq
    kb = (GLA_HEADS * GLA_DK) // GLA_DK
    vb = (2 * GLA_HEADS * GLA_DK) // GLA_DV
    return pl.pallas_call(
        _gla_kernel,
        grid=(batch, GLA_HEADS, nt_),
        in_specs=[
            pl.BlockSpec((t, GLA_DK), lambda b, h, s: (b * nt_ + s, h)),
            pl.BlockSpec((t, GLA_DK), lambda b, h, s: (b * nt_ + s, kb + h)),
            pl.BlockSpec((t, GLA_DV), lambda b, h, s: (b * nt_ + s, vb + h)),
            pl.BlockSpec((t, GLA_DV), lambda b, h, s: (b * nt_ + s, h)),
            pl.BlockSpec((t, LANES), lambda b, h, s: (b * nt_ + s, 0)),
            pl.BlockSpec((LANES, GLA_DK), lambda b, h, s: (0, h)),
            pl.BlockSpec((1, GLA_DK), lambda b, h, s: (0, h)),
            pl.BlockSpec((1, GLA_DV), lambda b, h, s: (0, 0)),
        ],
        out_specs=pl.BlockSpec((t, GLA_DV), lambda b, h, s: (b * nt_ + s, h)),
        out_shape=jax.ShapeDtypeStruct((m, GLA_HEADS * GLA_DV), BF16),
        scratch_shapes=[pltpu.VMEM((GLA_DV, GLA_DK), F32),
                        pltpu.VMEM((t, GLA_DK), F32)],
        compiler_params=_cparams(("parallel", "parallel", "arbitrary")),
    )(qkv, qkv, qkv, gr, glr, w2_pad, b2, gn)


def _t5_bucket_np(dist):
    max_exact = REL_BUCKETS // 2
    d_f = np.maximum(dist, 1).astype(np.float32)
    large = max_exact + (np.log(d_f / np.float32(max_exact))
                         / np.float32(math.log(REL_MAX_DIST / max_exact))
                         * np.float32(REL_BUCKETS - max_exact)).astype(np.int32)
    large = np.minimum(large, REL_BUCKETS - 1)
    return np.where(dist < max_exact, dist, large)


def _dil_bucket_index():
    steps = DIL_STEPS
    qi = np.arange(steps)[:, None]
    kj = np.arange(2 * steps)[None, :]
    rel = qi + steps - kj
    band = (rel >= 0) & (rel <= steps)
    out = []
    for _, dil in DIL_CONFIGS:
        bucket = _t5_bucket_np(np.clip(rel, 0, steps) * dil)
        out.append(np.where(band, bucket, REL_BUCKETS))
    return np.stack(out).astype(np.int32)


def _dil_bias_kernel(relb_ref, idx_ref, o_ref):
    head = pl.program_id(1)
    idx = idx_ref[0]
    bias = jnp.full(idx.shape, NEG_INF, F32)
    for bkt in range(REL_BUCKETS):
        bias = jnp.where(idx == bkt, relb_ref[bkt, head], bias)
    o_ref[0, 0] = bias


def _dil_bias(rel_bias):
    steps = DIL_STEPS
    ncfg = len(DIL_CONFIGS)
    idx = jnp.asarray(_dil_bucket_index())
    return pl.pallas_call(
        _dil_bias_kernel,
        grid=(ncfg, DIL_HEADS),
        in_specs=[pl.BlockSpec(memory_space=pltpu.SMEM),
                  pl.BlockSpec((1, steps, 2 * steps), lambda c, h: (c, 0, 0))],
        out_specs=pl.BlockSpec((1, 1, steps, 2 * steps), lambda c, h: (c, h, 0, 0)),
        out_shape=jax.ShapeDtypeStruct((ncfg, DIL_HEADS, steps, 2 * steps), F32),
        compiler_params=_cparams(("parallel", "parallel")),
    )(rel_bias, idx)


DIL_MERGE_ROWS = 256
DIL_GROUP = 8


def _dil_kernel(bias_ref, q_ref, k_ref, v_ref, o_ref, m_sc, l_sc, acc_sc, *, seq):
    steps = DIL_STEPS
    e = DIL_HEAD_DIM
    scale = e ** -0.5
    nt = (((1,), (1,)), ((), ()))

    def rows_of(start, size, dil):
        if dil == 1:
            return pl.ds(start, size)
        return pl.ds(start, size, stride=dil)

    def attend(c, dil, blocks):
        q_rows = [rows_of(q_start, steps, dil) for q_start, _, _, _ in blocks]
        k_rows = [rows_of(k_start, n_keys, dil) for _, k_start, n_keys, _ in blocks]
        logits = []
        for qr, kr, (_, _, _, bias) in zip(q_rows, k_rows, blocks):
            q = q_ref[0, qr, :].astype(BF16)
            k = k_ref[0, kr, :].astype(BF16)
            logits.append(lax.dot_general(q, k, nt, preferred_element_type=F32) * scale + bias)
        stats = []
        for s in logits:
            m = jnp.max(s, axis=-1, keepdims=True)
            p = jnp.exp(s - m)
            stats.append((m, jnp.sum(p, axis=-1, keepdims=True), p.astype(BF16)))
        pvs = [jnp.dot(p, v_ref[0, kr, :].astype(BF16), preferred_element_type=F32)
               for (_, _, p), kr in zip(stats, k_rows)]
        for qr, (m, l, _), pv in zip(q_rows, stats, pvs):
            m_sc[c, qr, :] = jnp.broadcast_to(m, (steps, e))
            l_sc[c, qr, :] = jnp.broadcast_to(l, (steps, e))
            acc_sc[c, qr, :] = pv

    for c, (_, dil) in enumerate(DIL_CONFIGS):
        nb = seq // dil // steps
        span = steps * dil

        def first(r, c=c):
            return (r, r, steps, bias_ref[c, 0, :, steps:2 * steps])

        def later(j, c=c, nb=nb, span=span):
            r = j // (nb - 1)
            n = 1 + j % (nb - 1)
            return (n * span + r, (n - 1) * span + r, 2 * steps, bias_ref[c, 0])

        for make, count in ((first, dil), (later, dil * (nb - 1))):
            full, rest = divmod(count, DIL_GROUP)

            def group(gi, carry, c=c, dil=dil, make=make):
                attend(c, dil, [make(gi * DIL_GROUP + u) for u in range(DIL_GROUP)])
                return carry

            if full:
                lax.fori_loop(0, full, group, 0)
            if rest:
                attend(c, dil, [make(full * DIL_GROUP + u) for u in range(rest)])

    def merge(t, carry):
        rows = pl.ds(pl.multiple_of(t * DIL_MERGE_ROWS, DIL_MERGE_ROWS), DIL_MERGE_ROWS)
        ms = [m_sc[c, rows, :] for c in range(len(DIL_CONFIGS))]
        m_max = functools.reduce(jnp.maximum, ms)
        num = jnp.zeros((DIL_MERGE_ROWS, e), F32)
        den = jnp.zeros((DIL_MERGE_ROWS, e), F32)
        for c in range(len(DIL_CONFIGS)):
            wgt = jnp.exp(ms[c] - m_max)
            num = num + wgt * acc_sc[c, rows, :]
            den = den + wgt * l_sc[c, rows, :]
        o_ref[0, rows, :] = (num / den).astype(o_ref.dtype)
        return carry

    lax.fori_loop(0, seq // DIL_MERGE_ROWS, merge, 0)


def _dilated(qkv3, bias, batch, seq):
    e = DIL_HEAD_DIM
    steps = DIL_STEPS
    ncfg = len(DIL_CONFIGS)
    return pl.pallas_call(
        functools.partial(_dil_kernel, seq=seq),
        grid=(batch, DIL_HEADS),
        in_specs=[
            pl.BlockSpec((ncfg, 1, steps, 2 * steps), lambda b, h: (0, h, 0, 0)),
            pl.BlockSpec((1, seq, e), lambda b, h: (b, 0, h)),
            pl.BlockSpec((1, seq, e), lambda b, h: (b, 0, DIL_HEADS + h)),
            pl.BlockSpec((1, seq, e), lambda b, h: (b, 0, 2 * DIL_HEADS + h)),
        ],
        out_specs=pl.BlockSpec((1, seq, e), lambda b, h: (b, 0, h)),
        out_shape=jax.ShapeDtypeStruct((batch, seq, DIL_HEADS * e), BF16),
        scratch_shapes=[pltpu.VMEM((ncfg, seq, e), F32),
                        pltpu.VMEM((ncfg, seq, e), F32),
                        pltpu.VMEM((ncfg, seq, e), F32)],
        compiler_params=_cparams(("parallel", "parallel")),
    )(bias, qkv3, qkv3, qkv3)


def _mix_out_kernel(a_ref, b_ref, wa_ref, wb_ref, x_ref, o_ref):
    acc = jnp.dot(a_ref[...], wa_ref[...], preferred_element_type=F32)
    acc = acc + jnp.dot(b_ref[...], wb_ref[...], preferred_element_type=F32)
    o_ref[...] = x_ref[...] + acc


def _mix_out(o_gla, o_dil, w_a, w_b, x, bm=512, bn=1024):
    m, ka = o_gla.shape
    kb = o_dil.shape[1]
    n = w_a.shape[1]
    return pl.pallas_call(
        _mix_out_kernel,
        grid=(m // bm, n // bn),
        in_specs=[pl.BlockSpec((bm, ka), lambda i, j: (i, 0)),
                  pl.BlockSpec((bm, kb), lambda i, j: (i, 0)),
                  pl.BlockSpec((ka, bn), lambda i, j: (0, j)),
                  pl.BlockSpec((kb, bn), lambda i, j: (0, j)),
                  pl.BlockSpec((bm, bn), lambda i, j: (i, j))],
        out_specs=pl.BlockSpec((bm, bn), lambda i, j: (i, j)),
        out_shape=jax.ShapeDtypeStruct((m, n), F32),
        compiler_params=_cparams(("parallel", "parallel")),
    )(o_gla, o_dil, w_a, w_b, x)


def _xattn_kernel(h_ref, gx_ref, wq_ref, k_ref, v_ref, wo_ref, gf_ref, h2_ref, hn_ref):
    e = XATTN_HEAD_DIM
    nt = (((1,), (1,)), ((), ()))
    h = h_ref[...]
    ms = jnp.mean(h * h, axis=-1, keepdims=True)
    hn = ((h * lax.rsqrt(ms + RMS_EPS)) * gx_ref[...]).astype(BF16)
    q = jnp.dot(hn, wq_ref[...], preferred_element_type=F32).astype(BF16)
    outs = []
    for hh in range(XATTN_HEADS):
        qh = q[:, hh * e:(hh + 1) * e]
        kh = k_ref[0, :, hh * e:(hh + 1) * e]
        vh = v_ref[0, :, hh * e:(hh + 1) * e]
        s = lax.dot_general(qh, kh, nt, preferred_element_type=F32) * (e ** -0.5)
        m = jnp.max(s, axis=-1, keepdims=True)
        p = jnp.exp(s - m)
        p = p / jnp.sum(p, axis=-1, keepdims=True)
        outs.append(jnp.dot(p.astype(BF16), vh, preferred_element_type=F32))
    o = jnp.concatenate(outs, axis=1).astype(BF16)
    h2 = h + jnp.dot(o, wo_ref[...], preferred_element_type=F32)
    h2_ref[...] = h2
    ms2 = jnp.mean(h2 * h2, axis=-1, keepdims=True)
    hn_ref[...] = ((h2 * lax.rsqrt(ms2 + RMS_EPS)) * gf_ref[...]).astype(hn_ref.dtype)


def _xattn(h1, gx, wq, kx, vx, wo, gf, seq, tm=256):
    m, d = h1.shape
    mem_len = kx.shape[1]
    per_seq = seq // tm
    return pl.pallas_call(
        _xattn_kernel,
        grid=(m // tm,),
        in_specs=[pl.BlockSpec((tm, d), lambda i: (i, 0)),
                  pl.BlockSpec((1, d), lambda i: (0, 0)),
                  pl.BlockSpec((d, XATTN_WIDTH), lambda i: (0, 0)),
                  pl.BlockSpec((1, mem_len, XATTN_WIDTH), lambda i: (i // per_seq, 0, 0)),
                  pl.BlockSpec((1, mem_len, XATTN_WIDTH), lambda i: (i // per_seq, 0, 0)),
                  pl.BlockSpec((XATTN_WIDTH, d), lambda i: (0, 0)),
                  pl.BlockSpec((1, d), lambda i: (0, 0))],
        out_specs=[pl.BlockSpec((tm, d), lambda i: (i, 0)),
                   pl.BlockSpec((tm, d), lambda i: (i, 0))],
        out_shape=[jax.ShapeDtypeStruct((m, d), F32),
                   jax.ShapeDtypeStruct((m, d), BF16)],
        compiler_params=_cparams(("parallel",)),
    )(h1, gx.reshape(1, d), wq, kx, vx, wo, gf.reshape(1, d))


def _ffn_in_kernel(x_ref, wg_ref, wu_ref, halo_ref, cw_ref, cb_ref, a_ref):
    x = x_ref[...]
    g = jnp.dot(x, wg_ref[...], preferred_element_type=F32)
    u = jnp.dot(x, wu_ref[...], preferred_element_type=F32)
    halo = halo_ref[0]
    prev1 = halo[SUBLANES - 1:SUBLANES, :]
    prev2 = halo[SUBLANES - 2:SUBLANES - 1, :]
    row = lax.broadcasted_iota(jnp.int32, (SUBLANES, g.shape[1]), 0)
    r1 = pltpu.roll(g, 1, axis=0)
    r2 = pltpu.roll(g, 2, axis=0)
    head1 = jnp.where(row == 0, prev1, r1[:SUBLANES])
    head2 = jnp.where(row == 0, prev2, jnp.where(row == 1, prev1, r2[:SUBLANES]))
    g_m1 = jnp.concatenate([head1, r1[SUBLANES:]], axis=0)
    g_m2 = jnp.concatenate([head2, r2[SUBLANES:]], axis=0)
    cw = cw_ref[...]
    y = cb_ref[...] + g_m2 * cw[0:1, :]
    y = y + g_m1 * cw[1:2, :]
    y = y + g * cw[2:3, :]
    a_ref[...] = ((y / (1.0 + jnp.exp(-y))) * u).astype(a_ref.dtype)


def _ffn_in(x, wg, wu, halo_g, cw, cb, bm=FFN_BM, bn=512):
    m, k = x.shape
    n = wg.shape[1]
    return pl.pallas_call(
        _ffn_in_kernel,
        grid=(m // bm, n // bn),
        in_specs=[pl.BlockSpec((bm, k), lambda i, j: (i, 0)),
                  pl.BlockSpec((k, bn), lambda i, j: (0, j)),
                  pl.BlockSpec((k, bn), lambda i, j: (0, j)),
                  pl.BlockSpec((1, SUBLANES, bn), lambda i, j: (i, 0, j)),
                  pl.BlockSpec((CONV_WIDTH, bn), lambda i, j: (0, j)),
                  pl.BlockSpec((1, bn), lambda i, j: (0, j))],
        out_specs=pl.BlockSpec((bm, bn), lambda i, j: (i, j)),
        out_shape=jax.ShapeDtypeStruct((m, n), BF16),
        compiler_params=_cparams(("parallel", "parallel")),
    )(x, wg, wu, halo_g, cw, cb)


def _ffn_halo_rows(hn, batch, seq, bm=FFN_BM):
    d = hn.shape[1]
    tiles = seq // bm
    tail = hn.reshape(batch, tiles, bm, d)[:, :, bm - SUBLANES:, :]
    prev = jnp.concatenate([jnp.zeros_like(tail[:, :1]), tail[:, :-1]], axis=1)
    return prev.reshape(batch * tiles * SUBLANES, d)


def _ffn_out_kernel(a_ref, wd_ref, h_ref, fg_ref, o_ref):
    kk = pl.program_id(1)

    @pl.when(kk == 0)
    def _():
        o_ref[...] = h_ref[...]

    o_ref[...] += jnp.dot(a_ref[...], wd_ref[...], preferred_element_type=F32)

    @pl.when(kk == pl.num_programs(1) - 1)
    def _():
        h3 = o_ref[...]
        ms = jnp.mean(h3 * h3, axis=-1, keepdims=True)
        o_ref[...] = (h3 * lax.rsqrt(ms + RMS_EPS)) * fg_ref[...]


def _ffn_out(a, wd, h2, fg, tm=512, tk=1024):
    m, ff = a.shape
    d = wd.shape[1]
    return pl.pallas_call(
        _ffn_out_kernel,
        grid=(m // tm, ff // tk),
        in_specs=[pl.BlockSpec((tm, tk), lambda i, k: (i, k)),
                  pl.BlockSpec((tk, d), lambda i, k: (k, 0)),
                  pl.BlockSpec((tm, d), lambda i, k: (i, 0), pipeline_mode=pl.Buffered(1)),
                  pl.BlockSpec((1, d), lambda i, k: (0, 0))],
        out_specs=pl.BlockSpec((tm, d), lambda i, k: (i, 0)),
        out_shape=jax.ShapeDtypeStruct((m, d), F32),
        compiler_params=_cparams(("parallel", "arbitrary")),
    )(a, wd, h2, fg.reshape(1, d))


def _layer(x2, mem2, rel_bias, batch, seq, mem_len, norm_mix_g, w_in, gla_w_gate2,
           gla_b_gate, gla_norm_g, w_out, norm_xattn_g, mem_norm_g, w_xq, w_xk, w_xv,
           w_xo, norm_ffn_g, w_ffn_gate, w_ffn_up, ffn_conv_w, ffn_conv_b, w_ffn_down,
           out_g):
    w_gqkv = w_in[:, _W_GQKV:_W_GLR].astype(BF16)
    w_glr = jnp.pad(w_in[:, _W_GLR:_W_GR], ((0, 0), (0, LANES - GLA_LOWRANK))).astype(BF16)
    w_gr = w_in[:, _W_GR:_W_DQKV].astype(BF16)
    w_dqkv = w_in[:, _W_DQKV:_W_END].astype(BF16)
    w2_pad = jnp.pad(gla_w_gate2, ((0, LANES - GLA_LOWRANK), (0, 0)))
    ff_pad = D_FF_PAD - D_FF
    wg = _cast_pad_cols(w_ffn_gate, D_FF_PAD)
    wu = _cast_pad_cols(w_ffn_up, D_FF_PAD)
    wd = _cast_pad_rows(w_ffn_down, D_FF_PAD)
    cw = jnp.pad(ffn_conv_w, ((0, 0), (0, ff_pad)))
    cb = jnp.pad(ffn_conv_b, ((0, ff_pad),)).reshape(1, D_FF_PAD)
    w_out_b = w_out.astype(BF16)
    w_kv = jnp.concatenate([w_xk, w_xv], axis=1).astype(BF16)

    hn = _rmsnorm(x2, norm_mix_g)
    g_qkv = _matmul(hn, w_gqkv, 1024, 1024, BF16)
    g_r = _matmul(hn, w_gr, 1024, 1024, BF16)
    d_qkv = _matmul(hn, w_dqkv, 1024, 1024, F32)
    glr = _matmul(hn, w_glr, 1024, LANES, F32)
    o_gla = _gla(g_qkv, g_r, glr, w2_pad, gla_b_gate.reshape(1, -1),
                 gla_norm_g.reshape(1, -1), batch, seq)
    o_dil = _dilated(d_qkv.reshape(batch, seq, 3 * DIL_HEADS * DIL_HEAD_DIM),
                     _dil_bias(rel_bias), batch, seq)
    o_dil = o_dil.reshape(batch * seq, DIL_HEADS * DIL_HEAD_DIM)
    half = GLA_HEADS * GLA_DV
    h1 = _mix_out(o_gla, o_dil, w_out_b[:half], w_out_b[half:], x2)

    memn = _rmsnorm(mem2, mem_norm_g)
    kv = _matmul(memn, w_kv, 512, 512, BF16)
    kx = kv[:, :XATTN_WIDTH].reshape(batch, mem_len, XATTN_WIDTH)
    vx = kv[:, XATTN_WIDTH:].reshape(batch, mem_len, XATTN_WIDTH)
    h2, hn3 = _xattn(h1, norm_xattn_g, w_xq.astype(BF16), kx, vx, w_xo.astype(BF16),
                     norm_ffn_g, seq)

    halo_x = _ffn_halo_rows(hn3, batch, seq)
    halo_g = _matmul(halo_x, wg, halo_x.shape[0], 512, F32)
    halo_g = halo_g.reshape(-1, SUBLANES, D_FF_PAD)
    act = _ffn_in(hn3, wg, wu, halo_g, cw, cb)
    return _ffn_out(act, wd, h2, out_g)


def kernel(x, mem, rel_bias, norm_mix_g, w_in, gla_w_gate2, gla_b_gate, gla_norm_g, w_out,
           norm_xattn_g, mem_norm_g, w_xq, w_xk, w_xv, w_xo, norm_ffn_g, w_ffn_gate,
           w_ffn_up, ffn_conv_w, ffn_conv_b, w_ffn_down, final_norm_g):
    batch, seq, d = x.shape
    mem_len = mem.shape[1]
    depth = w_in.shape[0]
    assert depth == 1, "the fused final rmsnorm assumes a single layer"
    out = _layer(x.reshape(batch * seq, d), mem.reshape(batch * mem_len, d), rel_bias,
                 batch, seq, mem_len, norm_mix_g[0], w_in[0], gla_w_gate2[0],
                 gla_b_gate[0], gla_norm_g[0], w_out[0], norm_xattn_g[0], mem_norm_g[0],
                 w_xq[0], w_xk[0], w_xv[0], w_xo[0], norm_ffn_g[0], w_ffn_gate[0],
                 w_ffn_up[0], ffn_conv_w[0], ffn_conv_b[0], w_ffn_down[0], final_norm_g)
    return out.reshape(batch, seq, d)
```

```python
import functools
import math

import numpy as np
import jax
import jax.numpy as jnp
from jax import lax
from jax.experimental import pallas as pl
from jax.experimental.pallas import tpu as pltpu

F32 = jnp.float32
BF16 = jnp.bfloat16

D_MODEL = 4096
RMS_EPS = 1e-6
GLA_HEADS = 4
GLA_DV = 512
GLA_DK = 256
GLA_LOWRANK = 16
GLA_TAU = 16.0
GLA_CHUNK = 64
DIL_HEAD_DIM = 128
DIL_HEADS = 16
DIL_CONFIGS = ((128, 1), (512, 4), (2048, 16))
DIL_STEPS = 128
REL_BUCKETS = 32
REL_MAX_DIST = 2048
XATTN_HEADS = 4
XATTN_HEAD_DIM = 128
XATTN_WIDTH = 512
D_FF = 11008
CONV_WIDTH = 3
NEG_INF = -1e30

LANES = 128
SUBLANES = 8
VMEM_LIMIT = 56 * 1024 * 1024

D_FF_PAD = 11264
FFN_BM = 1024

_W_GQKV = 0
_W_GLR = 4096
_W_GR = 4112
_W_DQKV = 6160
_W_END = 12304


def _cparams(sem):
    return pltpu.CompilerParams(dimension_semantics=sem, vmem_limit_bytes=VMEM_LIMIT)


def _rmsnorm_kernel(x_ref, g_ref, o_ref):
    x = x_ref[...]
    ms = jnp.mean(x * x, axis=-1, keepdims=True)
    o_ref[...] = ((x * lax.rsqrt(ms + RMS_EPS)) * g_ref[...]).astype(o_ref.dtype)


def _rmsnorm(x, g, tm=256):
    m, d = x.shape
    return pl.pallas_call(
        _rmsnorm_kernel,
        grid=(m // tm,),
        in_specs=[pl.BlockSpec((tm, d), lambda i: (i, 0)),
                  pl.BlockSpec((1, d), lambda i: (0, 0))],
        out_specs=pl.BlockSpec((tm, d), lambda i: (i, 0)),
        out_shape=jax.ShapeDtypeStruct((m, d), BF16),
        compiler_params=_cparams(("parallel",)),
    )(x, g.reshape(1, d))


def _cast_pad_cols_kernel(w_ref, o_ref):
    n = w_ref.shape[1]
    o_ref[:, :n] = w_ref[...].astype(o_ref.dtype)
    o_ref[:, n:] = jnp.zeros((o_ref.shape[0], o_ref.shape[1] - n), o_ref.dtype)


def _cast_pad_cols(w, n_pad, tm=256):
    k, n = w.shape
    return pl.pallas_call(
        _cast_pad_cols_kernel,
        grid=(k // tm,),
        in_specs=[pl.BlockSpec((tm, n), lambda i: (i, 0))],
        out_specs=pl.BlockSpec((tm, n_pad), lambda i: (i, 0)),
        out_shape=jax.ShapeDtypeStruct((k, n_pad), BF16),
        compiler_params=_cparams(("parallel",)),
    )(w)


def _cast_pad_rows_kernel(w_ref, o_ref, *, n_valid):
    valid = pl.program_id(0) < n_valid
    o_ref[...] = jnp.where(valid, w_ref[...], 0.0).astype(o_ref.dtype)


def _cast_pad_rows(w, k_pad, tm=256):
    k, n = w.shape
    n_valid = k // tm
    return pl.pallas_call(
        functools.partial(_cast_pad_rows_kernel, n_valid=n_valid),
        grid=(k_pad // tm,),
        in_specs=[pl.BlockSpec((tm, n), lambda i: (jnp.minimum(i, n_valid - 1), 0))],
        out_specs=pl.BlockSpec((tm, n), lambda i: (i, 0)),
        out_shape=jax.ShapeDtypeStruct((k_pad, n), BF16),
        compiler_params=_cparams(("parallel",)),
    )(w)


def _mm_kernel(x_ref, w_ref, o_ref):
    o_ref[...] = jnp.dot(x_ref[...], w_ref[...],
                         preferred_element_type=F32).astype(o_ref.dtype)


def _matmul(x, w, bm, bn, out_dtype):
    m, k = x.shape
    n = w.shape[1]
    return pl.pallas_call(
        _mm_kernel,
        grid=(m // bm, n // bn),
        in_specs=[pl.BlockSpec((bm, k), lambda i, j: (i, 0)),
                  pl.BlockSpec((k, bn), lambda i, j: (0, j))],
        out_specs=pl.BlockSpec((bm, bn), lambda i, j: (i, j)),
        out_shape=jax.ShapeDtypeStruct((m, n), out_dtype),
        compiler_params=_cparams(("parallel", "parallel")),
    )(x, w)


GLA_T = 512
GLA_GROUP = 4


def _split_bf16(x):
    hi = x.astype(BF16)
    lo = (x - hi.astype(F32)).astype(BF16)
    return hi, lo


def _gla_gate_kernel(glr_ref, w2_ref, b2_ref, tri_ref, b_ref):
    x_hi, x_lo = _split_bf16(glr_ref[...])
    w_hi, w_lo = _split_bf16(w2_ref[...])
    z = (jnp.dot(x_hi, w_hi, preferred_element_type=F32)
         + jnp.dot(x_lo, w_hi, preferred_element_type=F32)
         + jnp.dot(x_hi, w_lo, preferred_element_type=F32)) + b2_ref[...]
    log_sig = jnp.minimum(z, 0.0) - jnp.log(1.0 + jnp.exp(-jnp.abs(z)))
    g_hi, g_lo = _split_bf16(log_sig / GLA_TAU)
    tri = tri_ref[...]
    b_ref[...] = (jnp.dot(tri, g_hi, preferred_element_type=F32)
                  + jnp.dot(tri, g_lo, preferred_element_type=F32))


def _chunk_tril(t):
    idx = np.arange(t)
    same = (idx[:, None] // GLA_CHUNK) == (idx[None, :] // GLA_CHUNK)
    return jnp.asarray((same & (idx[:, None] >= idx[None, :])).astype(np.float32), BF16)


def _gla_gate(glr, w2_pad, b2, t=GLA_T):
    m = glr.shape[0]
    n = w2_pad.shape[1]
    return pl.pallas_call(
        _gla_gate_kernel,
        grid=(m // t,),
        in_specs=[pl.BlockSpec((t, LANES), lambda i: (i, 0)),
                  pl.BlockSpec((LANES, n), lambda i: (0, 0)),
                  pl.BlockSpec((1, n), lambda i: (0, 0)),
                  pl.BlockSpec((t, t), lambda i: (0, 0))],
        out_specs=pl.BlockSpec((t, n), lambda i: (i, 0)),
        out_shape=jax.ShapeDtypeStruct((m, n), F32),
        compiler_params=_cparams(("parallel",)),
    )(glr, w2_pad, b2, _chunk_tril(t))


def _gla_kernel(q_ref, k_ref, v_ref, r_ref, b_ref, gn_ref, o_ref, state_ref):
    c_sz = GLA_CHUNK

    @pl.when(pl.program_id(2) == 0)
    def _():
        state_ref[...] = jnp.zeros_like(state_ref)

    row = lax.broadcasted_iota(jnp.int32, (c_sz, c_sz), 0)
    col = lax.broadcasted_iota(jnp.int32, (c_sz, c_sz), 1)
    tril = row >= col
    gn = gn_ref[...]
    nt = (((1,), (1,)), ((), ()))
    tn = (((0,), (0,)), ((), ()))

    def group(gi, carry):
        st = state_ref[...]
        outs = []
        for u in range(GLA_GROUP):
            rows = pl.ds(pl.multiple_of((gi * GLA_GROUP + u) * c_sz, c_sz), c_sz)
            b = b_ref[rows, :]
            b_last = b[c_sz - 1:c_sz, :]
            b_mid = b[c_sz // 2:c_sz // 2 + 1, :]
            q = q_ref[rows, :].astype(F32) * (GLA_DK ** -0.5)
            k = k_ref[rows, :].astype(F32)
            v = v_ref[rows, :]
            q_start = (q * jnp.exp(b)).astype(BF16)
            k_end = (k * jnp.exp(b_last - b)).astype(BF16)
            q_mid = (q * jnp.exp(b - b_mid)).astype(BF16)
            k_mid = (k * jnp.exp(b_mid - b)).astype(BF16)
            o_inter = lax.dot_general(q_start, st.astype(BF16), nt,
                                      preferred_element_type=F32)
            att = lax.dot_general(q_mid, k_mid, nt, preferred_element_type=F32)
            att = jnp.where(tril, att, 0.0)
            o = o_inter + jnp.dot(att.astype(BF16), v, preferred_element_type=F32)
            kv_t = lax.dot_general(v, k_end, tn, preferred_element_type=F32)
            st = st * jnp.exp(b_last) + kv_t
            ms = jnp.mean(o * o, axis=-1, keepdims=True)
            on = (o * lax.rsqrt(ms + RMS_EPS)) * gn
            r = r_ref[rows, :].astype(F32)
            gate = r / (1.0 + jnp.exp(-r))
            outs.append((rows, (on * gate).astype(o_ref.dtype)))
        for rows, o in outs:
            o_ref[rows, :] = o
        state_ref[...] = st
        return carry

    lax.fori_loop(0, GLA_T // (c_sz * GLA_GROUP), group, 0)


def _gla(qkv, gr, b_cum, gn, batch, seq):
    t = GLA_T
    nt_ = seq // t
    m = batch * seq
    kb = (GLA_HEADS * GLA_DK) // GLA_DK
    vb = (2 * GLA_HEADS * GLA_DK) // GLA_DV
    return pl.pallas_call(
        _gla_kernel,
        grid=(batch, GLA_HEADS, nt_),
        in_specs=[
            pl.BlockSpec((t, GLA_DK), lambda b, h, s: (b * nt_ + s, h)),
            pl.BlockSpec((t, GLA_DK), lambda b, h, s: (b * nt_ + s, kb + h)),
            pl.BlockSpec((t, GLA_DV), lambda b, h, s: (b * nt_ + s, vb + h)),
            pl.BlockSpec((t, GLA_DV), lambda b, h, s: (b * nt_ + s, h)),
            pl.BlockSpec((t, GLA_DK), lambda b, h, s: (b * nt_ + s, h)),
            pl.BlockSpec((1, GLA_DV), lambda b, h, s: (0, 0)),
        ],
        out_specs=pl.BlockSpec((t, GLA_DV), lambda b, h, s: (b * nt_ + s, h)),
        out_shape=jax.ShapeDtypeStruct((m, GLA_HEADS * GLA_DV), BF16),
        scratch_shapes=[pltpu.VMEM((GLA_DV, GLA_DK), F32)],
        compiler_params=_cparams(("parallel", "parallel", "arbitrary")),
    )(qkv, qkv, qkv, gr, b_cum, gn)


def _t5_bucket_np(dist):
    max_exact = REL_BUCKETS // 2
    d_f = np.maximum(dist, 1).astype(np.float32)
    large = max_exact + (np.log(d_f / np.float32(max_exact))
                         / np.float32(math.log(REL_MAX_DIST / max_exact))
                         * np.float32(REL_BUCKETS - max_exact)).astype(np.int32)
    large = np.minimum(large, REL_BUCKETS - 1)
    return np.where(dist < max_exact, dist, large)


def _dil_bucket_index():
    steps = DIL_STEPS
    qi = np.arange(steps)[:, None]
    kj = np.arange(2 * steps)[None, :]
    rel = qi + steps - kj
    band = (rel >= 0) & (rel <= steps)
    out = []
    for _, dil in DIL_CONFIGS:
        bucket = _t5_bucket_np(np.clip(rel, 0, steps) * dil)
        out.append(np.where(band, bucket, REL_BUCKETS))
    return np.stack(out).astype(np.int32)


def _dil_bias_kernel(relb_ref, idx_ref, o_ref):
    head = pl.program_id(1)
    idx = idx_ref[0]
    bias = jnp.full(idx.shape, NEG_INF, F32)
    for bkt in range(REL_BUCKETS):
        bias = jnp.where(idx == bkt, relb_ref[bkt, head], bias)
    o_ref[0, 0] = bias


def _dil_bias(rel_bias):
    steps = DIL_STEPS
    ncfg = len(DIL_CONFIGS)
    idx = jnp.asarray(_dil_bucket_index())
    return pl.pallas_call(
        _dil_bias_kernel,
        grid=(ncfg, DIL_HEADS),
        in_specs=[pl.BlockSpec(memory_space=pltpu.SMEM),
                  pl.BlockSpec((1, steps, 2 * steps), lambda c, h: (c, 0, 0))],
        out_specs=pl.BlockSpec((1, 1, steps, 2 * steps), lambda c, h: (c, h, 0, 0)),
        out_shape=jax.ShapeDtypeStruct((ncfg, DIL_HEADS, steps, 2 * steps), F32),
        compiler_params=_cparams(("parallel", "parallel")),
    )(rel_bias, idx)


DIL_GROUP = 8


def _dil_kernel(bias_ref, q_ref, k_ref, v_ref, o_ref, m_sc, l_sc, acc_sc, *, seq):
    steps = DIL_STEPS
    e = DIL_HEAD_DIM
    scale = e ** -0.5
    nt = (((1,), (1,)), ((), ()))
    n_cfg = len(DIL_CONFIGS)

    def rows_of(start, size, dil):
        if dil == 1:
            return pl.ds(start, size)
        return pl.ds(start, size, stride=dil)

    def attend(c, dil, blocks, merge):
        q_rows = [rows_of(q_start, steps, dil) for q_start, _, _, _ in blocks]
        k_rows = [rows_of(k_start, n_keys, dil) for _, k_start, n_keys, _ in blocks]
        logits = []
        for qr, kr, (_, _, _, bias) in zip(q_rows, k_rows, blocks):
            q = q_ref[0, qr, :].astype(BF16)
            k = k_ref[0, kr, :].astype(BF16)
            logits.append(lax.dot_general(q, k, nt, preferred_element_type=F32) * scale + bias)
        stats = []
        for s in logits:
            m = jnp.max(s, axis=-1, keepdims=True)
            p = jnp.exp(s - m)
            stats.append((m, jnp.sum(p, axis=-1, keepdims=True), p.astype(BF16)))
        pvs = [jnp.dot(p, v_ref[0, kr, :].astype(BF16), preferred_element_type=F32)
               for (_, _, p), kr in zip(stats, k_rows)]
        if not merge:
            for qr, (m, l, _), pv in zip(q_rows, stats, pvs):
                m_sc[c - 1, qr, :] = jnp.broadcast_to(m, (steps, e))
                l_sc[c - 1, qr, :] = jnp.broadcast_to(l, (steps, e))
                acc_sc[c - 1, qr, :] = pv
            return
        outs = []
        for qr, (m, l, _), pv in zip(q_rows, stats, pvs):
            ms = [m] + [m_sc[i, qr, :] for i in range(n_cfg - 1)]
            ls = [l] + [l_sc[i, qr, :] for i in range(n_cfg - 1)]
            accs = [pv] + [acc_sc[i, qr, :] for i in range(n_cfg - 1)]
            m_max = functools.reduce(jnp.maximum, ms)
            num = None
            den = None
            for m_i, l_i, acc_i in zip(ms, ls, accs):
                wgt = jnp.exp(m_i - m_max)
                num = wgt * acc_i if num is None else num + wgt * acc_i
                den = wgt * l_i if den is None else den + wgt * l_i
            outs.append((qr, (num / den).astype(o_ref.dtype)))
        for qr, o in outs:
            o_ref[0, qr, :] = o

    for c in reversed(range(n_cfg)):
        dil = DIL_CONFIGS[c][1]
        nb = seq // dil // steps
        span = steps * dil

        def first(r, c=c):
            return (r, r, steps, bias_ref[c, 0, :, steps:2 * steps])

        def later(j, c=c, dil=dil, nb=nb, span=span):
            if dil == 1:
                q_start = (1 + j) * span
                if not isinstance(q_start, int):
                    q_start = pl.multiple_of(q_start, span)
                return (q_start, q_start - span, 2 * steps, bias_ref[c, 0])
            r = j // (nb - 1)
            n = 1 + j % (nb - 1)
            return (n * span + r, (n - 1) * span + r, 2 * steps, bias_ref[c, 0])

        for make, count in ((first, dil), (later, dil * (nb - 1))):
            full, rest = divmod(count, DIL_GROUP)

            def group(gi, carry, c=c, dil=dil, make=make):
                attend(c, dil, [make(gi * DIL_GROUP + u) for u in range(DIL_GROUP)], c == 0)
                return carry

            if full:
                lax.fori_loop(0, full, group, 0)
            if rest:
                attend(c, dil, [make(full * DIL_GROUP + u) for u in range(rest)], c == 0)


def _dilated(qkv3, bias, batch, seq):
    e = DIL_HEAD_DIM
    steps = DIL_STEPS
    ncfg = len(DIL_CONFIGS)
    assert DIL_CONFIGS[0][1] == 1, "the config that writes the output rows must be undilated"
    return pl.pallas_call(
        functools.partial(_dil_kernel, seq=seq),
        grid=(batch, DIL_HEADS),
        in_specs=[
            pl.BlockSpec((ncfg, 1, steps, 2 * steps), lambda b, h: (0, h, 0, 0)),
            pl.BlockSpec((1, seq, e), lambda b, h: (b, 0, h)),
            pl.BlockSpec((1, seq, e), lambda b, h: (b, 0, DIL_HEADS + h)),
            pl.BlockSpec((1, seq, e), lambda b, h: (b, 0, 2 * DIL_HEADS + h)),
        ],
        out_specs=pl.BlockSpec((1, seq, e), lambda b, h: (b, 0, h)),
        out_shape=jax.ShapeDtypeStruct((batch, seq, DIL_HEADS * e), BF16),
        scratch_shapes=[pltpu.VMEM((ncfg - 1, seq, e), F32),
                        pltpu.VMEM((ncfg - 1, seq, e), F32),
                        pltpu.VMEM((ncfg - 1, seq, e), F32)],
        compiler_params=_cparams(("parallel", "parallel")),
    )(bias, qkv3, qkv3, qkv3)


def _mix_out_kernel(a_ref, b_ref, wa_ref, wb_ref, x_ref, o_ref):
    acc = jnp.dot(a_ref[...], wa_ref[...], preferred_element_type=F32)
    acc = acc + jnp.dot(b_ref[...], wb_ref[...], preferred_element_type=F32)
    o_ref[...] = x_ref[...] + acc


def _mix_out(o_gla, o_dil, w_a, w_b, x, bm=512, bn=1024):
    m, ka = o_gla.shape
    kb = o_dil.shape[1]
    n = w_a.shape[1]
    return pl.pallas_call(
        _mix_out_kernel,
        grid=(m // bm, n // bn),
        in_specs=[pl.BlockSpec((bm, ka), lambda i, j: (i, 0)),
                  pl.BlockSpec((bm, kb), lambda i, j: (i, 0)),
                  pl.BlockSpec((ka, bn), lambda i, j: (0, j)),
                  pl.BlockSpec((kb, bn), lambda i, j: (0, j)),
                  pl.BlockSpec((bm, bn), lambda i, j: (i, j))],
        out_specs=pl.BlockSpec((bm, bn), lambda i, j: (i, j)),
        out_shape=jax.ShapeDtypeStruct((m, n), F32),
        compiler_params=_cparams(("parallel", "parallel")),
    )(o_gla, o_dil, w_a, w_b, x)


def _xattn_kernel(h_ref, gx_ref, wq_ref, k_ref, v_ref, wo_ref, gf_ref, h2_ref, hn_ref):
    e = XATTN_HEAD_DIM
    nt = (((1,), (1,)), ((), ()))
    h = h_ref[...]
    ms = jnp.mean(h * h, axis=-1, keepdims=True)
    hn = ((h * lax.rsqrt(ms + RMS_EPS)) * gx_ref[...]).astype(BF16)
    q = jnp.dot(hn, wq_ref[...], preferred_element_type=F32).astype(BF16)
    outs = []
    for hh in range(XATTN_HEADS):
        qh = q[:, hh * e:(hh + 1) * e]
        kh = k_ref[0, :, hh * e:(hh + 1) * e]
        vh = v_ref[0, :, hh * e:(hh + 1) * e]
        s = lax.dot_general(qh, kh, nt, preferred_element_type=F32) * (e ** -0.5)
        m = jnp.max(s, axis=-1, keepdims=True)
        p = jnp.exp(s - m)
        p = p / jnp.sum(p, axis=-1, keepdims=True)
        outs.append(jnp.dot(p.astype(BF16), vh, preferred_element_type=F32))
    o = jnp.concatenate(outs, axis=1).astype(BF16)
    h2 = h + jnp.dot(o, wo_ref[...], preferred_element_type=F32)
    h2_ref[...] = h2
    ms2 = jnp.mean(h2 * h2, axis=-1, keepdims=True)
    hn_ref[...] = ((h2 * lax.rsqrt(ms2 + RMS_EPS)) * gf_ref[...]).astype(hn_ref.dtype)


def _xattn(h1, gx, wq, kx, vx, wo, gf, seq, tm=256):
    m, d = h1.shape
    mem_len = kx.shape[1]
    per_seq = seq // tm
    return pl.pallas_call(
        _xattn_kernel,
        grid=(m // tm,),
        in_specs=[pl.BlockSpec((tm, d), lambda i: (i, 0)),
                  pl.BlockSpec((1, d), lambda i: (0, 0)),
                  pl.BlockSpec((d, XATTN_WIDTH), lambda i: (0, 0)),
                  pl.BlockSpec((1, mem_len, XATTN_WIDTH), lambda i: (i // per_seq, 0, 0)),
                  pl.BlockSpec((1, mem_len, XATTN_WIDTH), lambda i: (i // per_seq, 0, 0)),
                  pl.BlockSpec((XATTN_WIDTH, d), lambda i: (0, 0)),
                  pl.BlockSpec((1, d), lambda i: (0, 0))],
        out_specs=[pl.BlockSpec((tm, d), lambda i: (i, 0)),
                   pl.BlockSpec((tm, d), lambda i: (i, 0))],
        out_shape=[jax.ShapeDtypeStruct((m, d), F32),
                   jax.ShapeDtypeStruct((m, d), BF16)],
        compiler_params=_cparams(("parallel",)),
    )(h1, gx.reshape(1, d), wq, kx, vx, wo, gf.reshape(1, d))


def _ffn_in_kernel(x_ref, wg_ref, wu_ref, halo_ref, cw_ref, cb_ref, a_ref):
    x = x_ref[...]
    g = jnp.dot(x, wg_ref[...], preferred_element_type=F32)
    u = jnp.dot(x, wu_ref[...], preferred_element_type=F32)
    halo = halo_ref[0]
    prev1 = halo[SUBLANES - 1:SUBLANES, :]
    prev2 = halo[SUBLANES - 2:SUBLANES - 1, :]
    row = lax.broadcasted_iota(jnp.int32, (SUBLANES, g.shape[1]), 0)
    r1 = pltpu.roll(g, 1, axis=0)
    r2 = pltpu.roll(g, 2, axis=0)
    head1 = jnp.where(row == 0, prev1, r1[:SUBLANES])
    head2 = jnp.where(row == 0, prev2, jnp.where(row == 1, prev1, r2[:SUBLANES]))
    g_m1 = jnp.concatenate([head1, r1[SUBLANES:]], axis=0)
    g_m2 = jnp.concatenate([head2, r2[SUBLANES:]], axis=0)
    cw = cw_ref[...]
    y = cb_ref[...] + g_m2 * cw[0:1, :]
    y = y + g_m1 * cw[1:2, :]
    y = y + g * cw[2:3, :]
    a_ref[...] = ((y / (1.0 + jnp.exp(-y))) * u).astype(a_ref.dtype)


def _ffn_in(x, wg, wu, halo_g, cw, cb, bm=FFN_BM, bn=512):
    m, k = x.shape
    n = wg.shape[1]
    return pl.pallas_call(
        _ffn_in_kernel,
        grid=(m // bm, n // bn),
        in_specs=[pl.BlockSpec((bm, k), lambda i, j: (i, 0)),
                  pl.BlockSpec((k, bn), lambda i, j: (0, j)),
                  pl.BlockSpec((k, bn), lambda i, j: (0, j)),
                  pl.BlockSpec((1, SUBLANES, bn), lambda i, j: (i, 0, j)),
                  pl.BlockSpec((CONV_WIDTH, bn), lambda i, j: (0, j)),
                  pl.BlockSpec((1, bn), lambda i, j: (0, j))],
        out_specs=pl.BlockSpec((bm, bn), lambda i, j: (i, j)),
        out_shape=jax.ShapeDtypeStruct((m, n), BF16),
        compiler_params=_cparams(("parallel", "parallel")),
    )(x, wg, wu, halo_g, cw, cb)


def _ffn_halo_rows(hn, batch, seq, bm=FFN_BM):
    d = hn.shape[1]
    tiles = seq // bm
    tail = hn.reshape(batch, tiles, bm, d)[:, :, bm - SUBLANES:, :]
    prev = jnp.concatenate([jnp.zeros_like(tail[:, :1]), tail[:, :-1]], axis=1)
    return prev.reshape(batch * tiles * SUBLANES, d)


FFN_OUT_COLS = 1024
FFN_RES_COLS = 256
FFN_NORM_ROWS = 128


def _ffn_out_kernel(a_ref, wd_ref, h_ref, fg_ref, o_ref, *, n_res):
    kk = pl.program_id(1)
    d = o_ref.shape[1]

    @pl.when(kk == 0)
    def _():
        o_ref[...] = jnp.zeros_like(o_ref)

    a = a_ref[...]
    for c0 in range(0, d, FFN_OUT_COLS):
        cols = slice(c0, c0 + FFN_OUT_COLS)
        o_ref[:, cols] += jnp.dot(a, wd_ref[:, cols], preferred_element_type=F32)

    for c in range(n_res):
        @pl.when(kk == c)
        def _(c=c):
            cols = slice(c * FFN_RES_COLS, (c + 1) * FFN_RES_COLS)
            o_ref[:, cols] += h_ref[...]

    @pl.when(kk == pl.num_programs(1) - 1)
    def _():
        fg = fg_ref[...]
        for r0 in range(0, o_ref.shape[0], FFN_NORM_ROWS):
            rows = slice(r0, r0 + FFN_NORM_ROWS)
            h3 = o_ref[rows, :]
            ms = jnp.mean(h3 * h3, axis=-1, keepdims=True)
            o_ref[rows, :] = (h3 * lax.rsqrt(ms + RMS_EPS)) * fg


def _ffn_out(a, wd, h2, fg, tm=1024, tk=512):
    m, ff = a.shape
    d = wd.shape[1]
    n_res = d // FFN_RES_COLS
    assert ff // tk >= n_res, "one residual slab per contraction step"
    return pl.pallas_call(
        functools.partial(_ffn_out_kernel, n_res=n_res),
        grid=(m // tm, ff // tk),
        in_specs=[pl.BlockSpec((tm, tk), lambda i, k: (i, k)),
                  pl.BlockSpec((tk, d), lambda i, k: (k, 0)),
                  pl.BlockSpec((tm, FFN_RES_COLS), lambda i, k: (i, jnp.minimum(k, n_res - 1))),
                  pl.BlockSpec((1, d), lambda i, k: (0, 0))],
        out_specs=pl.BlockSpec((tm, d), lambda i, k: (i, 0)),
        out_shape=jax.ShapeDtypeStruct((m, d), F32),
        compiler_params=_cparams(("parallel", "arbitrary")),
    )(a, wd, h2, fg.reshape(1, d))


def _layer(x2, mem2, rel_bias, batch, seq, mem_len, norm_mix_g, w_in, gla_w_gate2,
           gla_b_gate, gla_norm_g, w_out, norm_xattn_g, mem_norm_g, w_xq, w_xk, w_xv,
           w_xo, norm_ffn_g, w_ffn_gate, w_ffn_up, ffn_conv_w, ffn_conv_b, w_ffn_down,
           out_g):
    w_gqkv = w_in[:, _W_GQKV:_W_GLR].astype(BF16)
    w_glr = jnp.pad(w_in[:, _W_GLR:_W_GR], ((0, 0), (0, LANES - GLA_LOWRANK))).astype(BF16)
    w_gr = w_in[:, _W_GR:_W_DQKV].astype(BF16)
    w_dqkv = w_in[:, _W_DQKV:_W_END].astype(BF16)
    w2_pad = jnp.pad(gla_w_gate2, ((0, LANES - GLA_LOWRANK), (0, 0)))
    ff_pad = D_FF_PAD - D_FF
    wg = _cast_pad_cols(w_ffn_gate, D_FF_PAD)
    wu = _cast_pad_cols(w_ffn_up, D_FF_PAD)
    wd = _cast_pad_rows(w_ffn_down, D_FF_PAD)
    cw = jnp.pad(ffn_conv_w, ((0, 0), (0, ff_pad)))
    cb = jnp.pad(ffn_conv_b, ((0, ff_pad),)).reshape(1, D_FF_PAD)
    w_out_b = w_out.astype(BF16)
    w_kv = jnp.concatenate([w_xk, w_xv], axis=1).astype(BF16)

    hn = _rmsnorm(x2, norm_mix_g)
    g_qkv = _matmul(hn, w_gqkv, 1024, 1024, BF16)
    g_r = _matmul(hn, w_gr, 1024, 1024, BF16)
    d_qkv = _matmul(hn, w_dqkv, 1024, 1024, F32)
    glr = _matmul(hn, w_glr, 1024, LANES, F32)
    b_cum = _gla_gate(glr, w2_pad, gla_b_gate.reshape(1, -1))
    o_gla = _gla(g_qkv, g_r, b_cum, gla_norm_g.reshape(1, -1), batch, seq)
    o_dil = _dilated(d_qkv.reshape(batch, seq, 3 * DIL_HEADS * DIL_HEAD_DIM),
                     _dil_bias(rel_bias), batch, seq)
    o_dil = o_dil.reshape(batch * seq, DIL_HEADS * DIL_HEAD_DIM)
    half = GLA_HEADS * GLA_DV
    h1 = _mix_out(o_gla, o_dil, w_out_b[:half], w_out_b[half:], x2)

    memn = _rmsnorm(mem2, mem_norm_g)
    kv = _matmul(memn, w_kv, 512, 512, BF16)
    kx = kv[:, :XATTN_WIDTH].reshape(batch, mem_len, XATTN_WIDTH)
    vx = kv[:, XATTN_WIDTH:].reshape(batch, mem_len, XATTN_WIDTH)
    h2, hn3 = _xattn(h1, norm_xattn_g, w_xq.astype(BF16), kx, vx, w_xo.astype(BF16),
                     norm_ffn_g, seq)

    halo_x = _ffn_halo_rows(hn3, batch, seq)
    halo_g = _matmul(halo_x, wg, halo_x.shape[0], 512, F32)
    halo_g = halo_g.reshape(-1, SUBLANES, D_FF_PAD)
    act = _ffn_in(hn3, wg, wu, halo_g, cw, cb)
    return _ffn_out(act, wd, h2, out_g)


def kernel(x, mem, rel_bias, norm_mix_g, w_in, gla_w_gate2, gla_b_gate, gla_norm_g, w_out,
           norm_xattn_g, mem_norm_g, w_xq, w_xk, w_xv, w_xo, norm_ffn_g, w_ffn_gate,
           w_ffn_up, ffn_conv_w, ffn_conv_b, w_ffn_down, final_norm_g):
    batch, seq, d = x.shape
    mem_len = mem.shape[1]
    depth = w_in.shape[0]
    assert depth == 1, "the fused final rmsnorm assumes a single layer"
    out = _layer(x.reshape(batch * seq, d), mem.reshape(batch * mem_len, d), rel_bias,
                 batch, seq, mem_len, norm_mix_g[0], w_in[0], gla_w_gate2[0],
                 gla_b_gate[0], gla_norm_g[0], w_out[0], norm_xattn_g[0], mem_norm_g[0],
                 w_xq[0], w_xk[0], w_xv[0], w_xo[0], norm_ffn_g[0], w_ffn_gate[0],
                 w_ffn_up[0], ffn_conv_w[0], ffn_conv_b[0], w_ffn_down[0], final_norm_g)
    return out.reshape(batch, seq, d)
```

```python
import functools
import math

import numpy as np
import jax
import jax.numpy as jnp
from jax import lax
from jax.experimental import pallas as pl
from jax.experimental.pallas import tpu as pltpu

F32 = jnp.float32
BF16 = jnp.bfloat16

D_MODEL = 4096
RMS_EPS = 1e-6
GLA_HEADS = 4
GLA_DV = 512
GLA_DK = 256
GLA_LOWRANK = 16
GLA_TAU = 16.0
GLA_CHUNK = 64
DIL_HEAD_DIM = 128
DIL_HEADS = 16
DIL_CONFIGS = ((128, 1), (512, 4), (2048, 16))
DIL_STEPS = 128
REL_BUCKETS = 32
REL_MAX_DIST = 2048
XATTN_HEADS = 4
XATTN_HEAD_DIM = 128
XATTN_WIDTH = 512
D_FF = 11008
CONV_WIDTH = 3
NEG_INF = -1e30
LOG2E = math.log2(math.e)

LANES = 128
SUBLANES = 8
VMEM_LIMIT = 56 * 1024 * 1024

D_FF_PAD = 11264
FFN_BM = 1024

_W_GQKV = 0
_W_GLR = 4096
_W_GR = 4112
_W_DQKV = 6160
_W_END = 12304


def _cparams(sem):
    return pltpu.CompilerParams(dimension_semantics=sem, vmem_limit_bytes=VMEM_LIMIT)


def _rmsnorm_kernel(x_ref, g_ref, o_ref):
    x = x_ref[...]
    ms = jnp.mean(x * x, axis=-1, keepdims=True)
    o_ref[...] = ((x * lax.rsqrt(ms + RMS_EPS)) * g_ref[...]).astype(o_ref.dtype)


def _rmsnorm(x, g, tm=256):
    m, d = x.shape
    return pl.pallas_call(
        _rmsnorm_kernel,
        grid=(m // tm,),
        in_specs=[pl.BlockSpec((tm, d), lambda i: (i, 0)),
                  pl.BlockSpec((1, d), lambda i: (0, 0))],
        out_specs=pl.BlockSpec((tm, d), lambda i: (i, 0)),
        out_shape=jax.ShapeDtypeStruct((m, d), BF16),
        compiler_params=_cparams(("parallel",)),
    )(x, g.reshape(1, d))


def _cast_pad_cols_kernel(w_ref, o_ref):
    n = w_ref.shape[1]
    o_ref[:, :n] = w_ref[...].astype(o_ref.dtype)
    o_ref[:, n:] = jnp.zeros((o_ref.shape[0], o_ref.shape[1] - n), o_ref.dtype)


def _cast_pad_cols(w, n_pad, tm=256):
    k, n = w.shape
    return pl.pallas_call(
        _cast_pad_cols_kernel,
        grid=(k // tm,),
        in_specs=[pl.BlockSpec((tm, n), lambda i: (i, 0))],
        out_specs=pl.BlockSpec((tm, n_pad), lambda i: (i, 0)),
        out_shape=jax.ShapeDtypeStruct((k, n_pad), BF16),
        compiler_params=_cparams(("parallel",)),
    )(w)


def _cast_pad_rows_kernel(w_ref, o_ref, *, n_valid):
    valid = pl.program_id(0) < n_valid
    o_ref[...] = jnp.where(valid, w_ref[...], 0.0).astype(o_ref.dtype)


def _cast_pad_rows(w, k_pad, tm=256):
    k, n = w.shape
    n_valid = k // tm
    return pl.pallas_call(
        functools.partial(_cast_pad_rows_kernel, n_valid=n_valid),
        grid=(k_pad // tm,),
        in_specs=[pl.BlockSpec((tm, n), lambda i: (jnp.minimum(i, n_valid - 1), 0))],
        out_specs=pl.BlockSpec((tm, n), lambda i: (i, 0)),
        out_shape=jax.ShapeDtypeStruct((k_pad, n), BF16),
        compiler_params=_cparams(("parallel",)),
    )(w)


def _split_w_in_kernel(w_ref, qkv_ref, lr_ref, r_ref, d_ref):
    tm = w_ref.shape[0]
    qkv_ref[...] = w_ref[:, _W_GQKV:_W_GLR].astype(BF16)
    lr_ref[:, :GLA_LOWRANK] = w_ref[:, _W_GLR:_W_GR].astype(BF16)
    lr_ref[:, GLA_LOWRANK:] = jnp.zeros((tm, LANES - GLA_LOWRANK), BF16)
    r_ref[...] = w_ref[:, _W_GR:_W_DQKV].astype(BF16)
    d_ref[...] = w_ref[:, _W_DQKV:_W_END].astype(BF16)


def _split_w_in(w, tm=256):
    k, n = w.shape
    widths = (_W_GLR - _W_GQKV, LANES, _W_DQKV - _W_GR, _W_END - _W_DQKV)
    return pl.pallas_call(
        _split_w_in_kernel,
        grid=(k // tm,),
        in_specs=[pl.BlockSpec((tm, n), lambda i: (i, 0))],
        out_specs=[pl.BlockSpec((tm, c), lambda i: (i, 0)) for c in widths],
        out_shape=[jax.ShapeDtypeStruct((k, c), BF16) for c in widths],
        compiler_params=_cparams(("parallel",)),
    )(w)


def _mm_kernel(x_ref, w_ref, o_ref):
    o_ref[...] = jnp.dot(x_ref[...], w_ref[...],
                         preferred_element_type=F32).astype(o_ref.dtype)


def _matmul(x, w, bm, bn, out_dtype):
    m, k = x.shape
    n = w.shape[1]
    return pl.pallas_call(
        _mm_kernel,
        grid=(m // bm, n // bn),
        in_specs=[pl.BlockSpec((bm, k), lambda i, j: (i, 0)),
                  pl.BlockSpec((k, bn), lambda i, j: (0, j))],
        out_specs=pl.BlockSpec((bm, bn), lambda i, j: (i, j)),
        out_shape=jax.ShapeDtypeStruct((m, n), out_dtype),
        compiler_params=_cparams(("parallel", "parallel")),
    )(x, w)


GLA_T = 512
GLA_GROUP = 4


def _split_bf16(x):
    hi = x.astype(BF16)
    lo = (x - hi.astype(F32)).astype(BF16)
    return hi, lo


def _gla_gate_kernel(glr_ref, w2_ref, b2_ref, tri_ref, b_ref):
    x_hi, x_lo = _split_bf16(glr_ref[...])
    w_hi, w_lo = _split_bf16(w2_ref[...])
    z = (jnp.dot(x_hi, w_hi, preferred_element_type=F32)
         + jnp.dot(x_lo, w_hi, preferred_element_type=F32)
         + jnp.dot(x_hi, w_lo, preferred_element_type=F32)) + b2_ref[...]
    log_sig = jnp.minimum(z, 0.0) - jnp.log(1.0 + jnp.exp(-jnp.abs(z)))
    g_hi, g_lo = _split_bf16(log_sig / GLA_TAU)
    tri = tri_ref[...]
    b_ref[...] = (jnp.dot(tri, g_hi, preferred_element_type=F32)
                  + jnp.dot(tri, g_lo, preferred_element_type=F32))


def _chunk_tril(t):
    idx = np.arange(t)
    same = (idx[:, None] // GLA_CHUNK) == (idx[None, :] // GLA_CHUNK)
    return jnp.asarray((same & (idx[:, None] >= idx[None, :])).astype(np.float32), BF16)


def _gla_gate(glr, w2_pad, b2, t=GLA_T):
    m = glr.shape[0]
    n = w2_pad.shape[1]
    return pl.pallas_call(
        _gla_gate_kernel,
        grid=(m // t,),
        in_specs=[pl.BlockSpec((t, LANES), lambda i: (i, 0)),
                  pl.BlockSpec((LANES, n), lambda i: (0, 0)),
                  pl.BlockSpec((1, n), lambda i: (0, 0)),
                  pl.BlockSpec((t, t), lambda i: (0, 0))],
        out_specs=pl.BlockSpec((t, n), lambda i: (i, 0)),
        out_shape=jax.ShapeDtypeStruct((m, n), F32),
        compiler_params=_cparams(("parallel",)),
    )(glr, w2_pad, b2, _chunk_tril(t))


def _gla_kernel(q_ref, k_ref, v_ref, r_ref, b_ref, gn_ref, o_ref, state_ref):
    c_sz = GLA_CHUNK

    @pl.when(pl.program_id(2) == 0)
    def _():
        state_ref[...] = jnp.zeros_like(state_ref)

    row = lax.broadcasted_iota(jnp.int32, (c_sz, c_sz), 0)
    col = lax.broadcasted_iota(jnp.int32, (c_sz, c_sz), 1)
    tril = row >= col
    gn = gn_ref[...]
    nt = (((1,), (1,)), ((), ()))
    tn = (((0,), (0,)), ((), ()))

    def group(gi, carry):
        st = state_ref[...]
        outs = []
        for u in range(GLA_GROUP):
            rows = pl.ds(pl.multiple_of((gi * GLA_GROUP + u) * c_sz, c_sz), c_sz)
            b = b_ref[rows, :]
            b_last = b[c_sz - 1:c_sz, :]
            b_mid = b[c_sz // 2:c_sz // 2 + 1, :]
            q = q_ref[rows, :].astype(F32) * (GLA_DK ** -0.5)
            k = k_ref[rows, :].astype(F32)
            v = v_ref[rows, :]
            q_start = (q * jnp.exp(b)).astype(BF16)
            k_end = (k * jnp.exp(b_last - b)).astype(BF16)
            q_mid = (q * jnp.exp(b - b_mid)).astype(BF16)
            k_mid = (k * jnp.exp(b_mid - b)).astype(BF16)
            o_inter = lax.dot_general(q_start, st.astype(BF16), nt,
                                      preferred_element_type=F32)
            att = lax.dot_general(q_mid, k_mid, nt, preferred_element_type=F32)
            att = jnp.where(tril, att, 0.0)
            o = o_inter + jnp.dot(att.astype(BF16), v, preferred_element_type=F32)
            kv_t = lax.dot_general(v, k_end, tn, preferred_element_type=F32)
            st = st * jnp.exp(b_last) + kv_t
            ms = jnp.mean(o * o, axis=-1, keepdims=True)
            on = (o * lax.rsqrt(ms + RMS_EPS)) * gn
            r = r_ref[rows, :].astype(F32)
            gate = r / (1.0 + jnp.exp(-r))
            outs.append((rows, (on * gate).astype(o_ref.dtype)))
        for rows, o in outs:
            o_ref[rows, :] = o
        state_ref[...] = st
        return carry

    lax.fori_loop(0, GLA_T // (c_sz * GLA_GROUP), group, 0)


def _gla(qkv, gr, b_cum, gn, batch, seq):
    t = GLA_T
    nt_ = seq // t
    m = batch * seq
    kb = (GLA_HEADS * GLA_DK) // GLA_DK
    vb = (2 * GLA_HEADS * GLA_DK) // GLA_DV
    return pl.pallas_call(
        _gla_kernel,
        grid=(batch, GLA_HEADS, nt_),
        in_specs=[
            pl.BlockSpec((t, GLA_DK), lambda b, h, s: (b * nt_ + s, h)),
            pl.BlockSpec((t, GLA_DK), lambda b, h, s: (b * nt_ + s, kb + h)),
            pl.BlockSpec((t, GLA_DV), lambda b, h, s: (b * nt_ + s, vb + h)),
            pl.BlockSpec((t, GLA_DV), lambda b, h, s: (b * nt_ + s, h)),
            pl.BlockSpec((t, GLA_DK), lambda b, h, s: (b * nt_ + s, h)),
            pl.BlockSpec((1, GLA_DV), lambda b, h, s: (0, 0)),
        ],
        out_specs=pl.BlockSpec((t, GLA_DV), lambda b, h, s: (b * nt_ + s, h)),
        out_shape=jax.ShapeDtypeStruct((m, GLA_HEADS * GLA_DV), BF16),
        scratch_shapes=[pltpu.VMEM((GLA_DV, GLA_DK), F32)],
        compiler_params=_cparams(("parallel", "parallel", "arbitrary")),
    )(qkv, qkv, qkv, gr, b_cum, gn)


def _t5_bucket_np(dist):
    max_exact = REL_BUCKETS // 2
    d_f = np.maximum(dist, 1).astype(np.float32)
    large = max_exact + (np.log(d_f / np.float32(max_exact))
                         / np.float32(math.log(REL_MAX_DIST / max_exact))
                         * np.float32(REL_BUCKETS - max_exact)).astype(np.int32)
    large = np.minimum(large, REL_BUCKETS - 1)
    return np.where(dist < max_exact, dist, large)


def _dil_bucket_index():
    steps = DIL_STEPS
    qi = np.arange(steps)[:, None]
    kj = np.arange(2 * steps)[None, :]
    rel = qi + steps - kj
    band = (rel >= 0) & (rel <= steps)
    out = []
    for _, dil in DIL_CONFIGS:
        bucket = _t5_bucket_np(np.clip(rel, 0, steps) * dil)
        out.append(np.where(band, bucket, REL_BUCKETS))
    return np.stack(out).astype(np.int32)


def _dil_bias_kernel(relb_ref, idx_ref, o_ref):
    head = pl.program_id(1)
    idx = idx_ref[0]
    bias = jnp.full(idx.shape, NEG_INF, F32)
    for bkt in range(REL_BUCKETS):
        bias = jnp.where(idx == bkt, relb_ref[bkt, head] * LOG2E, bias)
    o_ref[0, 0] = bias


def _dil_bias(rel_bias):
    steps = DIL_STEPS
    ncfg = len(DIL_CONFIGS)
    idx = jnp.asarray(_dil_bucket_index())
    return pl.pallas_call(
        _dil_bias_kernel,
        grid=(ncfg, DIL_HEADS),
        in_specs=[pl.BlockSpec(memory_space=pltpu.SMEM),
                  pl.BlockSpec((1, steps, 2 * steps), lambda c, h: (c, 0, 0))],
        out_specs=pl.BlockSpec((1, 1, steps, 2 * steps), lambda c, h: (c, h, 0, 0)),
        out_shape=jax.ShapeDtypeStruct((ncfg, DIL_HEADS, steps, 2 * steps), F32),
        compiler_params=_cparams(("parallel", "parallel")),
    )(rel_bias, idx)


DIL_GROUP = 8


def _dil_kernel(bias_ref, q_ref, k_ref, v_ref, o_ref, m_sc, l_sc, acc_sc, *, seq):
    steps = DIL_STEPS
    e = DIL_HEAD_DIM
    scale = e ** -0.5 * LOG2E
    nt = (((1,), (1,)), ((), ()))
    n_cfg = len(DIL_CONFIGS)

    def rows_of(start, size, dil):
        if dil == 1:
            return pl.ds(start, size)
        return pl.ds(start, size, stride=dil)

    def attend(c, dil, blocks, merge):
        q_rows = [rows_of(q_start, steps, dil) for q_start, _, _, _ in blocks]
        k_rows = [rows_of(k_start, n_keys, dil) for _, k_start, n_keys, _ in blocks]
        logits = []
        for qr, kr, (_, _, _, bias) in zip(q_rows, k_rows, blocks):
            q = q_ref[0, qr, :].astype(BF16)
            k = k_ref[0, kr, :].astype(BF16)
            logits.append(lax.dot_general(q, k, nt, preferred_element_type=F32) * scale + bias)
        stats = []
        for s in logits:
            m = jnp.max(s, axis=-1, keepdims=True)
            p = jnp.exp2(s - m)
            stats.append((m, jnp.sum(p, axis=-1, keepdims=True), p.astype(BF16)))
        pvs = [jnp.dot(p, v_ref[0, kr, :].astype(BF16), preferred_element_type=F32)
               for (_, _, p), kr in zip(stats, k_rows)]
        if not merge:
            for qr, (m, l, _), pv in zip(q_rows, stats, pvs):
                m_sc[c - 1, qr, :] = jnp.broadcast_to(m, (steps, e))
                l_sc[c - 1, qr, :] = jnp.broadcast_to(l, (steps, e))
                acc_sc[c - 1, qr, :] = pv
            return
        outs = []
        for qr, (m, l, _), pv in zip(q_rows, stats, pvs):
            ms = [m] + [m_sc[i, qr, :] for i in range(n_cfg - 1)]
            ls = [l] + [l_sc[i, qr, :] for i in range(n_cfg - 1)]
            accs = [pv] + [acc_sc[i, qr, :] for i in range(n_cfg - 1)]
            m_max = functools.reduce(jnp.maximum, ms)
            num = None
            den = None
            for m_i, l_i, acc_i in zip(ms, ls, accs):
                wgt = jnp.exp2(m_i - m_max)
                num = wgt * acc_i if num is None else num + wgt * acc_i
                den = wgt * l_i if den is None else den + wgt * l_i
            outs.append((qr, (num / den).astype(o_ref.dtype)))
        for qr, o in outs:
            o_ref[0, qr, :] = o

    for c in reversed(range(n_cfg)):
        dil = DIL_CONFIGS[c][1]
        nb = seq // dil // steps
        span = steps * dil

        def first(r, c=c):
            return (r, r, steps, bias_ref[c, 0, :, steps:2 * steps])

        def later(j, c=c, dil=dil, nb=nb, span=span):
            if dil == 1:
                q_start = (1 + j) * span
                if not isinstance(q_start, int):
                    q_start = pl.multiple_of(q_start, span)
                return (q_start, q_start - span, 2 * steps, bias_ref[c, 0])
            r = j // (nb - 1)
            n = 1 + j % (nb - 1)
            return (n * span + r, (n - 1) * span + r, 2 * steps, bias_ref[c, 0])

        for make, count in ((first, dil), (later, dil * (nb - 1))):
            full, rest = divmod(count, DIL_GROUP)

            def group(gi, carry, c=c, dil=dil, make=make):
                attend(c, dil, [make(gi * DIL_GROUP + u) for u in range(DIL_GROUP)], c == 0)
                return carry

            if full:
                lax.fori_loop(0, full, group, 0)
            if rest:
                attend(c, dil, [make(full * DIL_GROUP + u) for u in range(rest)], c == 0)


def _dilated(qkv3, bias, batch, seq):
    e = DIL_HEAD_DIM
    steps = DIL_STEPS
    ncfg = len(DIL_CONFIGS)
    assert DIL_CONFIGS[0][1] == 1, "the config that writes the output rows must be undilated"
    return pl.pallas_call(
        functools.partial(_dil_kernel, seq=seq),
        grid=(batch, DIL_HEADS),
        in_specs=[
            pl.BlockSpec((ncfg, 1, steps, 2 * steps), lambda b, h: (0, h, 0, 0)),
            pl.BlockSpec((1, seq, e), lambda b, h: (b, 0, h)),
            pl.BlockSpec((1, seq, e), lambda b, h: (b, 0, DIL_HEADS + h)),
            pl.BlockSpec((1, seq, e), lambda b, h: (b, 0, 2 * DIL_HEADS + h)),
        ],
        out_specs=pl.BlockSpec((1, seq, e), lambda b, h: (b, 0, h)),
        out_shape=jax.ShapeDtypeStruct((batch, seq, DIL_HEADS * e), BF16),
        scratch_shapes=[pltpu.VMEM((ncfg - 1, seq, e), F32),
                        pltpu.VMEM((ncfg - 1, seq, e), F32),
                        pltpu.VMEM((ncfg - 1, seq, e), F32)],
        compiler_params=_cparams(("parallel", "parallel")),
    )(bias, qkv3, qkv3, qkv3)


def _mix_out_kernel(a_ref, b_ref, wa_ref, wb_ref, x_ref, o_ref):
    acc = jnp.dot(a_ref[...], wa_ref[...], preferred_element_type=F32)
    acc = acc + jnp.dot(b_ref[...], wb_ref[...], preferred_element_type=F32)
    o_ref[...] = x_ref[...] + acc


def _mix_out(o_gla, o_dil, w_a, w_b, x, bm=1024, bn=1024):
    m, ka = o_gla.shape
    kb = o_dil.shape[1]
    n = w_a.shape[1]
    return pl.pallas_call(
        _mix_out_kernel,
        grid=(m // bm, n // bn),
        in_specs=[pl.BlockSpec((bm, ka), lambda i, j: (i, 0)),
                  pl.BlockSpec((bm, kb), lambda i, j: (i, 0)),
                  pl.BlockSpec((ka, bn), lambda i, j: (0, j)),
                  pl.BlockSpec((kb, bn), lambda i, j: (0, j)),
                  pl.BlockSpec((bm, bn), lambda i, j: (i, j))],
        out_specs=pl.BlockSpec((bm, bn), lambda i, j: (i, j)),
        out_shape=jax.ShapeDtypeStruct((m, n), F32),
        compiler_params=_cparams(("parallel", "parallel")),
    )(o_gla, o_dil, w_a, w_b, x)


XATTN_SLAB = 128


def _xattn_kernel(h_ref, gx_ref, wq_ref, k_ref, v_ref, wo_ref, gf_ref, h2_ref, hn_ref):
    e = XATTN_HEAD_DIM
    nt = (((1,), (1,)), ((), ()))
    gx = gx_ref[...]
    gf = gf_ref[...]
    slabs = [slice(r0, r0 + XATTN_SLAB) for r0 in range(0, h_ref.shape[0], XATTN_SLAB)]
    hs = [h_ref[rows, :] for rows in slabs]
    hns = []
    for h in hs:
        ms = jnp.mean(h * h, axis=-1, keepdims=True)
        hns.append(((h * lax.rsqrt(ms + RMS_EPS)) * gx).astype(BF16))
    qs = [jnp.dot(hn, wq_ref[...], preferred_element_type=F32).astype(BF16) for hn in hns]
    os_ = []
    for q in qs:
        outs = []
        for hh in range(XATTN_HEADS):
            qh = q[:, hh * e:(hh + 1) * e]
            kh = k_ref[0, :, hh * e:(hh + 1) * e]
            vh = v_ref[0, :, hh * e:(hh + 1) * e]
            s = lax.dot_general(qh, kh, nt, preferred_element_type=F32) * (e ** -0.5)
            m = jnp.max(s, axis=-1, keepdims=True)
            p = jnp.exp(s - m)
            p = p / jnp.sum(p, axis=-1, keepdims=True)
            outs.append(jnp.dot(p.astype(BF16), vh, preferred_element_type=F32))
        os_.append(jnp.concatenate(outs, axis=1).astype(BF16))
    h2s = [h + jnp.dot(o, wo_ref[...], preferred_element_type=F32) for h, o in zip(hs, os_)]
    hn2s = []
    for h2 in h2s:
        ms2 = jnp.mean(h2 * h2, axis=-1, keepdims=True)
        hn2s.append(((h2 * lax.rsqrt(ms2 + RMS_EPS)) * gf).astype(hn_ref.dtype))
    for rows, h2, hn2 in zip(slabs, h2s, hn2s):
        h2_ref[rows, :] = h2
        hn_ref[rows, :] = hn2


def _xattn(h1, gx, wq, kx, vx, wo, gf, seq, tm=256):
    m, d = h1.shape
    mem_len = kx.shape[1]
    per_seq = seq // tm
    return pl.pallas_call(
        _xattn_kernel,
        grid=(m // tm,),
        in_specs=[pl.BlockSpec((tm, d), lambda i: (i, 0)),
                  pl.BlockSpec((1, d), lambda i: (0, 0)),
                  pl.BlockSpec((d, XATTN_WIDTH), lambda i: (0, 0)),
                  pl.BlockSpec((1, mem_len, XATTN_WIDTH), lambda i: (i // per_seq, 0, 0)),
                  pl.BlockSpec((1, mem_len, XATTN_WIDTH), lambda i: (i // per_seq, 0, 0)),
                  pl.BlockSpec((XATTN_WIDTH, d), lambda i: (0, 0)),
                  pl.BlockSpec((1, d), lambda i: (0, 0))],
        out_specs=[pl.BlockSpec((tm, d), lambda i: (i, 0)),
                   pl.BlockSpec((tm, d), lambda i: (i, 0))],
        out_shape=[jax.ShapeDtypeStruct((m, d), F32),
                   jax.ShapeDtypeStruct((m, d), BF16)],
        compiler_params=_cparams(("parallel",)),
    )(h1, gx.reshape(1, d), wq, kx, vx, wo, gf.reshape(1, d))


def _ffn_in_kernel(x_ref, wg_ref, wu_ref, halo_ref, cw_ref, cb_ref, a_ref):
    x = x_ref[...]
    g = jnp.dot(x, wg_ref[...], preferred_element_type=F32)
    u = jnp.dot(x, wu_ref[...], preferred_element_type=F32)
    halo = halo_ref[0]
    prev1 = halo[SUBLANES - 1:SUBLANES, :]
    prev2 = halo[SUBLANES - 2:SUBLANES - 1, :]
    row = lax.broadcasted_iota(jnp.int32, (SUBLANES, g.shape[1]), 0)
    r1 = pltpu.roll(g, 1, axis=0)
    r2 = pltpu.roll(g, 2, axis=0)
    head1 = jnp.where(row == 0, prev1, r1[:SUBLANES])
    head2 = jnp.where(row == 0, prev2, jnp.where(row == 1, prev1, r2[:SUBLANES]))
    g_m1 = jnp.concatenate([head1, r1[SUBLANES:]], axis=0)
    g_m2 = jnp.concatenate([head2, r2[SUBLANES:]], axis=0)
    cw = cw_ref[...]
    y = cb_ref[...] + g_m2 * cw[0:1, :]
    y = y + g_m1 * cw[1:2, :]
    y = y + g * cw[2:3, :]
    a_ref[...] = ((y / (1.0 + jnp.exp(-y))) * u).astype(a_ref.dtype)


def _ffn_in(x, wg, wu, halo_g, cw, cb, bm=FFN_BM, bn=512):
    m, k = x.shape
    n = wg.shape[1]
    return pl.pallas_call(
        _ffn_in_kernel,
        grid=(m // bm, n // bn),
        in_specs=[pl.BlockSpec((bm, k), lambda i, j: (i, 0)),
                  pl.BlockSpec((k, bn), lambda i, j: (0, j)),
                  pl.BlockSpec((k, bn), lambda i, j: (0, j)),
                  pl.BlockSpec((1, SUBLANES, bn), lambda i, j: (i, 0, j)),
                  pl.BlockSpec((CONV_WIDTH, bn), lambda i, j: (0, j)),
                  pl.BlockSpec((1, bn), lambda i, j: (0, j))],
        out_specs=pl.BlockSpec((bm, bn), lambda i, j: (i, j)),
        out_shape=jax.ShapeDtypeStruct((m, n), BF16),
        compiler_params=_cparams(("parallel", "parallel")),
    )(x, wg, wu, halo_g, cw, cb)


def _ffn_halo_rows(hn, batch, seq, bm=FFN_BM):
    d = hn.shape[1]
    tiles = seq // bm
    tail = hn.reshape(batch, tiles, bm, d)[:, :, bm - SUBLANES:, :]
    prev = jnp.concatenate([jnp.zeros_like(tail[:, :1]), tail[:, :-1]], axis=1)
    return prev.reshape(batch * tiles * SUBLANES, d)


FFN_OUT_COLS = 1024
FFN_RES_COLS = 256
FFN_NORM_ROWS = 256


def _ffn_out_kernel(a_ref, wd_ref, h_ref, fg_ref, o_ref, *, n_res):
    kk = pl.program_id(1)
    d = o_ref.shape[1]

    @pl.when(kk == 0)
    def _():
        o_ref[...] = jnp.zeros_like(o_ref)

    a = a_ref[...]
    for c0 in range(0, d, FFN_OUT_COLS):
        cols = slice(c0, c0 + FFN_OUT_COLS)
        o_ref[:, cols] += jnp.dot(a, wd_ref[:, cols], preferred_element_type=F32)

    for c in range(n_res):
        @pl.when(kk == c)
        def _(c=c):
            cols = slice(c * FFN_RES_COLS, (c + 1) * FFN_RES_COLS)
            o_ref[:, cols] += h_ref[...]

    @pl.when(kk == pl.num_programs(1) - 1)
    def _():
        fg = fg_ref[...]
        for r0 in range(0, o_ref.shape[0], FFN_NORM_ROWS):
            rows = slice(r0, r0 + FFN_NORM_ROWS)
            h3 = o_ref[rows, :]
            ms = jnp.mean(h3 * h3, axis=-1, keepdims=True)
            o_ref[rows, :] = (h3 * lax.rsqrt(ms + RMS_EPS)) * fg


def _ffn_out(a, wd, h2, fg, tm=1024, tk=512):
    m, ff = a.shape
    d = wd.shape[1]
    n_res = d // FFN_RES_COLS
    assert ff // tk >= n_res, "one residual slab per contraction step"
    return pl.pallas_call(
        functools.partial(_ffn_out_kernel, n_res=n_res),
        grid=(m // tm, ff // tk),
        in_specs=[pl.BlockSpec((tm, tk), lambda i, k: (i, k)),
                  pl.BlockSpec((tk, d), lambda i, k: (k, 0)),
                  pl.BlockSpec((tm, FFN_RES_COLS), lambda i, k: (i, jnp.minimum(k, n_res - 1))),
                  pl.BlockSpec((1, d), lambda i, k: (0, 0))],
        out_specs=pl.BlockSpec((tm, d), lambda i, k: (i, 0)),
        out_shape=jax.ShapeDtypeStruct((m, d), F32),
        compiler_params=_cparams(("parallel", "arbitrary")),
    )(a, wd, h2, fg.reshape(1, d))


def _layer(x2, mem2, rel_bias, batch, seq, mem_len, norm_mix_g, w_in, gla_w_gate2,
           gla_b_gate, gla_norm_g, w_out, norm_xattn_g, mem_norm_g, w_xq, w_xk, w_xv,
           w_xo, norm_ffn_g, w_ffn_gate, w_ffn_up, ffn_conv_w, ffn_conv_b, w_ffn_down,
           out_g):
    w_gqkv, w_glr, w_gr, w_dqkv = _split_w_in(w_in)
    w2_pad = jnp.pad(gla_w_gate2, ((0, LANES - GLA_LOWRANK), (0, 0)))
    ff_pad = D_FF_PAD - D_FF
    wg = _cast_pad_cols(w_ffn_gate, D_FF_PAD)
    wu = _cast_pad_cols(w_ffn_up, D_FF_PAD)
    wd = _cast_pad_rows(w_ffn_down, D_FF_PAD)
    cw = jnp.pad(ffn_conv_w, ((0, 0), (0, ff_pad)))
    cb = jnp.pad(ffn_conv_b, ((0, ff_pad),)).reshape(1, D_FF_PAD)
    w_out_b = w_out.astype(BF16)
    w_kv = jnp.concatenate([w_xk, w_xv], axis=1).astype(BF16)

    hn = _rmsnorm(x2, norm_mix_g)
    g_qkv = _matmul(hn, w_gqkv, 1024, 1024, BF16)
    g_r = _matmul(hn, w_gr, 1024, 1024, BF16)
    d_qkv = _matmul(hn, w_dqkv, 1024, 1024, F32)
    glr = _matmul(hn, w_glr, 1024, LANES, F32)
    b_cum = _gla_gate(glr, w2_pad, gla_b_gate.reshape(1, -1))
    o_gla = _gla(g_qkv, g_r, b_cum, gla_norm_g.reshape(1, -1), batch, seq)
    o_dil = _dilated(d_qkv.reshape(batch, seq, 3 * DIL_HEADS * DIL_HEAD_DIM),
                     _dil_bias(rel_bias), batch, seq)
    o_dil = o_dil.reshape(batch * seq, DIL_HEADS * DIL_HEAD_DIM)
    half = GLA_HEADS * GLA_DV
    h1 = _mix_out(o_gla, o_dil, w_out_b[:half], w_out_b[half:], x2)

    memn = _rmsnorm(mem2, mem_norm_g)
    kv = _matmul(memn, w_kv, 512, 512, BF16)
    kx = kv[:, :XATTN_WIDTH].reshape(batch, mem_len, XATTN_WIDTH)
    vx = kv[:, XATTN_WIDTH:].reshape(batch, mem_len, XATTN_WIDTH)
    h2, hn3 = _xattn(h1, norm_xattn_g, w_xq.astype(BF16), kx, vx, w_xo.astype(BF16),
                     norm_ffn_g, seq)

    halo_x = _ffn_halo_rows(hn3, batch, seq)
    halo_g = _matmul(halo_x, wg, halo_x.shape[0], 512, F32)
    halo_g = halo_g.reshape(-1, SUBLANES, D_FF_PAD)
    act = _ffn_in(hn3, wg, wu, halo_g, cw, cb)
    return _ffn_out(act, wd, h2, out_g)


def kernel(x, mem, rel_bias, norm_mix_g, w_in, gla_w_gate2, gla_b_gate, gla_norm_g, w_out,
           norm_xattn_g, mem_norm_g, w_xq, w_xk, w_xv, w_xo, norm_ffn_g, w_ffn_gate,
           w_ffn_up, ffn_conv_w, ffn_conv_b, w_ffn_down, final_norm_g):
    batch, seq, d = x.shape
    mem_len = mem.shape[1]
    depth = w_in.shape[0]
    assert depth == 1, "the fused final rmsnorm assumes a single layer"
    out = _layer(x.reshape(batch * seq, d), mem.reshape(batch * mem_len, d), rel_bias,
                 batch, seq, mem_len, norm_mix_g[0], w_in[0], gla_w_gate2[0],
                 gla_b_gate[0], gla_norm_g[0], w_out[0], norm_xattn_g[0], mem_norm_g[0],
                 w_xq[0], w_xk[0], w_xv[0], w_xo[0], norm_ffn_g[0], w_ffn_gate[0],
                 w_ffn_up[0], ffn_conv_w[0], ffn_conv_b[0], w_ffn_down[0], final_norm_g)
    return out.reshape(batch, seq, d)
```

```python
import functools
import math

import numpy as np
import jax
import jax.numpy as jnp
from jax import lax
from jax.experimental import pallas as pl
from jax.experimental.pallas import tpu as pltpu

F32 = jnp.float32
BF16 = jnp.bfloat16

D_MODEL = 4096
RMS_EPS = 1e-6
GLA_HEADS = 4
GLA_DV = 512
GLA_DK = 256
GLA_LOWRANK = 16
GLA_TAU = 16.0
GLA_CHUNK = 64
DIL_HEAD_DIM = 128
DIL_HEADS = 16
DIL_CONFIGS = ((128, 1), (512, 4), (2048, 16))
DIL_STEPS = 128
REL_BUCKETS = 32
REL_MAX_DIST = 2048
XATTN_HEADS = 4
XATTN_HEAD_DIM = 128
XATTN_WIDTH = 512
D_FF = 11008
CONV_WIDTH = 3
NEG_INF = -1e30
LOG2E = math.log2(math.e)

LANES = 128
SUBLANES = 8
VMEM_LIMIT = 56 * 1024 * 1024

D_FF_PAD = 11264
FFN_BM = 1024

_W_GQKV = 0
_W_GLR = 4096
_W_GR = 4112
_W_DQKV = 6160
_W_END = 12304


def _cparams(sem):
    return pltpu.CompilerParams(dimension_semantics=sem, vmem_limit_bytes=VMEM_LIMIT)


def _rmsnorm_kernel(x_ref, g_ref, o_ref):
    x = x_ref[...]
    ms = jnp.mean(x * x, axis=-1, keepdims=True)
    o_ref[...] = ((x * lax.rsqrt(ms + RMS_EPS)) * g_ref[...]).astype(o_ref.dtype)


def _rmsnorm(x, g, tm=256):
    m, d = x.shape
    return pl.pallas_call(
        _rmsnorm_kernel,
        grid=(m // tm,),
        in_specs=[pl.BlockSpec((tm, d), lambda i: (i, 0)),
                  pl.BlockSpec((1, d), lambda i: (0, 0))],
        out_specs=pl.BlockSpec((tm, d), lambda i: (i, 0)),
        out_shape=jax.ShapeDtypeStruct((m, d), BF16),
        compiler_params=_cparams(("parallel",)),
    )(x, g.reshape(1, d))


def _cast_pad_cols_kernel(w_ref, o_ref):
    n = w_ref.shape[1]
    o_ref[:, :n] = w_ref[...].astype(o_ref.dtype)
    o_ref[:, n:] = jnp.zeros((o_ref.shape[0], o_ref.shape[1] - n), o_ref.dtype)


def _cast_pad_cols(w, n_pad, tm=256):
    k, n = w.shape
    return pl.pallas_call(
        _cast_pad_cols_kernel,
        grid=(k // tm,),
        in_specs=[pl.BlockSpec((tm, n), lambda i: (i, 0))],
        out_specs=pl.BlockSpec((tm, n_pad), lambda i: (i, 0)),
        out_shape=jax.ShapeDtypeStruct((k, n_pad), BF16),
        compiler_params=_cparams(("parallel",)),
    )(w)


def _cast_pad_rows_kernel(w_ref, o_ref, *, n_valid):
    valid = pl.program_id(0) < n_valid
    o_ref[...] = jnp.where(valid, w_ref[...], 0.0).astype(o_ref.dtype)


def _cast_pad_rows(w, k_pad, tm=256):
    k, n = w.shape
    n_valid = k // tm
    return pl.pallas_call(
        functools.partial(_cast_pad_rows_kernel, n_valid=n_valid),
        grid=(k_pad // tm,),
        in_specs=[pl.BlockSpec((tm, n), lambda i: (jnp.minimum(i, n_valid - 1), 0))],
        out_specs=pl.BlockSpec((tm, n), lambda i: (i, 0)),
        out_shape=jax.ShapeDtypeStruct((k_pad, n), BF16),
        compiler_params=_cparams(("parallel",)),
    )(w)


def _mm_nt_kernel(x_ref, wt_ref, o_ref):
    o_ref[...] = lax.dot_general(x_ref[...], wt_ref[...], (((1,), (1,)), ((), ())),
                                 preferred_element_type=F32).astype(o_ref.dtype)


def _matmul_nt(x, wt, bm, bn, out_dtype):
    m, k = x.shape
    n = wt.shape[0]
    return pl.pallas_call(
        _mm_nt_kernel,
        grid=(m // bm, n // bn),
        in_specs=[pl.BlockSpec((bm, k), lambda i, j: (i, 0)),
                  pl.BlockSpec((bn, k), lambda i, j: (j, 0))],
        out_specs=pl.BlockSpec((bm, bn), lambda i, j: (i, j)),
        out_shape=jax.ShapeDtypeStruct((m, n), out_dtype),
        compiler_params=_cparams(("parallel", "parallel")),
    )(x, wt)


def _mm_kernel(x_ref, w_ref, o_ref):
    o_ref[...] = jnp.dot(x_ref[...], w_ref[...],
                         preferred_element_type=F32).astype(o_ref.dtype)


def _matmul(x, w, bm, bn, out_dtype):
    m, k = x.shape
    n = w.shape[1]
    return pl.pallas_call(
        _mm_kernel,
        grid=(m // bm, n // bn),
        in_specs=[pl.BlockSpec((bm, k), lambda i, j: (i, 0)),
                  pl.BlockSpec((k, bn), lambda i, j: (0, j))],
        out_specs=pl.BlockSpec((bm, bn), lambda i, j: (i, j)),
        out_shape=jax.ShapeDtypeStruct((m, n), out_dtype),
        compiler_params=_cparams(("parallel", "parallel")),
    )(x, w)


GLA_T = 512
GLA_GROUP = 4


def _split_bf16(x):
    hi = x.astype(BF16)
    lo = (x - hi.astype(F32)).astype(BF16)
    return hi, lo


def _gla_gate_kernel(glr_ref, w2_ref, b2_ref, tri_ref, b_ref):
    x_hi, x_lo = _split_bf16(glr_ref[...])
    w_hi, w_lo = _split_bf16(w2_ref[...])
    z = (jnp.dot(x_hi, w_hi, preferred_element_type=F32)
         + jnp.dot(x_lo, w_hi, preferred_element_type=F32)
         + jnp.dot(x_hi, w_lo, preferred_element_type=F32)) + b2_ref[...]
    log_sig = jnp.minimum(z, 0.0) - jnp.log(1.0 + jnp.exp(-jnp.abs(z)))
    g_hi, g_lo = _split_bf16(log_sig / GLA_TAU)
    tri = tri_ref[...]
    b_ref[...] = (jnp.dot(tri, g_hi, preferred_element_type=F32)
                  + jnp.dot(tri, g_lo, preferred_element_type=F32))


def _chunk_tril(t):
    idx = np.arange(t)
    same = (idx[:, None] // GLA_CHUNK) == (idx[None, :] // GLA_CHUNK)
    return jnp.asarray((same & (idx[:, None] >= idx[None, :])).astype(np.float32), BF16)


def _gla_gate(glr, w2_pad, b2, t=GLA_T):
    m = glr.shape[0]
    n = w2_pad.shape[1]
    return pl.pallas_call(
        _gla_gate_kernel,
        grid=(m // t,),
        in_specs=[pl.BlockSpec((t, LANES), lambda i: (i, 0)),
                  pl.BlockSpec((LANES, n), lambda i: (0, 0)),
                  pl.BlockSpec((1, n), lambda i: (0, 0)),
                  pl.BlockSpec((t, t), lambda i: (0, 0))],
        out_specs=pl.BlockSpec((t, n), lambda i: (i, 0)),
        out_shape=jax.ShapeDtypeStruct((m, n), F32),
        compiler_params=_cparams(("parallel",)),
    )(glr, w2_pad, b2, _chunk_tril(t))


def _gla_kernel(q_ref, k_ref, v_ref, r_ref, b_ref, gn_ref, o_ref, state_ref):
    c_sz = GLA_CHUNK
    g_sz = GLA_GROUP * c_sz
    assert GLA_GROUP == 4

    @pl.when(pl.program_id(2) == 0)
    def _():
        state_ref[...] = jnp.zeros_like(state_ref)

    row = lax.broadcasted_iota(jnp.int32, (g_sz, g_sz), 0)
    col = lax.broadcasted_iota(jnp.int32, (g_sz, g_sz), 1)
    causal = row >= col
    gn = gn_ref[...]
    nt = (((1,), (1,)), ((), ()))
    tn = (((0,), (0,)), ((), ()))

    def cat(parts):
        return jnp.concatenate(parts, axis=0)

    def group(gi, carry):
        rows = pl.ds(pl.multiple_of(gi * g_sz, g_sz), g_sz)
        b_all = b_ref[rows, :]
        q_all = q_ref[rows, :].astype(F32) * (GLA_DK ** -0.5)
        k_all = k_ref[rows, :].astype(F32)
        v = v_ref[rows, :]
        st = state_ref[...]

        sl = [slice(c * c_sz, (c + 1) * c_sz) for c in range(GLA_GROUP)]
        b = [b_all[s] for s in sl]
        bl = [x[c_sz - 1:c_sz, :] for x in b]
        bm = [x[c_sz // 2:c_sz // 2 + 1, :] for x in b]
        q_start = [q_all[s] * jnp.exp(x) for s, x in zip(sl, b)]
        k_end = [k_all[s] * jnp.exp(t - x) for s, x, t in zip(sl, b, bl)]
        q_mid = [(q_all[s] * jnp.exp(x - m)).astype(BF16) for s, x, m in zip(sl, b, bm)]
        k_mid = [(k_all[s] * jnp.exp(m - x)).astype(BF16) for s, x, m in zip(sl, b, bm)]

        e1, e2, e3 = jnp.exp(bl[1]), jnp.exp(bl[2]), jnp.exp(bl[3])
        e0 = jnp.exp(bl[0])
        d01, d12, d23 = e0 * e1, e1 * e2, e2 * e3
        d012, d123 = d01 * e2, d12 * e3
        d_all = d012 * e3

        qs = cat([q_start[0], q_start[1] * e0, q_start[2] * d01, q_start[3] * d012]).astype(BF16)
        o = lax.dot_general(qs, st.astype(BF16), nt, preferred_element_type=F32)

        k_end_b = [x.astype(BF16) for x in k_end]
        a_r0 = lax.dot_general(q_mid[0], cat([k_mid[0], k_mid[1]]), nt,
                               preferred_element_type=F32)
        a_r1 = lax.dot_general(q_mid[1], cat([(k_end[0] * jnp.exp(bm[1])).astype(BF16),
                                              k_mid[1]]), nt, preferred_element_type=F32)
        a_r2 = lax.dot_general(q_mid[2], cat([k_mid[2], k_mid[3]]), nt,
                               preferred_element_type=F32)
        a_r3 = lax.dot_general(q_mid[3], cat([(k_end[2] * jnp.exp(bm[3])).astype(BF16),
                                              k_mid[3]]), nt, preferred_element_type=F32)
        a_off = lax.dot_general(cat([q_start[2], q_start[3] * e2]).astype(BF16),
                                cat([(k_end[0] * e1).astype(BF16), k_end_b[1]]), nt,
                                preferred_element_type=F32)
        zeros = jnp.zeros((2 * c_sz, 2 * c_sz), F32)
        att = jnp.concatenate([cat([a_r0, a_r1, a_off]), cat([zeros, a_r2, a_r3])], axis=1)
        att = jnp.where(causal, att, 0.0).astype(BF16)
        o = o + jnp.dot(att, v, preferred_element_type=F32)

        k_fin = cat([k_end[0] * d123, k_end[1] * d23, k_end[2] * e3, k_end[3]]).astype(BF16)
        kv_t = lax.dot_general(v, k_fin, tn, preferred_element_type=F32)
        state_new = st * d_all + kv_t

        ms = jnp.mean(o * o, axis=-1, keepdims=True)
        on = (o * lax.rsqrt(ms + RMS_EPS)) * gn
        r = r_ref[rows, :].astype(F32)
        gate = r / (1.0 + jnp.exp(-r))
        o_ref[rows, :] = (on * gate).astype(o_ref.dtype)
        state_ref[...] = state_new
        return carry

    lax.fori_loop(0, GLA_T // g_sz, group, 0, unroll=True)


def _gla(qkv, gr, b_cum, gn, batch, seq):
    t = GLA_T
    nt_ = seq // t
    m = batch * seq
    kb = (GLA_HEADS * GLA_DK) // GLA_DK
    vb = (2 * GLA_HEADS * GLA_DK) // GLA_DV
    return pl.pallas_call(
        _gla_kernel,
        grid=(batch, GLA_HEADS, nt_),
        in_specs=[
            pl.BlockSpec((t, GLA_DK), lambda b, h, s: (b * nt_ + s, h)),
            pl.BlockSpec((t, GLA_DK), lambda b, h, s: (b * nt_ + s, kb + h)),
            pl.BlockSpec((t, GLA_DV), lambda b, h, s: (b * nt_ + s, vb + h)),
            pl.BlockSpec((t, GLA_DV), lambda b, h, s: (b * nt_ + s, h)),
            pl.BlockSpec((t, GLA_DK), lambda b, h, s: (b * nt_ + s, h)),
            pl.BlockSpec((1, GLA_DV), lambda b, h, s: (0, 0)),
        ],
        out_specs=pl.BlockSpec((t, GLA_DV), lambda b, h, s: (b * nt_ + s, h)),
        out_shape=jax.ShapeDtypeStruct((m, GLA_HEADS * GLA_DV), BF16),
        scratch_shapes=[pltpu.VMEM((GLA_DV, GLA_DK), F32)],
        compiler_params=_cparams(("parallel", "parallel", "arbitrary")),
    )(qkv, qkv, qkv, gr, b_cum, gn)


def _t5_bucket_np(dist):
    max_exact = REL_BUCKETS // 2
    d_f = np.maximum(dist, 1).astype(np.float32)
    large = max_exact + (np.log(d_f / np.float32(max_exact))
                         / np.float32(math.log(REL_MAX_DIST / max_exact))
                         * np.float32(REL_BUCKETS - max_exact)).astype(np.int32)
    large = np.minimum(large, REL_BUCKETS - 1)
    return np.where(dist < max_exact, dist, large)


def _dil_bucket_index():
    steps = DIL_STEPS
    qi = np.arange(steps)[:, None]
    kj = np.arange(2 * steps)[None, :]
    rel = qi + steps - kj
    band = (rel >= 0) & (rel <= steps)
    out = []
    for _, dil in DIL_CONFIGS:
        bucket = _t5_bucket_np(np.clip(rel, 0, steps) * dil)
        out.append(np.where(band, bucket, REL_BUCKETS))
    return np.stack(out).astype(np.int32)


def _dil_bias_kernel(relb_ref, idx_ref, o_ref):
    head = pl.program_id(1)
    idx = idx_ref[0]
    bias = jnp.full(idx.shape, NEG_INF, F32)
    for bkt in range(REL_BUCKETS):
        bias = jnp.where(idx == bkt, relb_ref[bkt, head] * LOG2E, bias)
    o_ref[0, 0] = bias


def _dil_bias(rel_bias):
    steps = DIL_STEPS
    ncfg = len(DIL_CONFIGS)
    idx = jnp.asarray(_dil_bucket_index())
    return pl.pallas_call(
        _dil_bias_kernel,
        grid=(ncfg, DIL_HEADS),
        in_specs=[pl.BlockSpec(memory_space=pltpu.SMEM),
                  pl.BlockSpec((1, steps, 2 * steps), lambda c, h: (c, 0, 0))],
        out_specs=pl.BlockSpec((1, 1, steps, 2 * steps), lambda c, h: (c, h, 0, 0)),
        out_shape=jax.ShapeDtypeStruct((ncfg, DIL_HEADS, steps, 2 * steps), F32),
        compiler_params=_cparams(("parallel", "parallel")),
    )(rel_bias, idx)


DIL_GROUP = 8


def _dil_kernel(bias_ref, q_ref, k_ref, v_ref, o_ref, m_sc, l_sc, acc_sc, *, seq):
    steps = DIL_STEPS
    e = DIL_HEAD_DIM
    scale = e ** -0.5 * LOG2E
    nt = (((1,), (1,)), ((), ()))
    n_cfg = len(DIL_CONFIGS)

    def rows_of(start, size, dil):
        if dil == 1:
            return pl.ds(start, size)
        return pl.ds(start, size, stride=dil)

    def attend(c, dil, blocks, merge):
        q_rows = [rows_of(q_start, steps, dil) for q_start, _, _, _ in blocks]
        k_rows = [rows_of(k_start, n_keys, dil) for _, k_start, n_keys, _ in blocks]
        logits = []
        for qr, kr, (_, _, _, bias) in zip(q_rows, k_rows, blocks):
            q = q_ref[0, qr, :].astype(BF16)
            k = k_ref[0, kr, :].astype(BF16)
            logits.append(lax.dot_general(q, k, nt, preferred_element_type=F32) * scale + bias)
        stats = []
        for s in logits:
            m = jnp.max(s, axis=-1, keepdims=True)
            p = jnp.exp2(s - m)
            stats.append((m, jnp.sum(p, axis=-1, keepdims=True), p.astype(BF16)))
        pvs = [jnp.dot(p, v_ref[0, kr, :].astype(BF16), preferred_element_type=F32)
               for (_, _, p), kr in zip(stats, k_rows)]
        if not merge:
            for qr, (m, l, _), pv in zip(q_rows, stats, pvs):
                m_sc[c - 1, qr, :] = jnp.broadcast_to(m, (steps, e))
                l_sc[c - 1, qr, :] = jnp.broadcast_to(l, (steps, e))
                acc_sc[c - 1, qr, :] = pv
            return
        outs = []
        for qr, (m, l, _), pv in zip(q_rows, stats, pvs):
            ms = [m] + [m_sc[i, qr, :] for i in range(n_cfg - 1)]
            ls = [l] + [l_sc[i, qr, :] for i in range(n_cfg - 1)]
            accs = [pv] + [acc_sc[i, qr, :] for i in range(n_cfg - 1)]
            m_max = functools.reduce(jnp.maximum, ms)
            num = None
            den = None
            for m_i, l_i, acc_i in zip(ms, ls, accs):
                wgt = jnp.exp2(m_i - m_max)
                num = wgt * acc_i if num is None else num + wgt * acc_i
                den = wgt * l_i if den is None else den + wgt * l_i
            outs.append((qr, (num / den).astype(o_ref.dtype)))
        for qr, o in outs:
            o_ref[0, qr, :] = o

    for c in reversed(range(n_cfg)):
        dil = DIL_CONFIGS[c][1]
        nb = seq // dil // steps
        span = steps * dil

        def first(r, c=c):
            return (r, r, steps, bias_ref[c, 0, :, steps:2 * steps])

        def later(j, c=c, dil=dil, nb=nb, span=span):
            if dil == 1:
                q_start = (1 + j) * span
                if not isinstance(q_start, int):
                    q_start = pl.multiple_of(q_start, span)
                return (q_start, q_start - span, 2 * steps, bias_ref[c, 0])
            r = j // (nb - 1)
            n = 1 + j % (nb - 1)
            return (n * span + r, (n - 1) * span + r, 2 * steps, bias_ref[c, 0])

        for make, count in ((first, dil), (later, dil * (nb - 1))):
            full, rest = divmod(count, DIL_GROUP)

            def group(gi, carry, c=c, dil=dil, make=make):
                attend(c, dil, [make(gi * DIL_GROUP + u) for u in range(DIL_GROUP)], c == 0)
                return carry

            if full:
                lax.fori_loop(0, full, group, 0)
            if rest:
                attend(c, dil, [make(full * DIL_GROUP + u) for u in range(rest)], c == 0)


def _dilated(qkv3, bias, batch, seq):
    e = DIL_HEAD_DIM
    steps = DIL_STEPS
    ncfg = len(DIL_CONFIGS)
    assert DIL_CONFIGS[0][1] == 1, "the config that writes the output rows must be undilated"
    return pl.pallas_call(
        functools.partial(_dil_kernel, seq=seq),
        grid=(batch, DIL_HEADS),
        in_specs=[
            pl.BlockSpec((ncfg, 1, steps, 2 * steps), lambda b, h: (0, h, 0, 0)),
            pl.BlockSpec((1, seq, e), lambda b, h: (b, 0, h)),
            pl.BlockSpec((1, seq, e), lambda b, h: (b, 0, DIL_HEADS + h)),
            pl.BlockSpec((1, seq, e), lambda b, h: (b, 0, 2 * DIL_HEADS + h)),
        ],
        out_specs=pl.BlockSpec((1, seq, e), lambda b, h: (b, 0, h)),
        out_shape=jax.ShapeDtypeStruct((batch, seq, DIL_HEADS * e), BF16),
        scratch_shapes=[pltpu.VMEM((ncfg - 1, seq, e), F32),
                        pltpu.VMEM((ncfg - 1, seq, e), F32),
                        pltpu.VMEM((ncfg - 1, seq, e), F32)],
        compiler_params=_cparams(("parallel", "parallel")),
    )(bias, qkv3, qkv3, qkv3)


def _mix_out_kernel(a_ref, b_ref, wa_ref, wb_ref, x_ref, o_ref):
    acc = jnp.dot(a_ref[...], wa_ref[...], preferred_element_type=F32)
    acc = acc + jnp.dot(b_ref[...], wb_ref[...], preferred_element_type=F32)
    o_ref[...] = x_ref[...] + acc


def _mix_out(o_gla, o_dil, w_a, w_b, x, bm=1024, bn=1024):
    m, ka = o_gla.shape
    kb = o_dil.shape[1]
    n = w_a.shape[1]
    return pl.pallas_call(
        _mix_out_kernel,
        grid=(m // bm, n // bn),
        in_specs=[pl.BlockSpec((bm, ka), lambda i, j: (i, 0)),
                  pl.BlockSpec((bm, kb), lambda i, j: (i, 0)),
                  pl.BlockSpec((ka, bn), lambda i, j: (0, j)),
                  pl.BlockSpec((kb, bn), lambda i, j: (0, j)),
                  pl.BlockSpec((bm, bn), lambda i, j: (i, j))],
        out_specs=pl.BlockSpec((bm, bn), lambda i, j: (i, j)),
        out_shape=jax.ShapeDtypeStruct((m, n), F32),
        compiler_params=_cparams(("parallel", "parallel")),
    )(o_gla, o_dil, w_a, w_b, x)


XATTN_SLAB = 128


def _xattn_kernel(h_ref, gx_ref, wq_ref, k_ref, v_ref, wo_ref, gf_ref, h2_ref, hn_ref):
    e = XATTN_HEAD_DIM
    nt = (((1,), (1,)), ((), ()))
    gx = gx_ref[...]
    gf = gf_ref[...]
    slabs = [slice(r0, r0 + XATTN_SLAB) for r0 in range(0, h_ref.shape[0], XATTN_SLAB)]
    hs = [h_ref[rows, :] for rows in slabs]
    hns = []
    for h in hs:
        ms = jnp.mean(h * h, axis=-1, keepdims=True)
        hns.append(((h * lax.rsqrt(ms + RMS_EPS)) * gx).astype(BF16))
    qs = [jnp.dot(hn, wq_ref[...], preferred_element_type=F32).astype(BF16) for hn in hns]
    os_ = []
    for q in qs:
        outs = []
        for hh in range(XATTN_HEADS):
            qh = q[:, hh * e:(hh + 1) * e]
            kh = k_ref[0, :, hh * e:(hh + 1) * e]
            vh = v_ref[0, :, hh * e:(hh + 1) * e]
            s = lax.dot_general(qh, kh, nt, preferred_element_type=F32) * (e ** -0.5)
            m = jnp.max(s, axis=-1, keepdims=True)
            p = jnp.exp(s - m)
            p = p / jnp.sum(p, axis=-1, keepdims=True)
            outs.append(jnp.dot(p.astype(BF16), vh, preferred_element_type=F32))
        os_.append(jnp.concatenate(outs, axis=1).astype(BF16))
    h2s = [h + jnp.dot(o, wo_ref[...], preferred_element_type=F32) for h, o in zip(hs, os_)]
    hn2s = []
    for h2 in h2s:
        ms2 = jnp.mean(h2 * h2, axis=-1, keepdims=True)
        hn2s.append(((h2 * lax.rsqrt(ms2 + RMS_EPS)) * gf).astype(hn_ref.dtype))
    for rows, h2, hn2 in zip(slabs, h2s, hn2s):
        h2_ref[rows, :] = h2
        hn_ref[rows, :] = hn2


def _xattn(h1, gx, wq, kx, vx, wo, gf, seq, tm=256):
    m, d = h1.shape
    mem_len = kx.shape[1]
    per_seq = seq // tm
    return pl.pallas_call(
        _xattn_kernel,
        grid=(m // tm,),
        in_specs=[pl.BlockSpec((tm, d), lambda i: (i, 0)),
                  pl.BlockSpec((1, d), lambda i: (0, 0)),
                  pl.BlockSpec((d, XATTN_WIDTH), lambda i: (0, 0)),
                  pl.BlockSpec((1, mem_len, XATTN_WIDTH), lambda i: (i // per_seq, 0, 0)),
                  pl.BlockSpec((1, mem_len, XATTN_WIDTH), lambda i: (i // per_seq, 0, 0)),
                  pl.BlockSpec((XATTN_WIDTH, d), lambda i: (0, 0)),
                  pl.BlockSpec((1, d), lambda i: (0, 0))],
        out_specs=[pl.BlockSpec((tm, d), lambda i: (i, 0)),
                   pl.BlockSpec((tm, d), lambda i: (i, 0))],
        out_shape=[jax.ShapeDtypeStruct((m, d), F32),
                   jax.ShapeDtypeStruct((m, d), BF16)],
        compiler_params=_cparams(("parallel",)),
    )(h1, gx.reshape(1, d), wq, kx, vx, wo, gf.reshape(1, d))


def _ffn_in_kernel(x_ref, wg_ref, wu_ref, halo_ref, cw_ref, cb_ref, a_ref):
    x = x_ref[...]
    g = jnp.dot(x, wg_ref[...], preferred_element_type=F32)
    u = jnp.dot(x, wu_ref[...], preferred_element_type=F32)
    halo = halo_ref[0]
    prev1 = halo[SUBLANES - 1:SUBLANES, :]
    prev2 = halo[SUBLANES - 2:SUBLANES - 1, :]
    row = lax.broadcasted_iota(jnp.int32, (SUBLANES, g.shape[1]), 0)
    r1 = pltpu.roll(g, 1, axis=0)
    r2 = pltpu.roll(g, 2, axis=0)
    head1 = jnp.where(row == 0, prev1, r1[:SUBLANES])
    head2 = jnp.where(row == 0, prev2, jnp.where(row == 1, prev1, r2[:SUBLANES]))
    g_m1 = jnp.concatenate([head1, r1[SUBLANES:]], axis=0)
    g_m2 = jnp.concatenate([head2, r2[SUBLANES:]], axis=0)
    cw = cw_ref[...]
    y = cb_ref[...] + g_m2 * cw[0:1, :]
    y = y + g_m1 * cw[1:2, :]
    y = y + g * cw[2:3, :]
    a_ref[...] = ((y / (1.0 + jnp.exp(-y))) * u).astype(a_ref.dtype)


def _ffn_in(x, wg, wu, halo_g, cw, cb, bm=FFN_BM, bn=512):
    m, k = x.shape
    n = wg.shape[1]
    return pl.pallas_call(
        _ffn_in_kernel,
        grid=(m // bm, n // bn),
        in_specs=[pl.BlockSpec((bm, k), lambda i, j: (i, 0)),
                  pl.BlockSpec((k, bn), lambda i, j: (0, j)),
                  pl.BlockSpec((k, bn), lambda i, j: (0, j)),
                  pl.BlockSpec((1, SUBLANES, bn), lambda i, j: (i, 0, j)),
                  pl.BlockSpec((CONV_WIDTH, bn), lambda i, j: (0, j)),
                  pl.BlockSpec((1, bn), lambda i, j: (0, j))],
        out_specs=pl.BlockSpec((bm, bn), lambda i, j: (i, j)),
        out_shape=jax.ShapeDtypeStruct((m, n), BF16),
        compiler_params=_cparams(("parallel", "parallel")),
    )(x, wg, wu, halo_g, cw, cb)


def _ffn_halo_rows(hn, batch, seq, bm=FFN_BM):
    d = hn.shape[1]
    tiles = seq // bm
    tail = hn.reshape(batch, tiles, bm, d)[:, :, bm - SUBLANES:, :]
    prev = jnp.concatenate([jnp.zeros_like(tail[:, :1]), tail[:, :-1]], axis=1)
    return prev.reshape(batch * tiles * SUBLANES, d)


FFN_OUT_COLS = 1024
FFN_RES_COLS = 256
FFN_NORM_ROWS = 256


def _ffn_out_kernel(a_ref, wd_ref, h_ref, fg_ref, o_ref, *, n_res):
    kk = pl.program_id(1)
    d = o_ref.shape[1]

    @pl.when(kk == 0)
    def _():
        o_ref[...] = jnp.zeros_like(o_ref)

    a = a_ref[...]
    for c0 in range(0, d, FFN_OUT_COLS):
        cols = slice(c0, c0 + FFN_OUT_COLS)
        o_ref[:, cols] += jnp.dot(a, wd_ref[:, cols], preferred_element_type=F32)

    for c in range(n_res):
        @pl.when(kk == c)
        def _(c=c):
            cols = slice(c * FFN_RES_COLS, (c + 1) * FFN_RES_COLS)
            o_ref[:, cols] += h_ref[...]

    @pl.when(kk == pl.num_programs(1) - 1)
    def _():
        fg = fg_ref[...]
        for r0 in range(0, o_ref.shape[0], FFN_NORM_ROWS):
            rows = slice(r0, r0 + FFN_NORM_ROWS)
            h3 = o_ref[rows, :]
            ms = jnp.mean(h3 * h3, axis=-1, keepdims=True)
            o_ref[rows, :] = (h3 * lax.rsqrt(ms + RMS_EPS)) * fg


def _ffn_out(a, wd, h2, fg, tm=1024, tk=512):
    m, ff = a.shape
    d = wd.shape[1]
    n_res = d // FFN_RES_COLS
    assert ff // tk >= n_res, "one residual slab per contraction step"
    return pl.pallas_call(
        functools.partial(_ffn_out_kernel, n_res=n_res),
        grid=(m // tm, ff // tk),
        in_specs=[pl.BlockSpec((tm, tk), lambda i, k: (i, k)),
                  pl.BlockSpec((tk, d), lambda i, k: (k, 0)),
                  pl.BlockSpec((tm, FFN_RES_COLS), lambda i, k: (i, jnp.minimum(k, n_res - 1))),
                  pl.BlockSpec((1, d), lambda i, k: (0, 0))],
        out_specs=pl.BlockSpec((tm, d), lambda i, k: (i, 0)),
        out_shape=jax.ShapeDtypeStruct((m, d), F32),
        compiler_params=_cparams(("parallel", "arbitrary")),
    )(a, wd, h2, fg.reshape(1, d))


def _layer(x2, mem2, rel_bias, batch, seq, mem_len, norm_mix_g, w_in, gla_w_gate2,
           gla_b_gate, gla_norm_g, w_out, norm_xattn_g, mem_norm_g, w_xq, w_xk, w_xv,
           w_xo, norm_ffn_g, w_ffn_gate, w_ffn_up, ffn_conv_w, ffn_conv_b, w_ffn_down,
           out_g):
    w_in_t = w_in.T
    w_gqkv = w_in_t[_W_GQKV:_W_GLR].astype(BF16)
    w_glr = jnp.pad(w_in_t[_W_GLR:_W_GR], ((0, LANES - GLA_LOWRANK), (0, 0))).astype(BF16)
    w_gr = w_in_t[_W_GR:_W_DQKV].astype(BF16)
    w_dqkv = w_in_t[_W_DQKV:_W_END].astype(BF16)
    w2_pad = jnp.pad(gla_w_gate2, ((0, LANES - GLA_LOWRANK), (0, 0)))
    ff_pad = D_FF_PAD - D_FF
    wg = _cast_pad_cols(w_ffn_gate, D_FF_PAD)
    wu = _cast_pad_cols(w_ffn_up, D_FF_PAD)
    wd = _cast_pad_rows(w_ffn_down, D_FF_PAD)
    cw = jnp.pad(ffn_conv_w, ((0, 0), (0, ff_pad)))
    cb = jnp.pad(ffn_conv_b, ((0, ff_pad),)).reshape(1, D_FF_PAD)
    w_out_b = w_out.astype(BF16)
    w_kv = jnp.concatenate([w_xk, w_xv], axis=1).astype(BF16)

    hn = _rmsnorm(x2, norm_mix_g)
    g_qkv = _matmul_nt(hn, w_gqkv, 1024, 1024, BF16)
    g_r = _matmul_nt(hn, w_gr, 1024, 1024, BF16)
    d_qkv = _matmul_nt(hn, w_dqkv, 1024, 1024, F32)
    glr = _matmul_nt(hn, w_glr, 1024, LANES, F32)
    b_cum = _gla_gate(glr, w2_pad, gla_b_gate.reshape(1, -1))
    o_gla = _gla(g_qkv, g_r, b_cum, gla_norm_g.reshape(1, -1), batch, seq)
    o_dil = _dilated(d_qkv.reshape(batch, seq, 3 * DIL_HEADS * DIL_HEAD_DIM),
                     _dil_bias(rel_bias), batch, seq)
    o_dil = o_dil.reshape(batch * seq, DIL_HEADS * DIL_HEAD_DIM)
    half = GLA_HEADS * GLA_DV
    h1 = _mix_out(o_gla, o_dil, w_out_b[:half], w_out_b[half:], x2)

    memn = _rmsnorm(mem2, mem_norm_g)
    kv = _matmul(memn, w_kv, 512, 512, BF16)
    kx = kv[:, :XATTN_WIDTH].reshape(batch, mem_len, XATTN_WIDTH)
    vx = kv[:, XATTN_WIDTH:].reshape(batch, mem_len, XATTN_WIDTH)
    h2, hn3 = _xattn(h1, norm_xattn_g, w_xq.astype(BF16), kx, vx, w_xo.astype(BF16),
                     norm_ffn_g, seq)

    halo_x = _ffn_halo_rows(hn3, batch, seq)
    halo_g = _matmul(halo_x, wg, halo_x.shape[0], 512, F32)
    halo_g = halo_g.reshape(-1, SUBLANES, D_FF_PAD)
    act = _ffn_in(hn3, wg, wu, halo_g, cw, cb)
    return _ffn_out(act, wd, h2, out_g)


def kernel(x, mem, rel_bias, norm_mix_g, w_in, gla_w_gate2, gla_b_gate, gla_norm_g, w_out,
           norm_xattn_g, mem_norm_g, w_xq, w_xk, w_xv, w_xo, norm_ffn_g, w_ffn_gate,
           w_ffn_up, ffn_conv_w, ffn_conv_b, w_ffn_down, final_norm_g):
    batch, seq, d = x.shape
    mem_len = mem.shape[1]
    depth = w_in.shape[0]
    assert depth == 1, "the fused final rmsnorm assumes a single layer"
    out = _layer(x.reshape(batch * seq, d), mem.reshape(batch * mem_len, d), rel_bias,
                 batch, seq, mem_len, norm_mix_g[0], w_in[0], gla_w_gate2[0],
                 gla_b_gate[0], gla_norm_g[0], w_out[0], norm_xattn_g[0], mem_norm_g[0],
                 w_xq[0], w_xk[0], w_xv[0], w_xo[0], norm_ffn_g[0], w_ffn_gate[0],
                 w_ffn_up[0], ffn_conv_w[0], ffn_conv_b[0], w_ffn_down[0], final_norm_g)
    return out.reshape(batch, seq, d)
```

```python
import functools
import math

import numpy as np
import jax
import jax.numpy as jnp
from jax import lax
from jax.experimental import pallas as pl
from jax.experimental.pallas import tpu as pltpu

F32 = jnp.float32
BF16 = jnp.bfloat16

D_MODEL = 4096
RMS_EPS = 1e-6
GLA_HEADS = 4
GLA_DV = 512
GLA_DK = 256
GLA_LOWRANK = 16
GLA_TAU = 16.0
GLA_CHUNK = 64
DIL_HEAD_DIM = 128
DIL_HEADS = 16
DIL_CONFIGS = ((128, 1), (512, 4), (2048, 16))
DIL_STEPS = 128
REL_BUCKETS = 32
REL_MAX_DIST = 2048
XATTN_HEADS = 4
XATTN_HEAD_DIM = 128
XATTN_WIDTH = 512
D_FF = 11008
CONV_WIDTH = 3
NEG_INF = -1e30
LOG2E = math.log2(math.e)

LANES = 128
SUBLANES = 8
ROW_ALIGN = 16
VMEM_LIMIT = 56 * 1024 * 1024

D_FF_PAD = 11264
FFN_BM = 1024

_W_GQKV = 0
_W_GLR = 4096
_W_GR = 4112
_W_DQKV = 6160
_W_END = 12304


def _cparams(sem):
    return pltpu.CompilerParams(dimension_semantics=sem, vmem_limit_bytes=VMEM_LIMIT)


def _rmsnorm_kernel(x_ref, g_ref, o_ref):
    x = x_ref[...]
    ms = jnp.mean(x * x, axis=-1, keepdims=True)
    o_ref[...] = ((x * lax.rsqrt(ms + RMS_EPS)) * g_ref[...]).astype(o_ref.dtype)


def _rmsnorm(x, g, tm=256):
    m, d = x.shape
    return pl.pallas_call(
        _rmsnorm_kernel,
        grid=(m // tm,),
        in_specs=[pl.BlockSpec((tm, d), lambda i: (i, 0)),
                  pl.BlockSpec((1, d), lambda i: (0, 0))],
        out_specs=pl.BlockSpec((tm, d), lambda i: (i, 0)),
        out_shape=jax.ShapeDtypeStruct((m, d), BF16),
        compiler_params=_cparams(("parallel",)),
    )(x, g.reshape(1, d))


def _cast_pad_cols_kernel(w_ref, o_ref):
    n = w_ref.shape[1]
    o_ref[:, :n] = w_ref[...].astype(o_ref.dtype)
    o_ref[:, n:] = jnp.zeros((o_ref.shape[0], o_ref.shape[1] - n), o_ref.dtype)


def _cast_pad_cols(w, n_pad, tm=256):
    k, n = w.shape
    return pl.pallas_call(
        _cast_pad_cols_kernel,
        grid=(k // tm,),
        in_specs=[pl.BlockSpec((tm, n), lambda i: (i, 0))],
        out_specs=pl.BlockSpec((tm, n_pad), lambda i: (i, 0)),
        out_shape=jax.ShapeDtypeStruct((k, n_pad), BF16),
        compiler_params=_cparams(("parallel",)),
    )(w)


def _cast_pad_rows_kernel(w_ref, o_ref, *, n_valid):
    valid = pl.program_id(0) < n_valid
    o_ref[...] = jnp.where(valid, w_ref[...], 0.0).astype(o_ref.dtype)


def _cast_pad_rows(w, k_pad, tm=256):
    k, n = w.shape
    n_valid = k // tm
    return pl.pallas_call(
        functools.partial(_cast_pad_rows_kernel, n_valid=n_valid),
        grid=(k_pad // tm,),
        in_specs=[pl.BlockSpec((tm, n), lambda i: (jnp.minimum(i, n_valid - 1), 0))],
        out_specs=pl.BlockSpec((tm, n), lambda i: (i, 0)),
        out_shape=jax.ShapeDtypeStruct((k_pad, n), BF16),
        compiler_params=_cparams(("parallel",)),
    )(w)


def _mm_nt_kernel(x_ref, wt_ref, o_ref):
    o_ref[...] = lax.dot_general(x_ref[...], wt_ref[...], (((1,), (1,)), ((), ())),
                                 preferred_element_type=F32).astype(o_ref.dtype)


def _matmul_nt(x, wt, bm, bn, out_dtype, row0=0, n=None):
    m, k = x.shape
    n = wt.shape[0] if n is None else n
    return pl.pallas_call(
        _mm_nt_kernel,
        grid=(m // bm, n // bn),
        in_specs=[pl.BlockSpec((bm, k), lambda i, j: (i, 0)),
                  pl.BlockSpec((pl.Element(bn), pl.Element(k)),
                               lambda i, j: (pl.multiple_of(row0 + j * bn, ROW_ALIGN), 0))],
        out_specs=pl.BlockSpec((bm, bn), lambda i, j: (i, j)),
        out_shape=jax.ShapeDtypeStruct((m, n), out_dtype),
        compiler_params=_cparams(("parallel", "parallel")),
    )(x, wt)


def _mm_kernel(x_ref, w_ref, o_ref):
    o_ref[...] = jnp.dot(x_ref[...], w_ref[...],
                         preferred_element_type=F32).astype(o_ref.dtype)


def _matmul(x, w, bm, bn, out_dtype):
    m, k = x.shape
    n = w.shape[1]
    return pl.pallas_call(
        _mm_kernel,
        grid=(m // bm, n // bn),
        in_specs=[pl.BlockSpec((bm, k), lambda i, j: (i, 0)),
                  pl.BlockSpec((k, bn), lambda i, j: (0, j))],
        out_specs=pl.BlockSpec((bm, bn), lambda i, j: (i, j)),
        out_shape=jax.ShapeDtypeStruct((m, n), out_dtype),
        compiler_params=_cparams(("parallel", "parallel")),
    )(x, w)


NORM_SLAB = 128


def _norm_mm_nt_kernel(x_ref, g_ref, wt_ref, o_ref, hn_ref):
    nt = (((1,), (1,)), ((), ()))

    @pl.when(pl.program_id(1) == 0)
    def _():
        g = g_ref[...]
        for r0 in range(0, x_ref.shape[0], NORM_SLAB):
            rows = slice(r0, r0 + NORM_SLAB)
            x = x_ref[rows, :]
            ms = jnp.mean(x * x, axis=-1, keepdims=True)
            hn_ref[rows, :] = ((x * lax.rsqrt(ms + RMS_EPS)) * g).astype(hn_ref.dtype)

    o_ref[...] = lax.dot_general(hn_ref[...], wt_ref[...], nt,
                                 preferred_element_type=F32).astype(o_ref.dtype)


def _norm_matmul_nt(x, g, wt, bm, bn, out_dtype, row0=0, n=None):
    m, k = x.shape
    n = wt.shape[0] if n is None else n
    return pl.pallas_call(
        _norm_mm_nt_kernel,
        grid=(m // bm, n // bn),
        in_specs=[pl.BlockSpec((bm, k), lambda i, j: (i, 0)),
                  pl.BlockSpec((1, k), lambda i, j: (0, 0)),
                  pl.BlockSpec((pl.Element(bn), pl.Element(k)),
                               lambda i, j: (pl.multiple_of(row0 + j * bn, ROW_ALIGN), 0))],
        out_specs=[pl.BlockSpec((bm, bn), lambda i, j: (i, j)),
                   pl.BlockSpec((bm, k), lambda i, j: (i, 0))],
        out_shape=[jax.ShapeDtypeStruct((m, n), out_dtype),
                   jax.ShapeDtypeStruct((m, k), BF16)],
        compiler_params=_cparams(("parallel", "arbitrary")),
    )(x, g.reshape(1, k), wt)


GLA_T = 512
GLA_GROUP = 4
GLA_CUMSUM_ROWS = 256


def _split_bf16(x):
    hi = x.astype(BF16)
    lo = (x - hi.astype(F32)).astype(BF16)
    return hi, lo


def _gla_gate_kernel(glr_ref, w2_ref, b2_ref, tri_ref, b_ref):
    x_hi, x_lo = _split_bf16(glr_ref[...])
    w_hi, w_lo = _split_bf16(w2_ref[...])
    z = (jnp.dot(x_hi, w_hi, preferred_element_type=F32)
         + jnp.dot(x_lo, w_hi, preferred_element_type=F32)
         + jnp.dot(x_hi, w_lo, preferred_element_type=F32)) + b2_ref[...]
    log_sig = jnp.minimum(z, 0.0) - jnp.log(1.0 + jnp.exp(-jnp.abs(z)))
    g_hi, g_lo = _split_bf16(log_sig / GLA_TAU)
    tri = tri_ref[...]
    t_sub = tri.shape[0]
    for r0 in range(0, g_hi.shape[0], t_sub):
        rows = slice(r0, r0 + t_sub)
        b_ref[rows, :] = (jnp.dot(tri, g_hi[rows], preferred_element_type=F32)
                          + jnp.dot(tri, g_lo[rows], preferred_element_type=F32))


def _chunk_tril(t):
    idx = np.arange(t)
    same = (idx[:, None] // GLA_CHUNK) == (idx[None, :] // GLA_CHUNK)
    return jnp.asarray((same & (idx[:, None] >= idx[None, :])).astype(np.float32), BF16)


def _gla_gate(glr, w2_pad, b2, t=GLA_T):
    m = glr.shape[0]
    n = w2_pad.shape[1]
    return pl.pallas_call(
        _gla_gate_kernel,
        grid=(m // t,),
        in_specs=[pl.BlockSpec((t, LANES), lambda i: (i, 0)),
                  pl.BlockSpec((LANES, n), lambda i: (0, 0)),
                  pl.BlockSpec((1, n), lambda i: (0, 0)),
                  pl.BlockSpec((GLA_CUMSUM_ROWS, GLA_CUMSUM_ROWS), lambda i: (0, 0))],
        out_specs=pl.BlockSpec((t, n), lambda i: (i, 0)),
        out_shape=jax.ShapeDtypeStruct((m, n), F32),
        compiler_params=_cparams(("parallel",)),
    )(glr, w2_pad, b2, _chunk_tril(GLA_CUMSUM_ROWS))


def _gla_kernel(q_ref, k_ref, v_ref, r_ref, b_ref, gn_ref, o_ref, state_ref):
    c_sz = GLA_CHUNK
    g_sz = GLA_GROUP * c_sz
    assert GLA_GROUP == 4

    @pl.when(pl.program_id(2) == 0)
    def _():
        state_ref[...] = jnp.zeros_like(state_ref)

    row = lax.broadcasted_iota(jnp.int32, (g_sz, g_sz), 0)
    col = lax.broadcasted_iota(jnp.int32, (g_sz, g_sz), 1)
    causal = row >= col
    gn = gn_ref[...]
    nt = (((1,), (1,)), ((), ()))
    tn = (((0,), (0,)), ((), ()))

    def cat(parts):
        return jnp.concatenate(parts, axis=0)

    def group(gi, carry):
        rows = pl.ds(pl.multiple_of(gi * g_sz, g_sz), g_sz)
        b_all = b_ref[rows, :]
        q_all = q_ref[rows, :].astype(F32) * (GLA_DK ** -0.5)
        k_all = k_ref[rows, :].astype(F32)
        v = v_ref[rows, :]
        st = state_ref[...]

        sl = [slice(c * c_sz, (c + 1) * c_sz) for c in range(GLA_GROUP)]
        b = [b_all[s] for s in sl]
        bl = [x[c_sz - 1:c_sz, :] for x in b]
        bm = [x[c_sz // 2:c_sz // 2 + 1, :] for x in b]
        q_start = [q_all[s] * jnp.exp(x) for s, x in zip(sl, b)]
        k_end = [k_all[s] * jnp.exp(t - x) for s, x, t in zip(sl, b, bl)]
        q_mid = [(q_all[s] * jnp.exp(x - m)).astype(BF16) for s, x, m in zip(sl, b, bm)]
        k_mid = [(k_all[s] * jnp.exp(m - x)).astype(BF16) for s, x, m in zip(sl, b, bm)]

        e1, e2, e3 = jnp.exp(bl[1]), jnp.exp(bl[2]), jnp.exp(bl[3])
        e0 = jnp.exp(bl[0])
        d01, d12, d23 = e0 * e1, e1 * e2, e2 * e3
        d012, d123 = d01 * e2, d12 * e3
        d_all = d012 * e3

        qs = cat([q_start[0], q_start[1] * e0, q_start[2] * d01, q_start[3] * d012]).astype(BF16)
        o = lax.dot_general(qs, st.astype(BF16), nt, preferred_element_type=F32)

        k_end_b = [x.astype(BF16) for x in k_end]
        a_r0 = lax.dot_general(q_mid[0], cat([k_mid[0], k_mid[1]]), nt,
                               preferred_element_type=F32)
        a_r1 = lax.dot_general(q_mid[1], cat([(k_end[0] * jnp.exp(bm[1])).astype(BF16),
                                              k_mid[1]]), nt, preferred_element_type=F32)
        a_r2 = lax.dot_general(q_mid[2], cat([k_mid[2], k_mid[3]]), nt,
                               preferred_element_type=F32)
        a_r3 = lax.dot_general(q_mid[3], cat([(k_end[2] * jnp.exp(bm[3])).astype(BF16),
                                              k_mid[3]]), nt, preferred_element_type=F32)
        a_off = lax.dot_general(cat([q_start[2], q_start[3] * e2]).astype(BF16),
                                cat([(k_end[0] * e1).astype(BF16), k_end_b[1]]), nt,
                                preferred_element_type=F32)
        zeros = jnp.zeros((2 * c_sz, 2 * c_sz), F32)
        att = jnp.concatenate([cat([a_r0, a_r1, a_off]), cat([zeros, a_r2, a_r3])], axis=1)
        att = jnp.where(causal, att, 0.0).astype(BF16)
        o = o + jnp.dot(att, v, preferred_element_type=F32)

        k_fin = cat([k_end[0] * d123, k_end[1] * d23, k_end[2] * e3, k_end[3]]).astype(BF16)
        kv_t = lax.dot_general(v, k_fin, tn, preferred_element_type=F32)
        state_new = st * d_all + kv_t

        ms = jnp.mean(o * o, axis=-1, keepdims=True)
        on = (o * lax.rsqrt(ms + RMS_EPS)) * gn
        r = r_ref[rows, :].astype(F32)
        gate = r / (1.0 + jnp.exp(-r))
        o_ref[rows, :] = (on * gate).astype(o_ref.dtype)
        state_ref[...] = state_new
        return carry

    lax.fori_loop(0, GLA_T // g_sz, group, 0, unroll=True)


def _gla(qkv, gr, b_cum, gn, batch, seq):
    t = GLA_T
    nt_ = seq // t
    m = batch * seq
    kb = (GLA_HEADS * GLA_DK) // GLA_DK
    vb = (2 * GLA_HEADS * GLA_DK) // GLA_DV
    return pl.pallas_call(
        _gla_kernel,
        grid=(batch, GLA_HEADS, nt_),
        in_specs=[
            pl.BlockSpec((t, GLA_DK), lambda b, h, s: (b * nt_ + s, h)),
            pl.BlockSpec((t, GLA_DK), lambda b, h, s: (b * nt_ + s, kb + h)),
            pl.BlockSpec((t, GLA_DV), lambda b, h, s: (b * nt_ + s, vb + h)),
            pl.BlockSpec((t, GLA_DV), lambda b, h, s: (b * nt_ + s, h)),
            pl.BlockSpec((t, GLA_DK), lambda b, h, s: (b * nt_ + s, h)),
            pl.BlockSpec((1, GLA_DV), lambda b, h, s: (0, 0)),
        ],
        out_specs=pl.BlockSpec((t, GLA_DV), lambda b, h, s: (b * nt_ + s, h)),
        out_shape=jax.ShapeDtypeStruct((m, GLA_HEADS * GLA_DV), BF16),
        scratch_shapes=[pltpu.VMEM((GLA_DV, GLA_DK), F32)],
        compiler_params=_cparams(("parallel", "parallel", "arbitrary")),
    )(qkv, qkv, qkv, gr, b_cum, gn)


def _t5_bucket_np(dist):
    max_exact = REL_BUCKETS // 2
    d_f = np.maximum(dist, 1).astype(np.float32)
    large = max_exact + (np.log(d_f / np.float32(max_exact))
                         / np.float32(math.log(REL_MAX_DIST / max_exact))
                         * np.float32(REL_BUCKETS - max_exact)).astype(np.int32)
    large = np.minimum(large, REL_BUCKETS - 1)
    return np.where(dist < max_exact, dist, large)


def _dil_bucket_index():
    steps = DIL_STEPS
    qi = np.arange(steps)[:, None]
    kj = np.arange(2 * steps)[None, :]
    rel = qi + steps - kj
    band = (rel >= 0) & (rel <= steps)
    out = []
    for _, dil in DIL_CONFIGS:
        bucket = _t5_bucket_np(np.clip(rel, 0, steps) * dil)
        out.append(np.where(band, bucket, REL_BUCKETS))
    return np.stack(out).astype(np.int32)


def _dil_bias_kernel(relb_ref, idx_ref, o_ref):
    head = pl.program_id(1)
    idx = idx_ref[0]
    bias = jnp.full(idx.shape, NEG_INF, F32)
    for bkt in range(REL_BUCKETS):
        bias = jnp.where(idx == bkt, relb_ref[bkt, head] * LOG2E, bias)
    o_ref[0, 0] = bias


def _dil_bias(rel_bias):
    steps = DIL_STEPS
    ncfg = len(DIL_CONFIGS)
    idx = jnp.asarray(_dil_bucket_index())
    return pl.pallas_call(
        _dil_bias_kernel,
        grid=(ncfg, DIL_HEADS),
        in_specs=[pl.BlockSpec(memory_space=pltpu.SMEM),
                  pl.BlockSpec((1, steps, 2 * steps), lambda c, h: (c, 0, 0))],
        out_specs=pl.BlockSpec((1, 1, steps, 2 * steps), lambda c, h: (c, h, 0, 0)),
        out_shape=jax.ShapeDtypeStruct((ncfg, DIL_HEADS, steps, 2 * steps), F32),
        compiler_params=_cparams(("parallel", "parallel")),
    )(rel_bias, idx)


DIL_GROUP = 8


def _dil_kernel(bias_ref, q_ref, k_ref, v_ref, o_ref, m_sc, l_sc, acc_sc, *, seq):
    steps = DIL_STEPS
    e = DIL_HEAD_DIM
    scale = e ** -0.5 * LOG2E
    nt = (((1,), (1,)), ((), ()))
    n_cfg = len(DIL_CONFIGS)

    def rows_of(start, size, dil):
        if dil == 1:
            return pl.ds(start, size)
        return pl.ds(start, size, stride=dil)

    def attend(c, dil, blocks, merge):
        q_rows = [rows_of(q_start, steps, dil) for q_start, _, _, _ in blocks]
        k_rows = [rows_of(k_start, n_keys, dil) for _, k_start, n_keys, _ in blocks]
        logits = []
        for qr, kr, (_, _, _, bias) in zip(q_rows, k_rows, blocks):
            q = q_ref[0, qr, :].astype(BF16)
            k = k_ref[0, kr, :].astype(BF16)
            logits.append(lax.dot_general(q, k, nt, preferred_element_type=F32) * scale + bias)
        stats = []
        for s in logits:
            m = jnp.max(s, axis=-1, keepdims=True)
            p = jnp.exp2(s - m)
            stats.append((m, jnp.sum(p, axis=-1, keepdims=True), p.astype(BF16)))
        pvs = [jnp.dot(p, v_ref[0, kr, :].astype(BF16), preferred_element_type=F32)
               for (_, _, p), kr in zip(stats, k_rows)]
        if not merge:
            for qr, (m, l, _), pv in zip(q_rows, stats, pvs):
                m_sc[c - 1, qr, :] = jnp.broadcast_to(m, (steps, e))
                l_sc[c - 1, qr, :] = jnp.broadcast_to(l, (steps, e))
                acc_sc[c - 1, qr, :] = pv
            return
        outs = []
        for qr, (m, l, _), pv in zip(q_rows, stats, pvs):
            ms = [m] + [m_sc[i, qr, :] for i in range(n_cfg - 1)]
            ls = [l] + [l_sc[i, qr, :] for i in range(n_cfg - 1)]
            accs = [pv] + [acc_sc[i, qr, :] for i in range(n_cfg - 1)]
            m_max = functools.reduce(jnp.maximum, ms)
            num = None
            den = None
            for m_i, l_i, acc_i in zip(ms, ls, accs):
                wgt = jnp.exp2(m_i - m_max)
                num = wgt * acc_i if num is None else num + wgt * acc_i
                den = wgt * l_i if den is None else den + wgt * l_i
            outs.append((qr, (num / den).astype(o_ref.dtype)))
        for qr, o in outs:
            o_ref[0, qr, :] = o

    for c in reversed(range(n_cfg)):
        dil = DIL_CONFIGS[c][1]
        nb = seq // dil // steps
        span = steps * dil

        def first(r, c=c):
            return (r, r, steps, bias_ref[c, 0, :, steps:2 * steps])

        def later(j, c=c, dil=dil, nb=nb, span=span):
            if dil == 1:
                q_start = (1 + j) * span
                if not isinstance(q_start, int):
                    q_start = pl.multiple_of(q_start, span)
                return (q_start, q_start - span, 2 * steps, bias_ref[c, 0])
            r = j // (nb - 1)
            n = 1 + j % (nb - 1)
            return (n * span + r, (n - 1) * span + r, 2 * steps, bias_ref[c, 0])

        for make, count in ((first, dil), (later, dil * (nb - 1))):
            full, rest = divmod(count, DIL_GROUP)

            def group(gi, carry, c=c, dil=dil, make=make):
                attend(c, dil, [make(gi * DIL_GROUP + u) for u in range(DIL_GROUP)], c == 0)
                return carry

            if full:
                lax.fori_loop(0, full, group, 0)
            if rest:
                attend(c, dil, [make(full * DIL_GROUP + u) for u in range(rest)], c == 0)


def _dilated(qkv3, bias, batch, seq):
    e = DIL_HEAD_DIM
    steps = DIL_STEPS
    ncfg = len(DIL_CONFIGS)
    assert DIL_CONFIGS[0][1] == 1, "the config that writes the output rows must be undilated"
    return pl.pallas_call(
        functools.partial(_dil_kernel, seq=seq),
        grid=(batch, DIL_HEADS),
        in_specs=[
            pl.BlockSpec((ncfg, 1, steps, 2 * steps), lambda b, h: (0, h, 0, 0)),
            pl.BlockSpec((1, seq, e), lambda b, h: (b, 0, h)),
            pl.BlockSpec((1, seq, e), lambda b, h: (b, 0, DIL_HEADS + h)),
            pl.BlockSpec((1, seq, e), lambda b, h: (b, 0, 2 * DIL_HEADS + h)),
        ],
        out_specs=pl.BlockSpec((1, seq, e), lambda b, h: (b, 0, h)),
        out_shape=jax.ShapeDtypeStruct((batch, seq, DIL_HEADS * e), BF16),
        scratch_shapes=[pltpu.VMEM((ncfg - 1, seq, e), F32),
                        pltpu.VMEM((ncfg - 1, seq, e), F32),
                        pltpu.VMEM((ncfg - 1, seq, e), F32)],
        compiler_params=_cparams(("parallel", "parallel")),
    )(bias, qkv3, qkv3, qkv3)


def _mix_out_kernel(a_ref, b_ref, wa_ref, wb_ref, x_ref, o_ref):
    acc = jnp.dot(a_ref[...], wa_ref[...], preferred_element_type=F32)
    acc = acc + jnp.dot(b_ref[...], wb_ref[...], preferred_element_type=F32)
    o_ref[...] = x_ref[...] + acc


def _mix_out(o_gla, o_dil, w, x, bm=1024, bn=1024):
    m, ka = o_gla.shape
    kb = o_dil.shape[1]
    n = w.shape[1]
    assert ka == kb and w.shape[0] == ka + kb
    return pl.pallas_call(
        _mix_out_kernel,
        grid=(m // bm, n // bn),
        in_specs=[pl.BlockSpec((bm, ka), lambda i, j: (i, 0)),
                  pl.BlockSpec((bm, kb), lambda i, j: (i, 0)),
                  pl.BlockSpec((ka, bn), lambda i, j: (0, j)),
                  pl.BlockSpec((kb, bn), lambda i, j: (1, j)),
                  pl.BlockSpec((bm, bn), lambda i, j: (i, j))],
        out_specs=pl.BlockSpec((bm, bn), lambda i, j: (i, j)),
        out_shape=jax.ShapeDtypeStruct((m, n), F32),
        compiler_params=_cparams(("parallel", "parallel")),
    )(o_gla, o_dil, w, w, x)


XATTN_SLAB = 128


def _xattn_kernel(h_ref, gx_ref, wq_ref, k_ref, v_ref, wo_ref, gf_ref, h2_ref, hn_ref):
    e = XATTN_HEAD_DIM
    nt = (((1,), (1,)), ((), ()))
    gx = gx_ref[...]
    gf = gf_ref[...]
    slabs = [slice(r0, r0 + XATTN_SLAB) for r0 in range(0, h_ref.shape[0], XATTN_SLAB)]
    hs = [h_ref[rows, :] for rows in slabs]
    hns = []
    for h in hs:
        ms = jnp.mean(h * h, axis=-1, keepdims=True)
        hns.append(((h * lax.rsqrt(ms + RMS_EPS)) * gx).astype(BF16))
    qs = [jnp.dot(hn, wq_ref[...], preferred_element_type=F32).astype(BF16) for hn in hns]
    os_ = []
    for q in qs:
        outs = []
        for hh in range(XATTN_HEADS):
            qh = q[:, hh * e:(hh + 1) * e]
            kh = k_ref[0, :, hh * e:(hh + 1) * e]
            vh = v_ref[0, :, hh * e:(hh + 1) * e]
            s = lax.dot_general(qh, kh, nt, preferred_element_type=F32) * (e ** -0.5)
            m = jnp.max(s, axis=-1, keepdims=True)
            p = jnp.exp(s - m)
            p = p / jnp.sum(p, axis=-1, keepdims=True)
            outs.append(jnp.dot(p.astype(BF16), vh, preferred_element_type=F32))
        os_.append(jnp.concatenate(outs, axis=1).astype(BF16))
    h2s = [h + jnp.dot(o, wo_ref[...], preferred_element_type=F32) for h, o in zip(hs, os_)]
    hn2s = []
    for h2 in h2s:
        ms2 = jnp.mean(h2 * h2, axis=-1, keepdims=True)
        hn2s.append(((h2 * lax.rsqrt(ms2 + RMS_EPS)) * gf).astype(hn_ref.dtype))
    for rows, h2, hn2 in zip(slabs, h2s, hn2s):
        h2_ref[rows, :] = h2
        hn_ref[rows, :] = hn2


def _xattn(h1, gx, wq, kx, vx, wo, gf, seq, tm=256):
    m, d = h1.shape
    mem_len = kx.shape[1]
    per_seq = seq // tm
    return pl.pallas_call(
        _xattn_kernel,
        grid=(m // tm,),
        in_specs=[pl.BlockSpec((tm, d), lambda i: (i, 0)),
                  pl.BlockSpec((1, d), lambda i: (0, 0)),
                  pl.BlockSpec((d, XATTN_WIDTH), lambda i: (0, 0)),
                  pl.BlockSpec((1, mem_len, XATTN_WIDTH), lambda i: (i // per_seq, 0, 0)),
                  pl.BlockSpec((1, mem_len, XATTN_WIDTH), lambda i: (i // per_seq, 0, 0)),
                  pl.BlockSpec((XATTN_WIDTH, d), lambda i: (0, 0)),
                  pl.BlockSpec((1, d), lambda i: (0, 0))],
        out_specs=[pl.BlockSpec((tm, d), lambda i: (i, 0)),
                   pl.BlockSpec((tm, d), lambda i: (i, 0))],
        out_shape=[jax.ShapeDtypeStruct((m, d), F32),
                   jax.ShapeDtypeStruct((m, d), BF16)],
        compiler_params=_cparams(("parallel",)),
    )(h1, gx.reshape(1, d), wq, kx, vx, wo, gf.reshape(1, d))


def _ffn_in_kernel(x_ref, wg_ref, wu_ref, halo_ref, cw_ref, cb_ref, a_ref):
    x = x_ref[...]
    g = jnp.dot(x, wg_ref[...], preferred_element_type=F32)
    u = jnp.dot(x, wu_ref[...], preferred_element_type=F32)
    halo = halo_ref[0]
    prev1 = halo[SUBLANES - 1:SUBLANES, :]
    prev2 = halo[SUBLANES - 2:SUBLANES - 1, :]
    row = lax.broadcasted_iota(jnp.int32, (SUBLANES, g.shape[1]), 0)
    r1 = pltpu.roll(g, 1, axis=0)
    r2 = pltpu.roll(g, 2, axis=0)
    head1 = jnp.where(row == 0, prev1, r1[:SUBLANES])
    head2 = jnp.where(row == 0, prev2, jnp.where(row == 1, prev1, r2[:SUBLANES]))
    g_m1 = jnp.concatenate([head1, r1[SUBLANES:]], axis=0)
    g_m2 = jnp.concatenate([head2, r2[SUBLANES:]], axis=0)
    cw = cw_ref[...]
    y = cb_ref[...] + g_m2 * cw[0:1, :]
    y = y + g_m1 * cw[1:2, :]
    y = y + g * cw[2:3, :]
    a_ref[...] = ((y / (1.0 + jnp.exp(-y))) * u).astype(a_ref.dtype)


def _ffn_in(x, wg, wu, halo_g, cw, cb, bm=FFN_BM, bn=512):
    m, k = x.shape
    n = wg.shape[1]
    return pl.pallas_call(
        _ffn_in_kernel,
        grid=(m // bm, n // bn),
        in_specs=[pl.BlockSpec((bm, k), lambda i, j: (i, 0)),
                  pl.BlockSpec((k, bn), lambda i, j: (0, j)),
                  pl.BlockSpec((k, bn), lambda i, j: (0, j)),
                  pl.BlockSpec((1, SUBLANES, bn), lambda i, j: (i, 0, j)),
                  pl.BlockSpec((CONV_WIDTH, bn), lambda i, j: (0, j)),
                  pl.BlockSpec((1, bn), lambda i, j: (0, j))],
        out_specs=pl.BlockSpec((bm, bn), lambda i, j: (i, j)),
        out_shape=jax.ShapeDtypeStruct((m, n), BF16),
        compiler_params=_cparams(("parallel", "parallel")),
    )(x, wg, wu, halo_g, cw, cb)


def _ffn_halo_rows(hn, batch, seq, bm=FFN_BM):
    d = hn.shape[1]
    tiles = seq // bm
    tail = hn.reshape(batch, tiles, bm, d)[:, :, bm - SUBLANES:, :]
    prev = jnp.concatenate([jnp.zeros_like(tail[:, :1]), tail[:, :-1]], axis=1)
    return prev.reshape(batch * tiles * SUBLANES, d)


FFN_OUT_COLS = 1024
FFN_RES_COLS = 256
FFN_NORM_ROWS = 256


def _ffn_out_kernel(a_ref, wd_ref, h_ref, fg_ref, o_ref, *, n_res):
    kk = pl.program_id(1)
    d = o_ref.shape[1]

    @pl.when(kk == 0)
    def _():
        o_ref[...] = jnp.zeros_like(o_ref)

    a = a_ref[...]
    for c0 in range(0, d, FFN_OUT_COLS):
        cols = slice(c0, c0 + FFN_OUT_COLS)
        o_ref[:, cols] += jnp.dot(a, wd_ref[:, cols], preferred_element_type=F32)

    for c in range(n_res):
        @pl.when(kk == c)
        def _(c=c):
            cols = slice(c * FFN_RES_COLS, (c + 1) * FFN_RES_COLS)
            o_ref[:, cols] += h_ref[...]

    @pl.when(kk == pl.num_programs(1) - 1)
    def _():
        fg = fg_ref[...]
        for r0 in range(0, o_ref.shape[0], FFN_NORM_ROWS):
            rows = slice(r0, r0 + FFN_NORM_ROWS)
            h3 = o_ref[rows, :]
            ms = jnp.mean(h3 * h3, axis=-1, keepdims=True)
            o_ref[rows, :] = (h3 * lax.rsqrt(ms + RMS_EPS)) * fg


def _ffn_out(a, wd, h2, fg, tm=1024, tk=512):
    m, ff = a.shape
    d = wd.shape[1]
    n_res = d // FFN_RES_COLS
    assert ff // tk >= n_res, "one residual slab per contraction step"
    return pl.pallas_call(
        functools.partial(_ffn_out_kernel, n_res=n_res),
        grid=(m // tm, ff // tk),
        in_specs=[pl.BlockSpec((tm, tk), lambda i, k: (i, k)),
                  pl.BlockSpec((tk, d), lambda i, k: (k, 0)),
                  pl.BlockSpec((tm, FFN_RES_COLS), lambda i, k: (i, jnp.minimum(k, n_res - 1))),
                  pl.BlockSpec((1, d), lambda i, k: (0, 0))],
        out_specs=pl.BlockSpec((tm, d), lambda i, k: (i, 0)),
        out_shape=jax.ShapeDtypeStruct((m, d), F32),
        compiler_params=_cparams(("parallel", "arbitrary")),
    )(a, wd, h2, fg.reshape(1, d))


def _layer(x2, mem2, rel_bias, batch, seq, mem_len, norm_mix_g, w_in, gla_w_gate2,
           gla_b_gate, gla_norm_g, w_out, norm_xattn_g, mem_norm_g, w_xq, w_xk, w_xv,
           w_xo, norm_ffn_g, w_ffn_gate, w_ffn_up, ffn_conv_w, ffn_conv_b, w_ffn_down,
           out_g):
    w_in_t = w_in.T.astype(BF16)
    w_glr = jnp.pad(w_in_t[_W_GLR:_W_GR], ((0, LANES - GLA_LOWRANK), (0, 0)))
    w2_pad = jnp.pad(gla_w_gate2, ((0, LANES - GLA_LOWRANK), (0, 0)))
    ff_pad = D_FF_PAD - D_FF
    wg = _cast_pad_cols(w_ffn_gate, D_FF_PAD)
    wu = _cast_pad_cols(w_ffn_up, D_FF_PAD)
    wd = _cast_pad_rows(w_ffn_down, D_FF_PAD)
    cw = jnp.pad(ffn_conv_w, ((0, 0), (0, ff_pad)))
    cb = jnp.pad(ffn_conv_b, ((0, ff_pad),)).reshape(1, D_FF_PAD)
    w_out_b = w_out.astype(BF16)
    w_kv = jnp.concatenate([w_xk, w_xv], axis=1).astype(BF16)

    d_qkv, hn = _norm_matmul_nt(x2, norm_mix_g, w_in_t, 512, 1024, F32,
                                row0=_W_DQKV, n=_W_END - _W_DQKV)
    g_qkv = _matmul_nt(hn, w_in_t, 1024, 1024, BF16, row0=_W_GQKV, n=_W_GLR - _W_GQKV)
    g_r = _matmul_nt(hn, w_in_t, 1024, 1024, BF16, row0=_W_GR, n=_W_DQKV - _W_GR)
    glr = _matmul_nt(hn, w_glr, 1024, LANES, F32)
    b_cum = _gla_gate(glr, w2_pad, gla_b_gate.reshape(1, -1))
    o_gla = _gla(g_qkv, g_r, b_cum, gla_norm_g.reshape(1, -1), batch, seq)
    o_dil = _dilated(d_qkv.reshape(batch, seq, 3 * DIL_HEADS * DIL_HEAD_DIM),
                     _dil_bias(rel_bias), batch, seq)
    o_dil = o_dil.reshape(batch * seq, DIL_HEADS * DIL_HEAD_DIM)
    h1 = _mix_out(o_gla, o_dil, w_out_b, x2)

    memn = _rmsnorm(mem2, mem_norm_g)
    kv = _matmul(memn, w_kv, 512, 512, BF16)
    kx = kv[:, :XATTN_WIDTH].reshape(batch, mem_len, XATTN_WIDTH)
    vx = kv[:, XATTN_WIDTH:].reshape(batch, mem_len, XATTN_WIDTH)
    h2, hn3 = _xattn(h1, norm_xattn_g, w_xq.astype(BF16), kx, vx, w_xo.astype(BF16),
                     norm_ffn_g, seq)

    halo_x = _ffn_halo_rows(hn3, batch, seq)
    halo_g = _matmul(halo_x, wg, halo_x.shape[0], 512, F32)
    halo_g = halo_g.reshape(-1, SUBLANES, D_FF_PAD)
    act = _ffn_in(hn3, wg, wu, halo_g, cw, cb)
    return _ffn_out(act, wd, h2, out_g)


def kernel(x, mem, rel_bias, norm_mix_g, w_in, gla_w_gate2, gla_b_gate, gla_norm_g, w_out,
           norm_xattn_g, mem_norm_g, w_xq, w_xk, w_xv, w_xo, norm_ffn_g, w_ffn_gate,
           w_ffn_up, ffn_conv_w, ffn_conv_b, w_ffn_down, final_norm_g):
    batch, seq, d = x.shape
    mem_len = mem.shape[1]
    depth = w_in.shape[0]
    assert depth == 1, "the fused final rmsnorm assumes a single layer"
    out = _layer(x.reshape(batch * seq, d), mem.reshape(batch * mem_len, d), rel_bias,
                 batch, seq, mem_len, norm_mix_g[0], w_in[0], gla_w_gate2[0],
                 gla_b_gate[0], gla_norm_g[0], w_out[0], norm_xattn_g[0], mem_norm_g[0],
                 w_xq[0], w_xk[0], w_xv[0], w_xo[0], norm_ffn_g[0], w_ffn_gate[0],
                 w_ffn_up[0], ffn_conv_w[0], ffn_conv_b[0], w_ffn_down[0], final_norm_g)
    return out.reshape(batch, seq, d)
```

```python
import functools
import math

import numpy as np
import jax
import jax.numpy as jnp
from jax import lax
from jax.experimental import pallas as pl
from jax.experimental.pallas import tpu as pltpu

F32 = jnp.float32
BF16 = jnp.bfloat16

D_MODEL = 4096
RMS_EPS = 1e-6
GLA_HEADS = 4
GLA_DV = 512
GLA_DK = 256
GLA_LOWRANK = 16
GLA_TAU = 16.0
GLA_CHUNK = 64
DIL_HEAD_DIM = 128
DIL_HEADS = 16
DIL_CONFIGS = ((128, 1), (512, 4), (2048, 16))
DIL_STEPS = 128
REL_BUCKETS = 32
REL_MAX_DIST = 2048
XATTN_HEADS = 4
XATTN_HEAD_DIM = 128
XATTN_WIDTH = 512
D_FF = 11008
CONV_WIDTH = 3
NEG_INF = -1e30
LOG2E = math.log2(math.e)

LANES = 128
SUBLANES = 8
ROW_ALIGN = 16
VMEM_LIMIT = 56 * 1024 * 1024

D_FF_PAD = 11264
FFN_BM = 1024

_W_GQKV = 0
_W_GLR = 4096
_W_GR = 4112
_W_DQKV = 6160
_W_END = 12304


def _cparams(sem):
    return pltpu.CompilerParams(dimension_semantics=sem, vmem_limit_bytes=VMEM_LIMIT)


def _rmsnorm_kernel(x_ref, g_ref, o_ref):
    x = x_ref[...]
    ms = jnp.mean(x * x, axis=-1, keepdims=True)
    o_ref[...] = ((x * lax.rsqrt(ms + RMS_EPS)) * g_ref[...]).astype(o_ref.dtype)


def _rmsnorm(x, g, tm=256):
    m, d = x.shape
    return pl.pallas_call(
        _rmsnorm_kernel,
        grid=(m // tm,),
        in_specs=[pl.BlockSpec((tm, d), lambda i: (i, 0)),
                  pl.BlockSpec((1, d), lambda i: (0, 0))],
        out_specs=pl.BlockSpec((tm, d), lambda i: (i, 0)),
        out_shape=jax.ShapeDtypeStruct((m, d), BF16),
        compiler_params=_cparams(("parallel",)),
    )(x, g.reshape(1, d))


def _cast_pad_cols_kernel(w_ref, o_ref):
    n = w_ref.shape[1]
    o_ref[:, :n] = w_ref[...].astype(o_ref.dtype)
    o_ref[:, n:] = jnp.zeros((o_ref.shape[0], o_ref.shape[1] - n), o_ref.dtype)


def _cast_pad_cols(w, n_pad, tm=256):
    k, n = w.shape
    return pl.pallas_call(
        _cast_pad_cols_kernel,
        grid=(k // tm,),
        in_specs=[pl.BlockSpec((tm, n), lambda i: (i, 0))],
        out_specs=pl.BlockSpec((tm, n_pad), lambda i: (i, 0)),
        out_shape=jax.ShapeDtypeStruct((k, n_pad), BF16),
        compiler_params=_cparams(("parallel",)),
    )(w)


def _cast_pad_rows_kernel(w_ref, o_ref, *, n_valid):
    valid = pl.program_id(0) < n_valid
    o_ref[...] = jnp.where(valid, w_ref[...], 0.0).astype(o_ref.dtype)


def _cast_pad_rows(w, k_pad, tm=256):
    k, n = w.shape
    n_valid = k // tm
    return pl.pallas_call(
        functools.partial(_cast_pad_rows_kernel, n_valid=n_valid),
        grid=(k_pad // tm,),
        in_specs=[pl.BlockSpec((tm, n), lambda i: (jnp.minimum(i, n_valid - 1), 0))],
        out_specs=pl.BlockSpec((tm, n), lambda i: (i, 0)),
        out_shape=jax.ShapeDtypeStruct((k_pad, n), BF16),
        compiler_params=_cparams(("parallel",)),
    )(w)


def _mm_nt_kernel(x_ref, wt_ref, o_ref):
    o_ref[...] = lax.dot_general(x_ref[...], wt_ref[...], (((1,), (1,)), ((), ())),
                                 preferred_element_type=F32).astype(o_ref.dtype)


def _matmul_nt(x, wt, bm, bn, out_dtype, row0=0, n=None):
    m, k = x.shape
    n = wt.shape[0] if n is None else n
    return pl.pallas_call(
        _mm_nt_kernel,
        grid=(m // bm, n // bn),
        in_specs=[pl.BlockSpec((bm, k), lambda i, j: (i, 0)),
                  pl.BlockSpec((pl.Element(bn), pl.Element(k)),
                               lambda i, j: (pl.multiple_of(row0 + j * bn, ROW_ALIGN), 0))],
        out_specs=pl.BlockSpec((bm, bn), lambda i, j: (i, j)),
        out_shape=jax.ShapeDtypeStruct((m, n), out_dtype),
        compiler_params=_cparams(("parallel", "parallel")),
    )(x, wt)


def _mm_kernel(x_ref, w_ref, o_ref):
    o_ref[...] = jnp.dot(x_ref[...], w_ref[...],
                         preferred_element_type=F32).astype(o_ref.dtype)


def _matmul(x, w, bm, bn, out_dtype):
    m, k = x.shape
    n = w.shape[1]
    return pl.pallas_call(
        _mm_kernel,
        grid=(m // bm, n // bn),
        in_specs=[pl.BlockSpec((bm, k), lambda i, j: (i, 0)),
                  pl.BlockSpec((k, bn), lambda i, j: (0, j))],
        out_specs=pl.BlockSpec((bm, bn), lambda i, j: (i, j)),
        out_shape=jax.ShapeDtypeStruct((m, n), out_dtype),
        compiler_params=_cparams(("parallel", "parallel")),
    )(x, w)


GLA_T = 512
GLA_GROUP = 4
GLA_CUMSUM_ROWS = 256


def _split_bf16(x):
    hi = x.astype(BF16)
    lo = (x - hi.astype(F32)).astype(BF16)
    return hi, lo


def _gla_gate_kernel(glr_ref, w2_ref, b2_ref, tri_ref, b_ref):
    x_hi, x_lo = _split_bf16(glr_ref[...])
    w_hi, w_lo = _split_bf16(w2_ref[...])
    z = (jnp.dot(x_hi, w_hi, preferred_element_type=F32)
         + jnp.dot(x_lo, w_hi, preferred_element_type=F32)
         + jnp.dot(x_hi, w_lo, preferred_element_type=F32)) + b2_ref[...]
    log_sig = jnp.minimum(z, 0.0) - jnp.log(1.0 + jnp.exp(-jnp.abs(z)))
    g_hi, g_lo = _split_bf16(log_sig / GLA_TAU)
    tri = tri_ref[...]
    t_sub = tri.shape[0]
    for r0 in range(0, g_hi.shape[0], t_sub):
        rows = slice(r0, r0 + t_sub)
        b_ref[rows, :] = (jnp.dot(tri, g_hi[rows], preferred_element_type=F32)
                          + jnp.dot(tri, g_lo[rows], preferred_element_type=F32))


def _chunk_tril(t):
    idx = np.arange(t)
    same = (idx[:, None] // GLA_CHUNK) == (idx[None, :] // GLA_CHUNK)
    return jnp.asarray((same & (idx[:, None] >= idx[None, :])).astype(np.float32), BF16)


def _gla_gate(glr, w2_pad, b2, t=GLA_T):
    m = glr.shape[0]
    n = w2_pad.shape[1]
    return pl.pallas_call(
        _gla_gate_kernel,
        grid=(m // t,),
        in_specs=[pl.BlockSpec((t, LANES), lambda i: (i, 0)),
                  pl.BlockSpec((LANES, n), lambda i: (0, 0)),
                  pl.BlockSpec((1, n), lambda i: (0, 0)),
                  pl.BlockSpec((GLA_CUMSUM_ROWS, GLA_CUMSUM_ROWS), lambda i: (0, 0))],
        out_specs=pl.BlockSpec((t, n), lambda i: (i, 0)),
        out_shape=jax.ShapeDtypeStruct((m, n), F32),
        compiler_params=_cparams(("parallel",)),
    )(glr, w2_pad, b2, _chunk_tril(GLA_CUMSUM_ROWS))


def _gla_kernel(q_ref, k_ref, v_ref, r_ref, b_ref, gn_ref, o_ref, state_ref):
    c_sz = GLA_CHUNK
    g_sz = GLA_GROUP * c_sz
    assert GLA_GROUP == 4

    @pl.when(pl.program_id(2) == 0)
    def _():
        state_ref[...] = jnp.zeros_like(state_ref)

    row = lax.broadcasted_iota(jnp.int32, (g_sz, g_sz), 0)
    col = lax.broadcasted_iota(jnp.int32, (g_sz, g_sz), 1)
    causal = row >= col
    gn = gn_ref[...]
    nt = (((1,), (1,)), ((), ()))
    tn = (((0,), (0,)), ((), ()))

    def cat(parts):
        return jnp.concatenate(parts, axis=0)

    def group(gi, carry):
        rows = pl.ds(pl.multiple_of(gi * g_sz, g_sz), g_sz)
        b_all = b_ref[rows, :]
        q_all = q_ref[rows, :].astype(F32) * (GLA_DK ** -0.5)
        k_all = k_ref[rows, :].astype(F32)
        v = v_ref[rows, :]
        st = state_ref[...]

        sl = [slice(c * c_sz, (c + 1) * c_sz) for c in range(GLA_GROUP)]
        b = [b_all[s] for s in sl]
        bl = [x[c_sz - 1:c_sz, :] for x in b]
        bm = [x[c_sz // 2:c_sz // 2 + 1, :] for x in b]
        q_start = [q_all[s] * jnp.exp(x) for s, x in zip(sl, b)]
        k_end = [k_all[s] * jnp.exp(t - x) for s, x, t in zip(sl, b, bl)]
        q_mid = [(q_all[s] * jnp.exp(x - m)).astype(BF16) for s, x, m in zip(sl, b, bm)]
        k_mid = [(k_all[s] * jnp.exp(m - x)).astype(BF16) for s, x, m in zip(sl, b, bm)]

        e1, e2, e3 = jnp.exp(bl[1]), jnp.exp(bl[2]), jnp.exp(bl[3])
        e0 = jnp.exp(bl[0])
        d01, d12, d23 = e0 * e1, e1 * e2, e2 * e3
        d012, d123 = d01 * e2, d12 * e3
        d_all = d012 * e3

        qs = cat([q_start[0], q_start[1] * e0, q_start[2] * d01, q_start[3] * d012]).astype(BF16)
        o = lax.dot_general(qs, st.astype(BF16), nt, preferred_element_type=F32)

        k_end_b = [x.astype(BF16) for x in k_end]
        a_r0 = lax.dot_general(q_mid[0], cat([k_mid[0], k_mid[1]]), nt,
                               preferred_element_type=F32)
        a_r1 = lax.dot_general(q_mid[1], cat([(k_end[0] * jnp.exp(bm[1])).astype(BF16),
                                              k_mid[1]]), nt, preferred_element_type=F32)
        a_r2 = lax.dot_general(q_mid[2], cat([k_mid[2], k_mid[3]]), nt,
                               preferred_element_type=F32)
        a_r3 = lax.dot_general(q_mid[3], cat([(k_end[2] * jnp.exp(bm[3])).astype(BF16),
                                              k_mid[3]]), nt, preferred_element_type=F32)
        a_off = lax.dot_general(cat([q_start[2], q_start[3] * e2]).astype(BF16),
                                cat([(k_end[0] * e1).astype(BF16), k_end_b[1]]), nt,
                                preferred_element_type=F32)
        zeros = jnp.zeros((2 * c_sz, 2 * c_sz), F32)
        att = jnp.concatenate([cat([a_r0, a_r1, a_off]), cat([zeros, a_r2, a_r3])], axis=1)
        att = jnp.where(causal, att, 0.0).astype(BF16)
        o = o + jnp.dot(att, v, preferred_element_type=F32)

        k_fin = cat([k_end[0] * d123, k_end[1] * d23, k_end[2] * e3, k_end[3]]).astype(BF16)
        kv_t = lax.dot_general(v, k_fin, tn, preferred_element_type=F32)
        state_new = st * d_all + kv_t

        ms = jnp.mean(o * o, axis=-1, keepdims=True)
        on = (o * lax.rsqrt(ms + RMS_EPS)) * gn
        r = r_ref[rows, :].astype(F32)
        gate = r / (1.0 + jnp.exp(-r))
        o_ref[rows, :] = (on * gate).astype(o_ref.dtype)
        state_ref[...] = state_new
        return carry

    lax.fori_loop(0, GLA_T // g_sz, group, 0, unroll=True)


def _gla(qkv, gr, b_cum, gn, batch, seq):
    t = GLA_T
    nt_ = seq // t
    m = batch * seq
    kb = (GLA_HEADS * GLA_DK) // GLA_DK
    vb = (2 * GLA_HEADS * GLA_DK) // GLA_DV
    return pl.pallas_call(
        _gla_kernel,
        grid=(batch, GLA_HEADS, nt_),
        in_specs=[
            pl.BlockSpec((t, GLA_DK), lambda b, h, s: (b * nt_ + s, h)),
            pl.BlockSpec((t, GLA_DK), lambda b, h, s: (b * nt_ + s, kb + h)),
            pl.BlockSpec((t, GLA_DV), lambda b, h, s: (b * nt_ + s, vb + h)),
            pl.BlockSpec((t, GLA_DV), lambda b, h, s: (b * nt_ + s, h)),
            pl.BlockSpec((t, GLA_DK), lambda b, h, s: (b * nt_ + s, h)),
            pl.BlockSpec((1, GLA_DV), lambda b, h, s: (0, 0)),
        ],
        out_specs=pl.BlockSpec((t, GLA_DV), lambda b, h, s: (b * nt_ + s, h)),
        out_shape=jax.ShapeDtypeStruct((m, GLA_HEADS * GLA_DV), BF16),
        scratch_shapes=[pltpu.VMEM((GLA_DV, GLA_DK), F32)],
        compiler_params=_cparams(("parallel", "parallel", "arbitrary")),
    )(qkv, qkv, qkv, gr, b_cum, gn)


def _t5_bucket_np(dist):
    max_exact = REL_BUCKETS // 2
    d_f = np.maximum(dist, 1).astype(np.float32)
    large = max_exact + (np.log(d_f / np.float32(max_exact))
                         / np.float32(math.log(REL_MAX_DIST / max_exact))
                         * np.float32(REL_BUCKETS - max_exact)).astype(np.int32)
    large = np.minimum(large, REL_BUCKETS - 1)
    return np.where(dist < max_exact, dist, large)


def _dil_bucket_index():
    steps = DIL_STEPS
    qi = np.arange(steps)[:, None]
    kj = np.arange(2 * steps)[None, :]
    rel = qi + steps - kj
    band = (rel >= 0) & (rel <= steps)
    out = []
    for _, dil in DIL_CONFIGS:
        bucket = _t5_bucket_np(np.clip(rel, 0, steps) * dil)
        out.append(np.where(band, bucket, REL_BUCKETS))
    return np.stack(out).astype(np.int32)


def _dil_bias_kernel(relb_ref, idx_ref, o_ref):
    head = pl.program_id(1)
    idx = idx_ref[0]
    bias = jnp.full(idx.shape, NEG_INF, F32)
    for bkt in range(REL_BUCKETS):
        bias = jnp.where(idx == bkt, relb_ref[bkt, head] * LOG2E, bias)
    o_ref[0, 0] = bias


def _dil_bias(rel_bias):
    steps = DIL_STEPS
    ncfg = len(DIL_CONFIGS)
    idx = jnp.asarray(_dil_bucket_index())
    return pl.pallas_call(
        _dil_bias_kernel,
        grid=(ncfg, DIL_HEADS),
        in_specs=[pl.BlockSpec(memory_space=pltpu.SMEM),
                  pl.BlockSpec((1, steps, 2 * steps), lambda c, h: (c, 0, 0))],
        out_specs=pl.BlockSpec((1, 1, steps, 2 * steps), lambda c, h: (c, h, 0, 0)),
        out_shape=jax.ShapeDtypeStruct((ncfg, DIL_HEADS, steps, 2 * steps), F32),
        compiler_params=_cparams(("parallel", "parallel")),
    )(rel_bias, idx)


DIL_GROUP = 8


def _dil_kernel(bias_ref, q_ref, k_ref, v_ref, o_ref, m_sc, l_sc, acc_sc, *, seq):
    steps = DIL_STEPS
    e = DIL_HEAD_DIM
    scale = e ** -0.5 * LOG2E
    nt = (((1,), (1,)), ((), ()))
    n_cfg = len(DIL_CONFIGS)

    def rows_of(start, size, dil):
        if dil == 1:
            return pl.ds(start, size)
        return pl.ds(start, size, stride=dil)

    def attend(c, dil, blocks, merge):
        q_rows = [rows_of(q_start, steps, dil) for q_start, _, _, _ in blocks]
        k_rows = [rows_of(k_start, n_keys, dil) for _, k_start, n_keys, _ in blocks]
        logits = []
        for qr, kr, (_, _, _, bias) in zip(q_rows, k_rows, blocks):
            q = q_ref[0, qr, :].astype(BF16)
            k = k_ref[0, kr, :].astype(BF16)
            logits.append(lax.dot_general(q, k, nt, preferred_element_type=F32) * scale + bias)
        stats = []
        for s in logits:
            m = jnp.max(s, axis=-1, keepdims=True)
            p = jnp.exp2(s - m)
            stats.append((m, jnp.sum(p, axis=-1, keepdims=True), p.astype(BF16)))
        pvs = [jnp.dot(p, v_ref[0, kr, :].astype(BF16), preferred_element_type=F32)
               for (_, _, p), kr in zip(stats, k_rows)]
        if not merge:
            for qr, (m, l, _), pv in zip(q_rows, stats, pvs):
                m_sc[c - 1, qr, :] = jnp.broadcast_to(m, (steps, e))
                l_sc[c - 1, qr, :] = jnp.broadcast_to(l, (steps, e))
                acc_sc[c - 1, qr, :] = pv
            return
        outs = []
        for qr, (m, l, _), pv in zip(q_rows, stats, pvs):
            ms = [m] + [m_sc[i, qr, :] for i in range(n_cfg - 1)]
            ls = [l] + [l_sc[i, qr, :] for i in range(n_cfg - 1)]
            accs = [pv] + [acc_sc[i, qr, :] for i in range(n_cfg - 1)]
            m_max = functools.reduce(jnp.maximum, ms)
            num = None
            den = None
            for m_i, l_i, acc_i in zip(ms, ls, accs):
                wgt = jnp.exp2(m_i - m_max)
                num = wgt * acc_i if num is None else num + wgt * acc_i
                den = wgt * l_i if den is None else den + wgt * l_i
            outs.append((qr, (num / den).astype(o_ref.dtype)))
        for qr, o in outs:
            o_ref[0, qr, :] = o

    for c in reversed(range(n_cfg)):
        dil = DIL_CONFIGS[c][1]
        nb = seq // dil // steps
        span = steps * dil

        def first(r, c=c):
            return (r, r, steps, bias_ref[c, 0, :, steps:2 * steps])

        def later(j, c=c, dil=dil, nb=nb, span=span):
            if dil == 1:
                q_start = (1 + j) * span
                if not isinstance(q_start, int):
                    q_start = pl.multiple_of(q_start, span)
                return (q_start, q_start - span, 2 * steps, bias_ref[c, 0])
            r = j // (nb - 1)
            n = 1 + j % (nb - 1)
            return (n * span + r, (n - 1) * span + r, 2 * steps, bias_ref[c, 0])

        for make, count in ((first, dil), (later, dil * (nb - 1))):
            full, rest = divmod(count, DIL_GROUP)

            def group(gi, carry, c=c, dil=dil, make=make):
                attend(c, dil, [make(gi * DIL_GROUP + u) for u in range(DIL_GROUP)], c == 0)
                return carry

            if full:
                lax.fori_loop(0, full, group, 0)
            if rest:
                attend(c, dil, [make(full * DIL_GROUP + u) for u in range(rest)], c == 0)


def _dilated(qkv3, bias, batch, seq):
    e = DIL_HEAD_DIM
    steps = DIL_STEPS
    ncfg = len(DIL_CONFIGS)
    assert DIL_CONFIGS[0][1] == 1, "the config that writes the output rows must be undilated"
    return pl.pallas_call(
        functools.partial(_dil_kernel, seq=seq),
        grid=(batch, DIL_HEADS),
        in_specs=[
            pl.BlockSpec((ncfg, 1, steps, 2 * steps), lambda b, h: (0, h, 0, 0)),
            pl.BlockSpec((1, seq, e), lambda b, h: (b, 0, h)),
            pl.BlockSpec((1, seq, e), lambda b, h: (b, 0, DIL_HEADS + h)),
            pl.BlockSpec((1, seq, e), lambda b, h: (b, 0, 2 * DIL_HEADS + h)),
        ],
        out_specs=pl.BlockSpec((1, seq, e), lambda b, h: (b, 0, h)),
        out_shape=jax.ShapeDtypeStruct((batch, seq, DIL_HEADS * e), BF16),
        scratch_shapes=[pltpu.VMEM((ncfg - 1, seq, e), F32),
                        pltpu.VMEM((ncfg - 1, seq, e), F32),
                        pltpu.VMEM((ncfg - 1, seq, e), F32)],
        compiler_params=_cparams(("parallel", "parallel")),
    )(bias, qkv3, qkv3, qkv3)


def _mix_out_kernel(a_ref, b_ref, wa_ref, wb_ref, x_ref, o_ref):
    acc = jnp.dot(a_ref[...], wa_ref[...], preferred_element_type=F32)
    acc = acc + jnp.dot(b_ref[...], wb_ref[...], preferred_element_type=F32)
    o_ref[...] = x_ref[...] + acc


def _mix_out(o_gla, o_dil, w, x, bm=1024, bn=1024):
    m, ka = o_gla.shape
    kb = o_dil.shape[1]
    n = w.shape[1]
    assert ka == kb and w.shape[0] == ka + kb
    return pl.pallas_call(
        _mix_out_kernel,
        grid=(m // bm, n // bn),
        in_specs=[pl.BlockSpec((bm, ka), lambda i, j: (i, 0)),
                  pl.BlockSpec((bm, kb), lambda i, j: (i, 0)),
                  pl.BlockSpec((ka, bn), lambda i, j: (0, j)),
                  pl.BlockSpec((kb, bn), lambda i, j: (1, j)),
                  pl.BlockSpec((bm, bn), lambda i, j: (i, j))],
        out_specs=pl.BlockSpec((bm, bn), lambda i, j: (i, j)),
        out_shape=jax.ShapeDtypeStruct((m, n), F32),
        compiler_params=_cparams(("parallel", "parallel")),
    )(o_gla, o_dil, w, w, x)


XATTN_SLAB = 128


def _xattn_kernel(h_ref, gx_ref, wq_ref, k_ref, v_ref, wo_ref, gf_ref, h2_ref, hn_ref):
    e = XATTN_HEAD_DIM
    nt = (((1,), (1,)), ((), ()))
    gx = gx_ref[...]
    gf = gf_ref[...]
    slabs = [slice(r0, r0 + XATTN_SLAB) for r0 in range(0, h_ref.shape[0], XATTN_SLAB)]
    hs = [h_ref[rows, :] for rows in slabs]
    hns = []
    for h in hs:
        ms = jnp.mean(h * h, axis=-1, keepdims=True)
        hns.append(((h * lax.rsqrt(ms + RMS_EPS)) * gx).astype(BF16))
    qs = [jnp.dot(hn, wq_ref[...], preferred_element_type=F32).astype(BF16) for hn in hns]
    os_ = []
    for q in qs:
        outs = []
        for hh in range(XATTN_HEADS):
            qh = q[:, hh * e:(hh + 1) * e]
            kh = k_ref[0, :, hh * e:(hh + 1) * e]
            vh = v_ref[0, :, hh * e:(hh + 1) * e]
            s = lax.dot_general(qh, kh, nt, preferred_element_type=F32) * (e ** -0.5)
            m = jnp.max(s, axis=-1, keepdims=True)
            p = jnp.exp(s - m)
            p = p / jnp.sum(p, axis=-1, keepdims=True)
            outs.append(jnp.dot(p.astype(BF16), vh, preferred_element_type=F32))
        os_.append(jnp.concatenate(outs, axis=1).astype(BF16))
    h2s = [h + jnp.dot(o, wo_ref[...], preferred_element_type=F32) for h, o in zip(hs, os_)]
    hn2s = []
    for h2 in h2s:
        ms2 = jnp.mean(h2 * h2, axis=-1, keepdims=True)
        hn2s.append(((h2 * lax.rsqrt(ms2 + RMS_EPS)) * gf).astype(hn_ref.dtype))
    for rows, h2, hn2 in zip(slabs, h2s, hn2s):
        h2_ref[rows, :] = h2
        hn_ref[rows, :] = hn2


def _xattn(h1, gx, wq, kx, vx, wo, gf, seq, tm=256):
    m, d = h1.shape
    mem_len = kx.shape[1]
    per_seq = seq // tm
    return pl.pallas_call(
        _xattn_kernel,
        grid=(m // tm,),
        in_specs=[pl.BlockSpec((tm, d), lambda i: (i, 0)),
                  pl.BlockSpec((1, d), lambda i: (0, 0)),
                  pl.BlockSpec((d, XATTN_WIDTH), lambda i: (0, 0)),
                  pl.BlockSpec((1, mem_len, XATTN_WIDTH), lambda i: (i // per_seq, 0, 0)),
                  pl.BlockSpec((1, mem_len, XATTN_WIDTH), lambda i: (i // per_seq, 0, 0)),
                  pl.BlockSpec((XATTN_WIDTH, d), lambda i: (0, 0)),
                  pl.BlockSpec((1, d), lambda i: (0, 0))],
        out_specs=[pl.BlockSpec((tm, d), lambda i: (i, 0)),
                   pl.BlockSpec((tm, d), lambda i: (i, 0))],
        out_shape=[jax.ShapeDtypeStruct((m, d), F32),
                   jax.ShapeDtypeStruct((m, d), BF16)],
        compiler_params=_cparams(("parallel",)),
    )(h1, gx.reshape(1, d), wq, kx, vx, wo, gf.reshape(1, d))


def _ffn_in_kernel(x_ref, wg_ref, wu_ref, halo_ref, cw_ref, cb_ref, a_ref):
    x = x_ref[...]
    g = jnp.dot(x, wg_ref[...], preferred_element_type=F32)
    u = jnp.dot(x, wu_ref[...], preferred_element_type=F32)
    halo = halo_ref[0]
    prev1 = halo[SUBLANES - 1:SUBLANES, :]
    prev2 = halo[SUBLANES - 2:SUBLANES - 1, :]
    row = lax.broadcasted_iota(jnp.int32, (SUBLANES, g.shape[1]), 0)
    r1 = pltpu.roll(g, 1, axis=0)
    r2 = pltpu.roll(g, 2, axis=0)
    head1 = jnp.where(row == 0, prev1, r1[:SUBLANES])
    head2 = jnp.where(row == 0, prev2, jnp.where(row == 1, prev1, r2[:SUBLANES]))
    g_m1 = jnp.concatenate([head1, r1[SUBLANES:]], axis=0)
    g_m2 = jnp.concatenate([head2, r2[SUBLANES:]], axis=0)
    cw = cw_ref[...]
    y = cb_ref[...] + g_m2 * cw[0:1, :]
    y = y + g_m1 * cw[1:2, :]
    y = y + g * cw[2:3, :]
    a_ref[...] = ((y / (1.0 + jnp.exp(-y))) * u).astype(a_ref.dtype)


def _ffn_in(x, wg, wu, halo_g, cw, cb, bm=FFN_BM, bn=512):
    m, k = x.shape
    n = wg.shape[1]
    return pl.pallas_call(
        _ffn_in_kernel,
        grid=(m // bm, n // bn),
        in_specs=[pl.BlockSpec((bm, k), lambda i, j: (i, 0)),
                  pl.BlockSpec((k, bn), lambda i, j: (0, j)),
                  pl.BlockSpec((k, bn), lambda i, j: (0, j)),
                  pl.BlockSpec((1, SUBLANES, bn), lambda i, j: (i, 0, j)),
                  pl.BlockSpec((CONV_WIDTH, bn), lambda i, j: (0, j)),
                  pl.BlockSpec((1, bn), lambda i, j: (0, j))],
        out_specs=pl.BlockSpec((bm, bn), lambda i, j: (i, j)),
        out_shape=jax.ShapeDtypeStruct((m, n), BF16),
        compiler_params=_cparams(("parallel", "parallel")),
    )(x, wg, wu, halo_g, cw, cb)


def _ffn_halo_rows(hn, batch, seq, bm=FFN_BM):
    d = hn.shape[1]
    tiles = seq // bm
    tail = hn.reshape(batch, tiles, bm, d)[:, :, bm - SUBLANES:, :]
    prev = jnp.concatenate([jnp.zeros_like(tail[:, :1]), tail[:, :-1]], axis=1)
    return prev.reshape(batch * tiles * SUBLANES, d)


FFN_OUT_COLS = 1024
FFN_RES_COLS = 256
FFN_NORM_ROWS = 256


def _ffn_out_kernel(a_ref, wd_ref, h_ref, fg_ref, o_ref, *, n_res):
    kk = pl.program_id(1)
    d = o_ref.shape[1]

    @pl.when(kk == 0)
    def _():
        o_ref[...] = jnp.zeros_like(o_ref)

    a = a_ref[...]
    for c0 in range(0, d, FFN_OUT_COLS):
        cols = slice(c0, c0 + FFN_OUT_COLS)
        o_ref[:, cols] += jnp.dot(a, wd_ref[:, cols], preferred_element_type=F32)

    for c in range(n_res):
        @pl.when(kk == c)
        def _(c=c):
            cols = slice(c * FFN_RES_COLS, (c + 1) * FFN_RES_COLS)
            o_ref[:, cols] += h_ref[...]

    @pl.when(kk == pl.num_programs(1) - 1)
    def _():
        fg = fg_ref[...]
        for r0 in range(0, o_ref.shape[0], FFN_NORM_ROWS):
            rows = slice(r0, r0 + FFN_NORM_ROWS)
            h3 = o_ref[rows, :]
            ms = jnp.mean(h3 * h3, axis=-1, keepdims=True)
            o_ref[rows, :] = (h3 * lax.rsqrt(ms + RMS_EPS)) * fg


def _ffn_out(a, wd, h2, fg, tm=1024, tk=512):
    m, ff = a.shape
    d = wd.shape[1]
    n_res = d // FFN_RES_COLS
    assert ff // tk >= n_res, "one residual slab per contraction step"
    return pl.pallas_call(
        functools.partial(_ffn_out_kernel, n_res=n_res),
        grid=(m // tm, ff // tk),
        in_specs=[pl.BlockSpec((tm, tk), lambda i, k: (i, k)),
                  pl.BlockSpec((tk, d), lambda i, k: (k, 0)),
                  pl.BlockSpec((tm, FFN_RES_COLS), lambda i, k: (i, jnp.minimum(k, n_res - 1))),
                  pl.BlockSpec((1, d), lambda i, k: (0, 0))],
        out_specs=pl.BlockSpec((tm, d), lambda i, k: (i, 0)),
        out_shape=jax.ShapeDtypeStruct((m, d), F32),
        compiler_params=_cparams(("parallel", "arbitrary")),
    )(a, wd, h2, fg.reshape(1, d))


def _layer(x2, mem2, rel_bias, batch, seq, mem_len, norm_mix_g, w_in, gla_w_gate2,
           gla_b_gate, gla_norm_g, w_out, norm_xattn_g, mem_norm_g, w_xq, w_xk, w_xv,
           w_xo, norm_ffn_g, w_ffn_gate, w_ffn_up, ffn_conv_w, ffn_conv_b, w_ffn_down,
           out_g):
    w_in_t = w_in.T.astype(BF16)
    w_glr = jnp.pad(w_in_t[_W_GLR:_W_GR], ((0, LANES - GLA_LOWRANK), (0, 0)))
    w2_pad = jnp.pad(gla_w_gate2, ((0, LANES - GLA_LOWRANK), (0, 0)))
    ff_pad = D_FF_PAD - D_FF
    wg = _cast_pad_cols(w_ffn_gate, D_FF_PAD)
    wu = _cast_pad_cols(w_ffn_up, D_FF_PAD)
    wd = _cast_pad_rows(w_ffn_down, D_FF_PAD)
    cw = jnp.pad(ffn_conv_w, ((0, 0), (0, ff_pad)))
    cb = jnp.pad(ffn_conv_b, ((0, ff_pad),)).reshape(1, D_FF_PAD)
    w_out_b = w_out.astype(BF16)
    w_kv = jnp.concatenate([w_xk, w_xv], axis=1).astype(BF16)

    hn = _rmsnorm(x2, norm_mix_g)
    d_qkv = _matmul_nt(hn, w_in_t, 1024, 1024, F32, row0=_W_DQKV, n=_W_END - _W_DQKV)
    g_qkv = _matmul_nt(hn, w_in_t, 1024, 1024, BF16, row0=_W_GQKV, n=_W_GLR - _W_GQKV)
    g_r = _matmul_nt(hn, w_in_t, 1024, 1024, BF16, row0=_W_GR, n=_W_DQKV - _W_GR)
    glr = _matmul_nt(hn, w_glr, 1024, LANES, F32)
    b_cum = _gla_gate(glr, w2_pad, gla_b_gate.reshape(1, -1))
    o_gla = _gla(g_qkv, g_r, b_cum, gla_norm_g.reshape(1, -1), batch, seq)
    o_dil = _dilated(d_qkv.reshape(batch, seq, 3 * DIL_HEADS * DIL_HEAD_DIM),
                     _dil_bias(rel_bias), batch, seq)
    o_dil = o_dil.reshape(batch * seq, DIL_HEADS * DIL_HEAD_DIM)
    h1 = _mix_out(o_gla, o_dil, w_out_b, x2)

    memn = _rmsnorm(mem2, mem_norm_g)
    kv = _matmul(memn, w_kv, 512, 512, BF16)
    kx = kv[:, :XATTN_WIDTH].reshape(batch, mem_len, XATTN_WIDTH)
    vx = kv[:, XATTN_WIDTH:].reshape(batch, mem_len, XATTN_WIDTH)
    h2, hn3 = _xattn(h1, norm_xattn_g, w_xq.astype(BF16), kx, vx, w_xo.astype(BF16),
                     norm_ffn_g, seq)

    halo_x = _ffn_halo_rows(hn3, batch, seq)
    halo_g = _matmul(halo_x, wg, halo_x.shape[0], 512, F32)
    halo_g = halo_g.reshape(-1, SUBLANES, D_FF_PAD)
    act = _ffn_in(hn3, wg, wu, halo_g, cw, cb)
    return _ffn_out(act, wd, h2, out_g)


def kernel(x, mem, rel_bias, norm_mix_g, w_in, gla_w_gate2, gla_b_gate, gla_norm_g, w_out,
           norm_xattn_g, mem_norm_g, w_xq, w_xk, w_xv, w_xo, norm_ffn_g, w_ffn_gate,
           w_ffn_up, ffn_conv_w, ffn_conv_b, w_ffn_down, final_norm_g):
    batch, seq, d = x.shape
    mem_len = mem.shape[1]
    depth = w_in.shape[0]
    assert depth == 1, "the fused final rmsnorm assumes a single layer"
    out = _layer(x.reshape(batch * seq, d), mem.reshape(batch * mem_len, d), rel_bias,
                 batch, seq, mem_len, norm_mix_g[0], w_in[0], gla_w_gate2[0],
                 gla_b_gate[0], gla_norm_g[0], w_out[0], norm_xattn_g[0], mem_norm_g[0],
                 w_xq[0], w_xk[0], w_xv[0], w_xo[0], norm_ffn_g[0], w_ffn_gate[0],
                 w_ffn_up[0], ffn_conv_w[0], ffn_conv_b[0], w_ffn_down[0], final_norm_g)
    return out.reshape(batch, seq, d)
```

```python
import functools
import math

import numpy as np
import jax
import jax.numpy as jnp
from jax import lax
from jax.experimental import pallas as pl
from jax.experimental.pallas import tpu as pltpu

F32 = jnp.float32
BF16 = jnp.bfloat16

D_MODEL = 4096
RMS_EPS = 1e-6
GLA_HEADS = 4
GLA_DV = 512
GLA_DK = 256
GLA_LOWRANK = 16
GLA_TAU = 16.0
GLA_CHUNK = 64
DIL_HEAD_DIM = 128
DIL_HEADS = 16
DIL_CONFIGS = ((128, 1), (512, 4), (2048, 16))
DIL_STEPS = 128
REL_BUCKETS = 32
REL_MAX_DIST = 2048
XATTN_HEADS = 4
XATTN_HEAD_DIM = 128
XATTN_WIDTH = 512
D_FF = 11008
CONV_WIDTH = 3
NEG_INF = -1e30
LOG2E = math.log2(math.e)

LANES = 128
SUBLANES = 8
ROW_ALIGN = 16
VMEM_LIMIT = 56 * 1024 * 1024

D_FF_PAD = 11264
FFN_BM = 1024

_W_GQKV = 0
_W_GLR = 4096
_W_GR = 4112
_W_DQKV = 6160
_W_END = 12304


def _cparams(sem):
    return pltpu.CompilerParams(dimension_semantics=sem, vmem_limit_bytes=VMEM_LIMIT)


def _rmsnorm_kernel(x_ref, g_ref, o_ref):
    x = x_ref[...]
    ms = jnp.mean(x * x, axis=-1, keepdims=True)
    o_ref[...] = ((x * lax.rsqrt(ms + RMS_EPS)) * g_ref[...]).astype(o_ref.dtype)


def _rmsnorm(x, g, tm=256):
    m, d = x.shape
    return pl.pallas_call(
        _rmsnorm_kernel,
        grid=(m // tm,),
        in_specs=[pl.BlockSpec((tm, d), lambda i: (i, 0)),
                  pl.BlockSpec((1, d), lambda i: (0, 0))],
        out_specs=pl.BlockSpec((tm, d), lambda i: (i, 0)),
        out_shape=jax.ShapeDtypeStruct((m, d), BF16),
        compiler_params=_cparams(("parallel",)),
    )(x, g.reshape(1, d))


def _cast_pad_cols_kernel(w_ref, o_ref):
    n = w_ref.shape[1]
    o_ref[:, :n] = w_ref[...].astype(o_ref.dtype)
    o_ref[:, n:] = jnp.zeros((o_ref.shape[0], o_ref.shape[1] - n), o_ref.dtype)


def _cast_pad_cols(w, n_pad, tm=256):
    k, n = w.shape
    return pl.pallas_call(
        _cast_pad_cols_kernel,
        grid=(k // tm,),
        in_specs=[pl.BlockSpec((tm, n), lambda i: (i, 0))],
        out_specs=pl.BlockSpec((tm, n_pad), lambda i: (i, 0)),
        out_shape=jax.ShapeDtypeStruct((k, n_pad), BF16),
        compiler_params=_cparams(("parallel",)),
    )(w)


def _cast_pad_rows_kernel(w_ref, o_ref, *, n_valid):
    valid = pl.program_id(0) < n_valid
    o_ref[...] = jnp.where(valid, w_ref[...], 0.0).astype(o_ref.dtype)


def _cast_pad_rows(w, k_pad, tm=256):
    k, n = w.shape
    n_valid = k // tm
    return pl.pallas_call(
        functools.partial(_cast_pad_rows_kernel, n_valid=n_valid),
        grid=(k_pad // tm,),
        in_specs=[pl.BlockSpec((tm, n), lambda i: (jnp.minimum(i, n_valid - 1), 0))],
        out_specs=pl.BlockSpec((tm, n), lambda i: (i, 0)),
        out_shape=jax.ShapeDtypeStruct((k_pad, n), BF16),
        compiler_params=_cparams(("parallel",)),
    )(w)


def _mm_nt_kernel(x_ref, wt_ref, o_ref):
    o_ref[...] = lax.dot_general(x_ref[...], wt_ref[...], (((1,), (1,)), ((), ())),
                                 preferred_element_type=F32).astype(o_ref.dtype)


def _matmul_nt(x, wt, bm, bn, out_dtype, row0=0, n=None):
    m, k = x.shape
    n = wt.shape[0] if n is None else n
    return pl.pallas_call(
        _mm_nt_kernel,
        grid=(m // bm, n // bn),
        in_specs=[pl.BlockSpec((bm, k), lambda i, j: (i, 0)),
                  pl.BlockSpec((pl.Element(bn), pl.Element(k)),
                               lambda i, j: (pl.multiple_of(row0 + j * bn, ROW_ALIGN), 0))],
        out_specs=pl.BlockSpec((bm, bn), lambda i, j: (i, j)),
        out_shape=jax.ShapeDtypeStruct((m, n), out_dtype),
        compiler_params=_cparams(("parallel", "parallel")),
    )(x, wt)


def _mm_kernel(x_ref, w_ref, o_ref):
    o_ref[...] = jnp.dot(x_ref[...], w_ref[...],
                         preferred_element_type=F32).astype(o_ref.dtype)


def _matmul(x, w, bm, bn, out_dtype):
    m, k = x.shape
    n = w.shape[1]
    return pl.pallas_call(
        _mm_kernel,
        grid=(m // bm, n // bn),
        in_specs=[pl.BlockSpec((bm, k), lambda i, j: (i, 0)),
                  pl.BlockSpec((k, bn), lambda i, j: (0, j))],
        out_specs=pl.BlockSpec((bm, bn), lambda i, j: (i, j)),
        out_shape=jax.ShapeDtypeStruct((m, n), out_dtype),
        compiler_params=_cparams(("parallel", "parallel")),
    )(x, w)


GLA_T = 512
GLA_GROUP = 4
GLA_CUMSUM_ROWS = 256


def _split_bf16(x):
    hi = x.astype(BF16)
    lo = (x - hi.astype(F32)).astype(BF16)
    return hi, lo


def _gla_gate_kernel(glr_ref, w2_ref, b2_ref, tri_ref, b_ref):
    x_hi, x_lo = _split_bf16(glr_ref[...])
    w_hi, w_lo = _split_bf16(w2_ref[...])
    z = (jnp.dot(x_hi, w_hi, preferred_element_type=F32)
         + jnp.dot(x_lo, w_hi, preferred_element_type=F32)
         + jnp.dot(x_hi, w_lo, preferred_element_type=F32)) + b2_ref[...]
    log_sig = jnp.minimum(z, 0.0) - jnp.log(1.0 + jnp.exp(-jnp.abs(z)))
    g_hi, g_lo = _split_bf16(log_sig / GLA_TAU)
    tri = tri_ref[...]
    t_sub = tri.shape[0]
    for r0 in range(0, g_hi.shape[0], t_sub):
        rows = slice(r0, r0 + t_sub)
        b_ref[rows, :] = (jnp.dot(tri, g_hi[rows], preferred_element_type=F32)
                          + jnp.dot(tri, g_lo[rows], preferred_element_type=F32))


def _chunk_tril(t):
    idx = np.arange(t)
    same = (idx[:, None] // GLA_CHUNK) == (idx[None, :] // GLA_CHUNK)
    return jnp.asarray((same & (idx[:, None] >= idx[None, :])).astype(np.float32), BF16)


def _gla_gate(glr, w2_pad, b2, t=GLA_T):
    m = glr.shape[0]
    n = w2_pad.shape[1]
    return pl.pallas_call(
        _gla_gate_kernel,
        grid=(m // t,),
        in_specs=[pl.BlockSpec((t, LANES), lambda i: (i, 0)),
                  pl.BlockSpec((LANES, n), lambda i: (0, 0)),
                  pl.BlockSpec((1, n), lambda i: (0, 0)),
                  pl.BlockSpec((GLA_CUMSUM_ROWS, GLA_CUMSUM_ROWS), lambda i: (0, 0))],
        out_specs=pl.BlockSpec((t, n), lambda i: (i, 0)),
        out_shape=jax.ShapeDtypeStruct((m, n), F32),
        compiler_params=_cparams(("parallel",)),
    )(glr, w2_pad, b2, _chunk_tril(GLA_CUMSUM_ROWS))


def _gla_kernel(q_ref, k_ref, v_ref, r_ref, b_ref, gn_ref, o_ref, state_ref):
    c_sz = GLA_CHUNK
    g_sz = GLA_GROUP * c_sz
    assert GLA_GROUP == 4

    @pl.when(pl.program_id(2) == 0)
    def _():
        state_ref[...] = jnp.zeros_like(state_ref)

    row = lax.broadcasted_iota(jnp.int32, (g_sz, g_sz), 0)
    col = lax.broadcasted_iota(jnp.int32, (g_sz, g_sz), 1)
    causal = row >= col
    gn = gn_ref[...]
    nt = (((1,), (1,)), ((), ()))
    tn = (((0,), (0,)), ((), ()))

    def cat(parts):
        return jnp.concatenate(parts, axis=0)

    def group(gi, carry):
        rows = pl.ds(pl.multiple_of(gi * g_sz, g_sz), g_sz)
        b_all = b_ref[rows, :]
        q_all = q_ref[rows, :].astype(F32) * (GLA_DK ** -0.5)
        k_all = k_ref[rows, :].astype(F32)
        v = v_ref[rows, :]
        st = state_ref[...]

        sl = [slice(c * c_sz, (c + 1) * c_sz) for c in range(GLA_GROUP)]
        b = [b_all[s] for s in sl]
        bl = [x[c_sz - 1:c_sz, :] for x in b]
        bm = [x[c_sz // 2:c_sz // 2 + 1, :] for x in b]
        q_start = [q_all[s] * jnp.exp(x) for s, x in zip(sl, b)]
        k_end = [k_all[s] * jnp.exp(t - x) for s, x, t in zip(sl, b, bl)]
        q_mid = [(q_all[s] * jnp.exp(x - m)).astype(BF16) for s, x, m in zip(sl, b, bm)]
        k_mid = [(k_all[s] * jnp.exp(m - x)).astype(BF16) for s, x, m in zip(sl, b, bm)]

        e1, e2, e3 = jnp.exp(bl[1]), jnp.exp(bl[2]), jnp.exp(bl[3])
        e0 = jnp.exp(bl[0])
        d01, d12, d23 = e0 * e1, e1 * e2, e2 * e3
        d012, d123 = d01 * e2, d12 * e3
        d_all = d012 * e3

        qs = cat([q_start[0], q_start[1] * e0, q_start[2] * d01, q_start[3] * d012]).astype(BF16)
        o = lax.dot_general(qs, st.astype(BF16), nt, preferred_element_type=F32)

        k_end_b = [x.astype(BF16) for x in k_end]
        a_r0 = lax.dot_general(q_mid[0], cat([k_mid[0], k_mid[1]]), nt,
                               preferred_element_type=F32)
        a_r1 = lax.dot_general(q_mid[1], cat([(k_end[0] * jnp.exp(bm[1])).astype(BF16),
                                              k_mid[1]]), nt, preferred_element_type=F32)
        a_r2 = lax.dot_general(q_mid[2], cat([k_mid[2], k_mid[3]]), nt,
                               preferred_element_type=F32)
        a_r3 = lax.dot_general(q_mid[3], cat([(k_end[2] * jnp.exp(bm[3])).astype(BF16),
                                              k_mid[3]]), nt, preferred_element_type=F32)
        a_off = lax.dot_general(cat([q_start[2], q_start[3] * e2]).astype(BF16),
                                cat([(k_end[0] * e1).astype(BF16), k_end_b[1]]), nt,
                                preferred_element_type=F32)
        zeros = jnp.zeros((2 * c_sz, 2 * c_sz), F32)
        att = jnp.concatenate([cat([a_r0, a_r1, a_off]), cat([zeros, a_r2, a_r3])], axis=1)
        att = jnp.where(causal, att, 0.0).astype(BF16)
        o = o + jnp.dot(att, v, preferred_element_type=F32)

        k_fin = cat([k_end[0] * d123, k_end[1] * d23, k_end[2] * e3, k_end[3]]).astype(BF16)
        kv_t = lax.dot_general(v, k_fin, tn, preferred_element_type=F32)
        state_new = st * d_all + kv_t

        ms = jnp.mean(o * o, axis=-1, keepdims=True)
        on = (o * lax.rsqrt(ms + RMS_EPS)) * gn
        r = r_ref[rows, :].astype(F32)
        gate = r / (1.0 + jnp.exp(-r))
        o_ref[rows, :] = (on * gate).astype(o_ref.dtype)
        state_ref[...] = state_new
        return carry

    lax.fori_loop(0, GLA_T // g_sz, group, 0, unroll=True)


def _gla(qkv, gr, b_cum, gn, batch, seq):
    t = GLA_T
    nt_ = seq // t
    m = batch * seq
    kb = (GLA_HEADS * GLA_DK) // GLA_DK
    vb = (2 * GLA_HEADS * GLA_DK) // GLA_DV
    return pl.pallas_call(
        _gla_kernel,
        grid=(batch, GLA_HEADS, nt_),
        in_specs=[
            pl.BlockSpec((t, GLA_DK), lambda b, h, s: (b * nt_ + s, h)),
            pl.BlockSpec((t, GLA_DK), lambda b, h, s: (b * nt_ + s, kb + h)),
            pl.BlockSpec((t, GLA_DV), lambda b, h, s: (b * nt_ + s, vb + h)),
            pl.BlockSpec((t, GLA_DV), lambda b, h, s: (b * nt_ + s, h)),
            pl.BlockSpec((t, GLA_DK), lambda b, h, s: (b * nt_ + s, h)),
            pl.BlockSpec((1, GLA_DV), lambda b, h, s: (0, 0)),
        ],
        out_specs=pl.BlockSpec((t, GLA_DV), lambda b, h, s: (b * nt_ + s, h)),
        out_shape=jax.ShapeDtypeStruct((m, GLA_HEADS * GLA_DV), BF16),
        scratch_shapes=[pltpu.VMEM((GLA_DV, GLA_DK), F32)],
        compiler_params=_cparams(("parallel", "parallel", "arbitrary")),
    )(qkv, qkv, qkv, gr, b_cum, gn)


def _t5_bucket_np(dist):
    max_exact = REL_BUCKETS // 2
    d_f = np.maximum(dist, 1).astype(np.float32)
    large = max_exact + (np.log(d_f / np.float32(max_exact))
                         / np.float32(math.log(REL_MAX_DIST / max_exact))
                         * np.float32(REL_BUCKETS - max_exact)).astype(np.int32)
    large = np.minimum(large, REL_BUCKETS - 1)
    return np.where(dist < max_exact, dist, large)


def _dil_bucket_index():
    steps = DIL_STEPS
    qi = np.arange(steps)[:, None]
    kj = np.arange(2 * steps)[None, :]
    rel = qi + steps - kj
    band = (rel >= 0) & (rel <= steps)
    out = []
    for _, dil in DIL_CONFIGS:
        bucket = _t5_bucket_np(np.clip(rel, 0, steps) * dil)
        out.append(np.where(band, bucket, REL_BUCKETS))
    return np.stack(out).astype(np.int32)


def _dil_bias_kernel(relb_ref, idx_ref, o_ref):
    head = pl.program_id(1)
    idx = idx_ref[0]
    bias = jnp.full(idx.shape, NEG_INF, F32)
    for bkt in range(REL_BUCKETS):
        bias = jnp.where(idx == bkt, relb_ref[bkt, head] * LOG2E, bias)
    o_ref[0, 0] = bias


def _dil_bias(rel_bias):
    steps = DIL_STEPS
    ncfg = len(DIL_CONFIGS)
    idx = jnp.asarray(_dil_bucket_index())
    return pl.pallas_call(
        _dil_bias_kernel,
        grid=(ncfg, DIL_HEADS),
        in_specs=[pl.BlockSpec(memory_space=pltpu.SMEM),
                  pl.BlockSpec((1, steps, 2 * steps), lambda c, h: (c, 0, 0))],
        out_specs=pl.BlockSpec((1, 1, steps, 2 * steps), lambda c, h: (c, h, 0, 0)),
        out_shape=jax.ShapeDtypeStruct((ncfg, DIL_HEADS, steps, 2 * steps), F32),
        compiler_params=_cparams(("parallel", "parallel")),
    )(rel_bias, idx)


DIL_GROUP = 8
DIL_SPLIT = 4
DIL_COPY_ROWS = 256


def _dil_kernel(bias_ref, q_ref, k_ref, v_ref, o_ref, m_sc, l_sc, acc_sc, qd, kd, vd, *, seq):
    steps = DIL_STEPS
    e = DIL_HEAD_DIM
    scale = e ** -0.5 * LOG2E
    nt = (((1,), (1,)), ((), ()))
    n_cfg = len(DIL_CONFIGS)
    split = DIL_SPLIT

    def rows_of(start, size, stride):
        if stride == 1:
            return pl.ds(start, size)
        return pl.ds(start, size, stride=stride)

    for src, dst in ((q_ref, qd), (k_ref, kd), (v_ref, vd)):
        for r in range(split):
            for c0 in range(0, seq // split, DIL_COPY_ROWS):
                dst[r, c0:c0 + DIL_COPY_ROWS, :] = src[
                    0, pl.ds(r + split * c0, DIL_COPY_ROWS, stride=split), :]

    def load(nat_ref, split_ref, dil, r, n, span, size):
        if dil % split:
            return nat_ref[0, rows_of(n * span + r, size, dil), :].astype(BF16)
        rows = rows_of(n * (span // split) + r // split, size, dil // split)
        return split_ref[r % split, rows, :].astype(BF16)

    def attend(c, dil, blocks, merge):
        span = steps * dil

        def q_start(r, n_q):
            start = n_q * span + r
            if dil == 1 and not isinstance(start, int):
                start = pl.multiple_of(start, span)
            return start

        q_rows = [rows_of(q_start(r, n_q), steps, dil) for r, n_q, _, _, _ in blocks]
        logits = []
        for r, n_q, n_k, n_keys, bias in blocks:
            q = load(q_ref, qd, dil, r, n_q, span, steps)
            k = load(k_ref, kd, dil, r, n_k, span, n_keys)
            logits.append(lax.dot_general(q, k, nt, preferred_element_type=F32) * scale + bias)
        stats = []
        for s in logits:
            m = jnp.max(s, axis=-1, keepdims=True)
            p = jnp.exp2(s - m)
            stats.append((m, jnp.sum(p, axis=-1, keepdims=True), p.astype(BF16)))
        pvs = [jnp.dot(p, load(v_ref, vd, dil, r, n_k, span, n_keys), preferred_element_type=F32)
               for (_, _, p), (r, _, n_k, n_keys, _) in zip(stats, blocks)]
        if not merge:
            for qr, (m, l, _), pv in zip(q_rows, stats, pvs):
                m_sc[c - 1, qr, :] = jnp.broadcast_to(m, (steps, e))
                l_sc[c - 1, qr, :] = jnp.broadcast_to(l, (steps, e))
                acc_sc[c - 1, qr, :] = pv
            return
        outs = []
        for qr, (m, l, _), pv in zip(q_rows, stats, pvs):
            ms = [m] + [m_sc[i, qr, :] for i in range(n_cfg - 1)]
            ls = [l] + [l_sc[i, qr, :] for i in range(n_cfg - 1)]
            accs = [pv] + [acc_sc[i, qr, :] for i in range(n_cfg - 1)]
            m_max = functools.reduce(jnp.maximum, ms)
            num = None
            den = None
            for m_i, l_i, acc_i in zip(ms, ls, accs):
                wgt = jnp.exp2(m_i - m_max)
                num = wgt * acc_i if num is None else num + wgt * acc_i
                den = wgt * l_i if den is None else den + wgt * l_i
            outs.append((qr, (num / den).astype(o_ref.dtype)))
        for qr, o in outs:
            o_ref[0, qr, :] = o

    for c in reversed(range(n_cfg)):
        dil = DIL_CONFIGS[c][1]
        nb = seq // dil // steps

        def first(r, c=c):
            return (r, 0, 0, steps, bias_ref[c, 0, :, steps:2 * steps])

        def later(j, c=c, dil=dil, nb=nb):
            if dil == 1:
                return (0, 1 + j, j, 2 * steps, bias_ref[c, 0])
            r = j // (nb - 1)
            n = 1 + j % (nb - 1)
            return (r, n, n - 1, 2 * steps, bias_ref[c, 0])

        for make, count in ((first, dil), (later, dil * (nb - 1))):
            full, rest = divmod(count, DIL_GROUP)

            def group(gi, carry, c=c, dil=dil, make=make):
                attend(c, dil, [make(gi * DIL_GROUP + u) for u in range(DIL_GROUP)], c == 0)
                return carry

            if full:
                lax.fori_loop(0, full, group, 0)
            if rest:
                attend(c, dil, [make(full * DIL_GROUP + u) for u in range(rest)], c == 0)


def _dilated(qkv3, bias, batch, seq):
    e = DIL_HEAD_DIM
    steps = DIL_STEPS
    ncfg = len(DIL_CONFIGS)
    assert DIL_CONFIGS[0][1] == 1, "the config that writes the output rows must be undilated"
    return pl.pallas_call(
        functools.partial(_dil_kernel, seq=seq),
        grid=(batch, DIL_HEADS),
        in_specs=[
            pl.BlockSpec((ncfg, 1, steps, 2 * steps), lambda b, h: (0, h, 0, 0)),
            pl.BlockSpec((1, seq, e), lambda b, h: (b, 0, h)),
            pl.BlockSpec((1, seq, e), lambda b, h: (b, 0, DIL_HEADS + h)),
            pl.BlockSpec((1, seq, e), lambda b, h: (b, 0, 2 * DIL_HEADS + h)),
        ],
        out_specs=pl.BlockSpec((1, seq, e), lambda b, h: (b, 0, h)),
        out_shape=jax.ShapeDtypeStruct((batch, seq, DIL_HEADS * e), BF16),
        scratch_shapes=[pltpu.VMEM((ncfg - 1, seq, e), F32)] * 3
        + [pltpu.VMEM((DIL_SPLIT, seq // DIL_SPLIT, e), F32)] * 3,
        compiler_params=_cparams(("parallel", "parallel")),
    )(bias, qkv3, qkv3, qkv3)


def _mix_out_kernel(a_ref, b_ref, wa_ref, wb_ref, x_ref, o_ref):
    acc = jnp.dot(a_ref[...], wa_ref[...], preferred_element_type=F32)
    acc = acc + jnp.dot(b_ref[...], wb_ref[...], preferred_element_type=F32)
    o_ref[...] = x_ref[...] + acc


def _mix_out(o_gla, o_dil, w, x, bm=1024, bn=1024):
    m, ka = o_gla.shape
    kb = o_dil.shape[1]
    n = w.shape[1]
    assert ka == kb and w.shape[0] == ka + kb
    return pl.pallas_call(
        _mix_out_kernel,
        grid=(m // bm, n // bn),
        in_specs=[pl.BlockSpec((bm, ka), lambda i, j: (i, 0)),
                  pl.BlockSpec((bm, kb), lambda i, j: (i, 0)),
                  pl.BlockSpec((ka, bn), lambda i, j: (0, j)),
                  pl.BlockSpec((kb, bn), lambda i, j: (1, j)),
                  pl.BlockSpec((bm, bn), lambda i, j: (i, j))],
        out_specs=pl.BlockSpec((bm, bn), lambda i, j: (i, j)),
        out_shape=jax.ShapeDtypeStruct((m, n), F32),
        compiler_params=_cparams(("parallel", "parallel")),
    )(o_gla, o_dil, w, w, x)


XATTN_SLAB = 128


def _xattn_kernel(h_ref, gx_ref, wq_ref, k_ref, v_ref, wo_ref, gf_ref, h2_ref, hn_ref):
    e = XATTN_HEAD_DIM
    nt = (((1,), (1,)), ((), ()))
    gx = gx_ref[...]
    gf = gf_ref[...]
    slabs = [slice(r0, r0 + XATTN_SLAB) for r0 in range(0, h_ref.shape[0], XATTN_SLAB)]
    hs = [h_ref[rows, :] for rows in slabs]
    hns = []
    for h in hs:
        ms = jnp.mean(h * h, axis=-1, keepdims=True)
        hns.append(((h * lax.rsqrt(ms + RMS_EPS)) * gx).astype(BF16))
    qs = [jnp.dot(hn, wq_ref[...], preferred_element_type=F32).astype(BF16) for hn in hns]
    os_ = []
    for q in qs:
        outs = []
        for hh in range(XATTN_HEADS):
            qh = q[:, hh * e:(hh + 1) * e]
            kh = k_ref[0, :, hh * e:(hh + 1) * e]
            vh = v_ref[0, :, hh * e:(hh + 1) * e]
            s = lax.dot_general(qh, kh, nt, preferred_element_type=F32) * (e ** -0.5)
            m = jnp.max(s, axis=-1, keepdims=True)
            p = jnp.exp(s - m)
            p = p / jnp.sum(p, axis=-1, keepdims=True)
            outs.append(jnp.dot(p.astype(BF16), vh, preferred_element_type=F32))
        os_.append(jnp.concatenate(outs, axis=1).astype(BF16))
    h2s = [h + jnp.dot(o, wo_ref[...], preferred_element_type=F32) for h, o in zip(hs, os_)]
    hn2s = []
    for h2 in h2s:
        ms2 = jnp.mean(h2 * h2, axis=-1, keepdims=True)
        hn2s.append(((h2 * lax.rsqrt(ms2 + RMS_EPS)) * gf).astype(hn_ref.dtype))
    for rows, h2, hn2 in zip(slabs, h2s, hn2s):
        h2_ref[rows, :] = h2
        hn_ref[rows, :] = hn2


def _xattn(h1, gx, wq, kx, vx, wo, gf, seq, tm=512):
    m, d = h1.shape
    mem_len = kx.shape[1]
    per_seq = seq // tm
    return pl.pallas_call(
        _xattn_kernel,
        grid=(m // tm,),
        in_specs=[pl.BlockSpec((tm, d), lambda i: (i, 0)),
                  pl.BlockSpec((1, d), lambda i: (0, 0)),
                  pl.BlockSpec((d, XATTN_WIDTH), lambda i: (0, 0),
                               pipeline_mode=pl.Buffered(1)),
                  pl.BlockSpec((1, mem_len, XATTN_WIDTH), lambda i: (i // per_seq, 0, 0)),
                  pl.BlockSpec((1, mem_len, XATTN_WIDTH), lambda i: (i // per_seq, 0, 0)),
                  pl.BlockSpec((XATTN_WIDTH, d), lambda i: (0, 0),
                               pipeline_mode=pl.Buffered(1)),
                  pl.BlockSpec((1, d), lambda i: (0, 0))],
        out_specs=[pl.BlockSpec((tm, d), lambda i: (i, 0)),
                   pl.BlockSpec((tm, d), lambda i: (i, 0))],
        out_shape=[jax.ShapeDtypeStruct((m, d), F32),
                   jax.ShapeDtypeStruct((m, d), BF16)],
        compiler_params=_cparams(("parallel",)),
    )(h1, gx.reshape(1, d), wq, kx, vx, wo, gf.reshape(1, d))


def _ffn_in_kernel(x_ref, wg_ref, wu_ref, halo_ref, cw_ref, cb_ref, a_ref):
    x = x_ref[...]
    g = jnp.dot(x, wg_ref[...], preferred_element_type=F32)
    u = jnp.dot(x, wu_ref[...], preferred_element_type=F32)
    halo = halo_ref[0]
    prev1 = halo[SUBLANES - 1:SUBLANES, :]
    prev2 = halo[SUBLANES - 2:SUBLANES - 1, :]
    row = lax.broadcasted_iota(jnp.int32, (SUBLANES, g.shape[1]), 0)
    r1 = pltpu.roll(g, 1, axis=0)
    r2 = pltpu.roll(g, 2, axis=0)
    head1 = jnp.where(row == 0, prev1, r1[:SUBLANES])
    head2 = jnp.where(row == 0, prev2, jnp.where(row == 1, prev1, r2[:SUBLANES]))
    g_m1 = jnp.concatenate([head1, r1[SUBLANES:]], axis=0)
    g_m2 = jnp.concatenate([head2, r2[SUBLANES:]], axis=0)
    cw = cw_ref[...]
    y = cb_ref[...] + g_m2 * cw[0:1, :]
    y = y + g_m1 * cw[1:2, :]
    y = y + g * cw[2:3, :]
    a_ref[...] = ((y / (1.0 + jnp.exp(-y))) * u).astype(a_ref.dtype)


def _ffn_in(x, wg, wu, halo_g, cw, cb, bm=FFN_BM, bn=512):
    m, k = x.shape
    n = wg.shape[1]
    return pl.pallas_call(
        _ffn_in_kernel,
        grid=(m // bm, n // bn),
        in_specs=[pl.BlockSpec((bm, k), lambda i, j: (i, 0)),
                  pl.BlockSpec((k, bn), lambda i, j: (0, j)),
                  pl.BlockSpec((k, bn), lambda i, j: (0, j)),
                  pl.BlockSpec((1, SUBLANES, bn), lambda i, j: (i, 0, j)),
                  pl.BlockSpec((CONV_WIDTH, bn), lambda i, j: (0, j)),
                  pl.BlockSpec((1, bn), lambda i, j: (0, j))],
        out_specs=pl.BlockSpec((bm, bn), lambda i, j: (i, j)),
        out_shape=jax.ShapeDtypeStruct((m, n), BF16),
        compiler_params=_cparams(("parallel", "parallel")),
    )(x, wg, wu, halo_g, cw, cb)


def _ffn_halo_rows(hn, batch, seq, bm=FFN_BM):
    d = hn.shape[1]
    tiles = seq // bm
    tail = hn.reshape(batch, tiles, bm, d)[:, :, bm - SUBLANES:, :]
    prev = jnp.concatenate([jnp.zeros_like(tail[:, :1]), tail[:, :-1]], axis=1)
    return prev.reshape(batch * tiles * SUBLANES, d)


FFN_OUT_COLS = 1024
FFN_RES_COLS = 256
FFN_NORM_ROWS = 256


def _ffn_out_kernel(a_ref, wd_ref, h_ref, fg_ref, o_ref, *, n_res):
    kk = pl.program_id(1)
    d = o_ref.shape[1]

    @pl.when(kk == 0)
    def _():
        o_ref[...] = jnp.zeros_like(o_ref)

    a = a_ref[...]
    for c0 in range(0, d, FFN_OUT_COLS):
        cols = slice(c0, c0 + FFN_OUT_COLS)
        o_ref[:, cols] += jnp.dot(a, wd_ref[:, cols], preferred_element_type=F32)

    for c in range(n_res):
        @pl.when(kk == c)
        def _(c=c):
            cols = slice(c * FFN_RES_COLS, (c + 1) * FFN_RES_COLS)
            o_ref[:, cols] += h_ref[...]

    @pl.when(kk == pl.num_programs(1) - 1)
    def _():
        fg = fg_ref[...]
        for r0 in range(0, o_ref.shape[0], FFN_NORM_ROWS):
            rows = slice(r0, r0 + FFN_NORM_ROWS)
            h3 = o_ref[rows, :]
            ms = jnp.mean(h3 * h3, axis=-1, keepdims=True)
            o_ref[rows, :] = (h3 * lax.rsqrt(ms + RMS_EPS)) * fg


def _ffn_out(a, wd, h2, fg, tm=1024, tk=512):
    m, ff = a.shape
    d = wd.shape[1]
    n_res = d // FFN_RES_COLS
    assert ff // tk >= n_res, "one residual slab per contraction step"
    return pl.pallas_call(
        functools.partial(_ffn_out_kernel, n_res=n_res),
        grid=(m // tm, ff // tk),
        in_specs=[pl.BlockSpec((tm, tk), lambda i, k: (i, k)),
                  pl.BlockSpec((tk, d), lambda i, k: (k, 0)),
                  pl.BlockSpec((tm, FFN_RES_COLS), lambda i, k: (i, jnp.minimum(k, n_res - 1))),
                  pl.BlockSpec((1, d), lambda i, k: (0, 0))],
        out_specs=pl.BlockSpec((tm, d), lambda i, k: (i, 0)),
        out_shape=jax.ShapeDtypeStruct((m, d), F32),
        compiler_params=_cparams(("parallel", "arbitrary")),
    )(a, wd, h2, fg.reshape(1, d))


def _layer(x2, mem2, rel_bias, batch, seq, mem_len, norm_mix_g, w_in, gla_w_gate2,
           gla_b_gate, gla_norm_g, w_out, norm_xattn_g, mem_norm_g, w_xq, w_xk, w_xv,
           w_xo, norm_ffn_g, w_ffn_gate, w_ffn_up, ffn_conv_w, ffn_conv_b, w_ffn_down,
           out_g):
    w_in_t = w_in.T.astype(BF16)
    w_glr = jnp.pad(w_in_t[_W_GLR:_W_GR], ((0, LANES - GLA_LOWRANK), (0, 0)))
    w2_pad = jnp.pad(gla_w_gate2, ((0, LANES - GLA_LOWRANK), (0, 0)))
    ff_pad = D_FF_PAD - D_FF
    wg = _cast_pad_cols(w_ffn_gate, D_FF_PAD)
    wu = _cast_pad_cols(w_ffn_up, D_FF_PAD)
    wd = _cast_pad_rows(w_ffn_down, D_FF_PAD)
    cw = jnp.pad(ffn_conv_w, ((0, 0), (0, ff_pad)))
    cb = jnp.pad(ffn_conv_b, ((0, ff_pad),)).reshape(1, D_FF_PAD)
    w_out_b = w_out.astype(BF16)
    w_kv = jnp.concatenate([w_xk, w_xv], axis=1).astype(BF16)

    hn = _rmsnorm(x2, norm_mix_g)
    d_qkv = _matmul_nt(hn, w_in_t, 1024, 1024, F32, row0=_W_DQKV, n=_W_END - _W_DQKV)
    g_qkv = _matmul_nt(hn, w_in_t, 1024, 1024, BF16, row0=_W_GQKV, n=_W_GLR - _W_GQKV)
    g_r = _matmul_nt(hn, w_in_t, 1024, 1024, BF16, row0=_W_GR, n=_W_DQKV - _W_GR)
    glr = _matmul_nt(hn, w_glr, 1024, LANES, F32)
    b_cum = _gla_gate(glr, w2_pad, gla_b_gate.reshape(1, -1))
    o_gla = _gla(g_qkv, g_r, b_cum, gla_norm_g.reshape(1, -1), batch, seq)
    o_dil = _dilated(d_qkv.reshape(batch, seq, 3 * DIL_HEADS * DIL_HEAD_DIM),
                     _dil_bias(rel_bias), batch, seq)
    o_dil = o_dil.reshape(batch * seq, DIL_HEADS * DIL_HEAD_DIM)
    h1 = _mix_out(o_gla, o_dil, w_out_b, x2)

    memn = _rmsnorm(mem2, mem_norm_g)
    kv = _matmul(memn, w_kv, 512, 512, BF16)
    kx = kv[:, :XATTN_WIDTH].reshape(batch, mem_len, XATTN_WIDTH)
    vx = kv[:, XATTN_WIDTH:].reshape(batch, mem_len, XATTN_WIDTH)
    h2, hn3 = _xattn(h1, norm_xattn_g, w_xq.astype(BF16), kx, vx, w_xo.astype(BF16),
                     norm_ffn_g, seq)

    halo_x = _ffn_halo_rows(hn3, batch, seq)
    halo_g = _matmul(halo_x, wg, halo_x.shape[0], 512, F32)
    halo_g = halo_g.reshape(-1, SUBLANES, D_FF_PAD)
    act = _ffn_in(hn3, wg, wu, halo_g, cw, cb)
    return _ffn_out(act, wd, h2, out_g)


def kernel(x, mem, rel_bias, norm_mix_g, w_in, gla_w_gate2, gla_b_gate, gla_norm_g, w_out,
           norm_xattn_g, mem_norm_g, w_xq, w_xk, w_xv, w_xo, norm_ffn_g, w_ffn_gate,
           w_ffn_up, ffn_conv_w, ffn_conv_b, w_ffn_down, final_norm_g):
    batch, seq, d = x.shape
    mem_len = mem.shape[1]
    depth = w_in.shape[0]
    assert depth == 1, "the fused final rmsnorm assumes a single layer"
    out = _layer(x.reshape(batch * seq, d), mem.reshape(batch * mem_len, d), rel_bias,
                 batch, seq, mem_len, norm_mix_g[0], w_in[0], gla_w_gate2[0],
                 gla_b_gate[0], gla_norm_g[0], w_out[0], norm_xattn_g[0], mem_norm_g[0],
                 w_xq[0], w_xk[0], w_xv[0], w_xo[0], norm_ffn_g[0], w_ffn_gate[0],
                 w_ffn_up[0], ffn_conv_w[0], ffn_conv_b[0], w_ffn_down[0], final_norm_g)
    return out.reshape(batch, seq, d)
```

```python
import functools
import math

import numpy as np
import jax
import jax.numpy as jnp
from jax import lax
from jax.experimental import pallas as pl
from jax.experimental.pallas import tpu as pltpu

F32 = jnp.float32
BF16 = jnp.bfloat16

D_MODEL = 4096
RMS_EPS = 1e-6
GLA_HEADS = 4
GLA_DV = 512
GLA_DK = 256
GLA_LOWRANK = 16
GLA_TAU = 16.0
GLA_CHUNK = 64
DIL_HEAD_DIM = 128
DIL_HEADS = 16
DIL_CONFIGS = ((128, 1), (512, 4), (2048, 16))
DIL_STEPS = 128
REL_BUCKETS = 32
REL_MAX_DIST = 2048
XATTN_HEADS = 4
XATTN_HEAD_DIM = 128
XATTN_WIDTH = 512
D_FF = 11008
CONV_WIDTH = 3
NEG_INF = -1e30
LOG2E = math.log2(math.e)

LANES = 128
SUBLANES = 8
ROW_ALIGN = 16
VMEM_LIMIT = 56 * 1024 * 1024

D_FF_PAD = 11264
FFN_BM = 1024

_W_GQKV = 0
_W_GLR = 4096
_W_GR = 4112
_W_DQKV = 6160
_W_END = 12304


def _cparams(sem):
    return pltpu.CompilerParams(dimension_semantics=sem, vmem_limit_bytes=VMEM_LIMIT)


def _rmsnorm_kernel(x_ref, g_ref, o_ref):
    x = x_ref[...]
    ms = jnp.mean(x * x, axis=-1, keepdims=True)
    o_ref[...] = ((x * lax.rsqrt(ms + RMS_EPS)) * g_ref[...]).astype(o_ref.dtype)


def _rmsnorm(x, g, tm=256):
    m, d = x.shape
    return pl.pallas_call(
        _rmsnorm_kernel,
        grid=(m // tm,),
        in_specs=[pl.BlockSpec((tm, d), lambda i: (i, 0)),
                  pl.BlockSpec((1, d), lambda i: (0, 0))],
        out_specs=pl.BlockSpec((tm, d), lambda i: (i, 0)),
        out_shape=jax.ShapeDtypeStruct((m, d), BF16),
        compiler_params=_cparams(("parallel",)),
    )(x, g.reshape(1, d))


def _cast_pad_cols_kernel(w_ref, o_ref):
    n = w_ref.shape[1]
    o_ref[:, :n] = w_ref[...].astype(o_ref.dtype)
    o_ref[:, n:] = jnp.zeros((o_ref.shape[0], o_ref.shape[1] - n), o_ref.dtype)


def _cast_pad_cols(w, n_pad, tm=256):
    k, n = w.shape
    return pl.pallas_call(
        _cast_pad_cols_kernel,
        grid=(k // tm,),
        in_specs=[pl.BlockSpec((tm, n), lambda i: (i, 0))],
        out_specs=pl.BlockSpec((tm, n_pad), lambda i: (i, 0)),
        out_shape=jax.ShapeDtypeStruct((k, n_pad), BF16),
        compiler_params=_cparams(("parallel",)),
    )(w)


def _cast_pad_rows_kernel(w_ref, o_ref, *, n_valid):
    valid = pl.program_id(0) < n_valid
    o_ref[...] = jnp.where(valid, w_ref[...], 0.0).astype(o_ref.dtype)


def _cast_pad_rows(w, k_pad, tm=256):
    k, n = w.shape
    n_valid = k // tm
    return pl.pallas_call(
        functools.partial(_cast_pad_rows_kernel, n_valid=n_valid),
        grid=(k_pad // tm,),
        in_specs=[pl.BlockSpec((tm, n), lambda i: (jnp.minimum(i, n_valid - 1), 0))],
        out_specs=pl.BlockSpec((tm, n), lambda i: (i, 0)),
        out_shape=jax.ShapeDtypeStruct((k_pad, n), BF16),
        compiler_params=_cparams(("parallel",)),
    )(w)


def _mm_nt_kernel(x_ref, wt_ref, o_ref):
    o_ref[...] = lax.dot_general(x_ref[...], wt_ref[...], (((1,), (1,)), ((), ())),
                                 preferred_element_type=F32).astype(o_ref.dtype)


def _matmul_nt(x, wt, bm, bn, out_dtype, row0=0, n=None):
    m, k = x.shape
    n = wt.shape[0] if n is None else n
    return pl.pallas_call(
        _mm_nt_kernel,
        grid=(m // bm, n // bn),
        in_specs=[pl.BlockSpec((bm, k), lambda i, j: (i, 0)),
                  pl.BlockSpec((pl.Element(bn), pl.Element(k)),
                               lambda i, j: (pl.multiple_of(row0 + j * bn, ROW_ALIGN), 0))],
        out_specs=pl.BlockSpec((bm, bn), lambda i, j: (i, j)),
        out_shape=jax.ShapeDtypeStruct((m, n), out_dtype),
        compiler_params=_cparams(("parallel", "parallel")),
    )(x, wt)


def _mm_kernel(x_ref, w_ref, o_ref):
    o_ref[...] = jnp.dot(x_ref[...], w_ref[...],
                         preferred_element_type=F32).astype(o_ref.dtype)


def _matmul(x, w, bm, bn, out_dtype):
    m, k = x.shape
    n = w.shape[1]
    return pl.pallas_call(
        _mm_kernel,
        grid=(m // bm, n // bn),
        in_specs=[pl.BlockSpec((bm, k), lambda i, j: (i, 0)),
                  pl.BlockSpec((k, bn), lambda i, j: (0, j))],
        out_specs=pl.BlockSpec((bm, bn), lambda i, j: (i, j)),
        out_shape=jax.ShapeDtypeStruct((m, n), out_dtype),
        compiler_params=_cparams(("parallel", "parallel")),
    )(x, w)


GLA_T = 512
GLA_GROUP = 4
GLA_CUMSUM_ROWS = 256


def _split_bf16(x):
    hi = x.astype(BF16)
    lo = (x - hi.astype(F32)).astype(BF16)
    return hi, lo


def _gla_gate_kernel(glr_ref, w2_ref, b2_ref, tri_ref, b_ref):
    x_hi, x_lo = _split_bf16(glr_ref[...])
    w_hi, w_lo = _split_bf16(w2_ref[...])
    z = (jnp.dot(x_hi, w_hi, preferred_element_type=F32)
         + jnp.dot(x_lo, w_hi, preferred_element_type=F32)
         + jnp.dot(x_hi, w_lo, preferred_element_type=F32)) + b2_ref[...]
    log_sig = jnp.minimum(z, 0.0) - jnp.log(1.0 + jnp.exp(-jnp.abs(z)))
    g_hi, g_lo = _split_bf16(log_sig / GLA_TAU)
    tri = tri_ref[...]
    t_sub = tri.shape[0]
    for r0 in range(0, g_hi.shape[0], t_sub):
        rows = slice(r0, r0 + t_sub)
        b_ref[rows, :] = (jnp.dot(tri, g_hi[rows], preferred_element_type=F32)
                          + jnp.dot(tri, g_lo[rows], preferred_element_type=F32))


def _chunk_tril(t):
    idx = np.arange(t)
    same = (idx[:, None] // GLA_CHUNK) == (idx[None, :] // GLA_CHUNK)
    return jnp.asarray((same & (idx[:, None] >= idx[None, :])).astype(np.float32), BF16)


def _gla_gate(glr, w2_pad, b2, t=GLA_T):
    m = glr.shape[0]
    n = w2_pad.shape[1]
    return pl.pallas_call(
        _gla_gate_kernel,
        grid=(m // t,),
        in_specs=[pl.BlockSpec((t, LANES), lambda i: (i, 0)),
                  pl.BlockSpec((LANES, n), lambda i: (0, 0)),
                  pl.BlockSpec((1, n), lambda i: (0, 0)),
                  pl.BlockSpec((GLA_CUMSUM_ROWS, GLA_CUMSUM_ROWS), lambda i: (0, 0))],
        out_specs=pl.BlockSpec((t, n), lambda i: (i, 0)),
        out_shape=jax.ShapeDtypeStruct((m, n), F32),
        compiler_params=_cparams(("parallel",)),
    )(glr, w2_pad, b2, _chunk_tril(GLA_CUMSUM_ROWS))


def _gla_kernel(q_ref, k_ref, v_ref, r_ref, b_ref, gn_ref, o_ref, state_ref):
    c_sz = GLA_CHUNK
    g_sz = GLA_GROUP * c_sz
    assert GLA_GROUP == 4

    @pl.when(pl.program_id(2) == 0)
    def _():
        state_ref[...] = jnp.zeros_like(state_ref)

    row = lax.broadcasted_iota(jnp.int32, (g_sz, g_sz), 0)
    col = lax.broadcasted_iota(jnp.int32, (g_sz, g_sz), 1)
    causal = row >= col
    gn = gn_ref[...]
    nt = (((1,), (1,)), ((), ()))
    tn = (((0,), (0,)), ((), ()))

    def cat(parts):
        return jnp.concatenate(parts, axis=0)

    def group(gi, carry):
        rows = pl.ds(pl.multiple_of(gi * g_sz, g_sz), g_sz)
        b_all = b_ref[rows, :]
        q_all = q_ref[rows, :].astype(F32) * (GLA_DK ** -0.5)
        k_all = k_ref[rows, :].astype(F32)
        v = v_ref[rows, :]
        st = state_ref[...]

        sl = [slice(c * c_sz, (c + 1) * c_sz) for c in range(GLA_GROUP)]
        b = [b_all[s] for s in sl]
        bl = [x[c_sz - 1:c_sz, :] for x in b]
        bm = [x[c_sz // 2:c_sz // 2 + 1, :] for x in b]
        q_start = [q_all[s] * jnp.exp(x) for s, x in zip(sl, b)]
        k_end = [k_all[s] * jnp.exp(t - x) for s, x, t in zip(sl, b, bl)]
        q_mid = [(q_all[s] * jnp.exp(x - m)).astype(BF16) for s, x, m in zip(sl, b, bm)]
        k_mid = [(k_all[s] * jnp.exp(m - x)).astype(BF16) for s, x, m in zip(sl, b, bm)]

        e1, e2, e3 = jnp.exp(bl[1]), jnp.exp(bl[2]), jnp.exp(bl[3])
        e0 = jnp.exp(bl[0])
        d01, d12, d23 = e0 * e1, e1 * e2, e2 * e3
        d012, d123 = d01 * e2, d12 * e3
        d_all = d012 * e3

        qs = cat([q_start[0], q_start[1] * e0, q_start[2] * d01, q_start[3] * d012]).astype(BF16)
        o = lax.dot_general(qs, st.astype(BF16), nt, preferred_element_type=F32)

        k_end_b = [x.astype(BF16) for x in k_end]
        a_r0 = lax.dot_general(q_mid[0], cat([k_mid[0], k_mid[1]]), nt,
                               preferred_element_type=F32)
        a_r1 = lax.dot_general(q_mid[1], cat([(k_end[0] * jnp.exp(bm[1])).astype(BF16),
                                              k_mid[1]]), nt, preferred_element_type=F32)
        a_r2 = lax.dot_general(q_mid[2], cat([k_mid[2], k_mid[3]]), nt,
                               preferred_element_type=F32)
        a_r3 = lax.dot_general(q_mid[3], cat([(k_end[2] * jnp.exp(bm[3])).astype(BF16),
                                              k_mid[3]]), nt, preferred_element_type=F32)
        a_off = lax.dot_general(cat([q_start[2], q_start[3] * e2]).astype(BF16),
                                cat([(k_end[0] * e1).astype(BF16), k_end_b[1]]), nt,
                                preferred_element_type=F32)
        zeros = jnp.zeros((2 * c_sz, 2 * c_sz), F32)
        att = jnp.concatenate([cat([a_r0, a_r1, a_off]), cat([zeros, a_r2, a_r3])], axis=1)
        att = jnp.where(causal, att, 0.0).astype(BF16)
        o = o + jnp.dot(att, v, preferred_element_type=F32)

        k_fin = cat([k_end[0] * d123, k_end[1] * d23, k_end[2] * e3, k_end[3]]).astype(BF16)
        kv_t = lax.dot_general(v, k_fin, tn, preferred_element_type=F32)
        state_new = st * d_all + kv_t

        ms = jnp.mean(o * o, axis=-1, keepdims=True)
        on = (o * lax.rsqrt(ms + RMS_EPS)) * gn
        r = r_ref[rows, :].astype(F32)
        gate = r / (1.0 + jnp.exp(-r))
        o_ref[rows, :] = (on * gate).astype(o_ref.dtype)
        state_ref[...] = state_new
        return carry

    lax.fori_loop(0, GLA_T // g_sz, group, 0, unroll=True)


def _gla(qkv, gr, b_cum, gn, batch, seq):
    t = GLA_T
    nt_ = seq // t
    m = batch * seq
    kb = (GLA_HEADS * GLA_DK) // GLA_DK
    vb = (2 * GLA_HEADS * GLA_DK) // GLA_DV
    return pl.pallas_call(
        _gla_kernel,
        grid=(batch, GLA_HEADS, nt_),
        in_specs=[
            pl.BlockSpec((t, GLA_DK), lambda b, h, s: (b * nt_ + s, h)),
            pl.BlockSpec((t, GLA_DK), lambda b, h, s: (b * nt_ + s, kb + h)),
            pl.BlockSpec((t, GLA_DV), lambda b, h, s: (b * nt_ + s, vb + h)),
            pl.BlockSpec((t, GLA_DV), lambda b, h, s: (b * nt_ + s, h)),
            pl.BlockSpec((t, GLA_DK), lambda b, h, s: (b * nt_ + s, h)),
            pl.BlockSpec((1, GLA_DV), lambda b, h, s: (0, 0)),
        ],
        out_specs=pl.BlockSpec((t, GLA_DV), lambda b, h, s: (b * nt_ + s, h)),
        out_shape=jax.ShapeDtypeStruct((m, GLA_HEADS * GLA_DV), BF16),
        scratch_shapes=[pltpu.VMEM((GLA_DV, GLA_DK), F32)],
        compiler_params=_cparams(("parallel", "parallel", "arbitrary")),
    )(qkv, qkv, qkv, gr, b_cum, gn)


def _t5_bucket_np(dist):
    max_exact = REL_BUCKETS // 2
    d_f = np.maximum(dist, 1).astype(np.float32)
    large = max_exact + (np.log(d_f / np.float32(max_exact))
                         / np.float32(math.log(REL_MAX_DIST / max_exact))
                         * np.float32(REL_BUCKETS - max_exact)).astype(np.int32)
    large = np.minimum(large, REL_BUCKETS - 1)
    return np.where(dist < max_exact, dist, large)


def _dil_bucket_index():
    steps = DIL_STEPS
    qi = np.arange(steps)[:, None]
    kj = np.arange(2 * steps)[None, :]
    rel = qi + steps - kj
    band = (rel >= 0) & (rel <= steps)
    out = []
    for _, dil in DIL_CONFIGS:
        bucket = _t5_bucket_np(np.clip(rel, 0, steps) * dil)
        out.append(np.where(band, bucket, REL_BUCKETS))
    return np.stack(out).astype(np.int32)


def _dil_bias_kernel(relb_ref, idx_ref, o_ref):
    head = pl.program_id(1)
    idx = idx_ref[0]
    bias = jnp.full(idx.shape, NEG_INF, F32)
    for bkt in range(REL_BUCKETS):
        bias = jnp.where(idx == bkt, relb_ref[bkt, head] * LOG2E, bias)
    o_ref[0, 0] = bias


def _dil_bias(rel_bias):
    steps = DIL_STEPS
    ncfg = len(DIL_CONFIGS)
    idx = jnp.asarray(_dil_bucket_index())
    return pl.pallas_call(
        _dil_bias_kernel,
        grid=(ncfg, DIL_HEADS),
        in_specs=[pl.BlockSpec(memory_space=pltpu.SMEM),
                  pl.BlockSpec((1, steps, 2 * steps), lambda c, h: (c, 0, 0))],
        out_specs=pl.BlockSpec((1, 1, steps, 2 * steps), lambda c, h: (c, h, 0, 0)),
        out_shape=jax.ShapeDtypeStruct((ncfg, DIL_HEADS, steps, 2 * steps), F32),
        compiler_params=_cparams(("parallel", "parallel")),
    )(rel_bias, idx)


DIL_GROUP = 8
DIL_SPLIT = 4
DIL_COPY_ROWS = 256


def _dil_kernel(bias_ref, q_ref, k_ref, v_ref, o_ref, m_sc, l_sc, acc_sc, qd, kd, vd, *, seq):
    steps = DIL_STEPS
    e = DIL_HEAD_DIM
    scale = e ** -0.5 * LOG2E
    nt = (((1,), (1,)), ((), ()))
    n_cfg = len(DIL_CONFIGS)
    split = DIL_SPLIT

    def rows_of(start, size, stride):
        if stride == 1:
            return pl.ds(start, size)
        return pl.ds(start, size, stride=stride)

    for src, dst in ((q_ref, qd), (k_ref, kd), (v_ref, vd)):
        for r in range(split):
            for c0 in range(0, seq // split, DIL_COPY_ROWS):
                dst[r, c0:c0 + DIL_COPY_ROWS, :] = src[
                    0, pl.ds(r + split * c0, DIL_COPY_ROWS, stride=split), :]

    def load(nat_ref, split_ref, dil, r, n, span, size):
        if dil % split:
            return nat_ref[0, rows_of(n * span + r, size, dil), :].astype(BF16)
        rows = rows_of(n * (span // split) + r // split, size, dil // split)
        return split_ref[r % split, rows, :].astype(BF16)

    def attend(c, dil, blocks, merge):
        span = steps * dil

        def q_start(r, n_q):
            start = n_q * span + r
            if dil == 1 and not isinstance(start, int):
                start = pl.multiple_of(start, span)
            return start

        q_rows = [rows_of(q_start(r, n_q), steps, dil) for r, n_q, _, _, _ in blocks]
        logits = []
        for r, n_q, n_k, n_keys, bias in blocks:
            q = load(q_ref, qd, dil, r, n_q, span, steps)
            k = load(k_ref, kd, dil, r, n_k, span, n_keys)
            logits.append(lax.dot_general(q, k, nt, preferred_element_type=F32) * scale + bias)
        stats = []
        for s in logits:
            m = jnp.max(s, axis=-1, keepdims=True)
            p = jnp.exp2(s - m)
            stats.append((m, jnp.sum(p, axis=-1, keepdims=True), p.astype(BF16)))
        pvs = [jnp.dot(p, load(v_ref, vd, dil, r, n_k, span, n_keys), preferred_element_type=F32)
               for (_, _, p), (r, _, n_k, n_keys, _) in zip(stats, blocks)]
        if not merge:
            for qr, (m, l, _), pv in zip(q_rows, stats, pvs):
                m_sc[c - 1, qr, :] = jnp.broadcast_to(m, (steps, e))
                l_sc[c - 1, qr, :] = jnp.broadcast_to(l, (steps, e))
                acc_sc[c - 1, qr, :] = pv
            return
        outs = []
        for qr, (m, l, _), pv in zip(q_rows, stats, pvs):
            ms = [m] + [m_sc[i, qr, :] for i in range(n_cfg - 1)]
            ls = [l] + [l_sc[i, qr, :] for i in range(n_cfg - 1)]
            accs = [pv] + [acc_sc[i, qr, :] for i in range(n_cfg - 1)]
            m_max = functools.reduce(jnp.maximum, ms)
            num = None
            den = None
            for m_i, l_i, acc_i in zip(ms, ls, accs):
                wgt = jnp.exp2(m_i - m_max)
                num = wgt * acc_i if num is None else num + wgt * acc_i
                den = wgt * l_i if den is None else den + wgt * l_i
            outs.append((qr, (num / den).astype(o_ref.dtype)))
        for qr, o in outs:
            o_ref[0, qr, :] = o

    for c in reversed(range(n_cfg)):
        dil = DIL_CONFIGS[c][1]
        nb = seq // dil // steps

        def first(r, c=c):
            return (r, 0, 0, steps, bias_ref[c, 0, :, steps:2 * steps])

        def later(j, c=c, dil=dil, nb=nb):
            if dil == 1:
                return (0, 1 + j, j, 2 * steps, bias_ref[c, 0])
            r = j // (nb - 1)
            n = 1 + j % (nb - 1)
            return (r, n, n - 1, 2 * steps, bias_ref[c, 0])

        for make, count in ((first, dil), (later, dil * (nb - 1))):
            full, rest = divmod(count, DIL_GROUP)

            def group(gi, carry, c=c, dil=dil, make=make):
                attend(c, dil, [make(gi * DIL_GROUP + u) for u in range(DIL_GROUP)], c == 0)
                return carry

            if full:
                lax.fori_loop(0, full, group, 0)
            if rest:
                attend(c, dil, [make(full * DIL_GROUP + u) for u in range(rest)], c == 0)


def _dilated(qkv3, bias, batch, seq):
    e = DIL_HEAD_DIM
    steps = DIL_STEPS
    ncfg = len(DIL_CONFIGS)
    assert DIL_CONFIGS[0][1] == 1, "the config that writes the output rows must be undilated"
    return pl.pallas_call(
        functools.partial(_dil_kernel, seq=seq),
        grid=(batch, DIL_HEADS),
        in_specs=[
            pl.BlockSpec((ncfg, 1, steps, 2 * steps), lambda b, h: (0, h, 0, 0)),
            pl.BlockSpec((1, seq, e), lambda b, h: (b, 0, h)),
            pl.BlockSpec((1, seq, e), lambda b, h: (b, 0, DIL_HEADS + h)),
            pl.BlockSpec((1, seq, e), lambda b, h: (b, 0, 2 * DIL_HEADS + h)),
        ],
        out_specs=pl.BlockSpec((1, seq, e), lambda b, h: (b, 0, h)),
        out_shape=jax.ShapeDtypeStruct((batch, seq, DIL_HEADS * e), BF16),
        scratch_shapes=[pltpu.VMEM((ncfg - 1, seq, e), F32)] * 3
        + [pltpu.VMEM((DIL_SPLIT, seq // DIL_SPLIT, e), F32)] * 3,
        compiler_params=_cparams(("parallel", "parallel")),
    )(bias, qkv3, qkv3, qkv3)


def _mix_out_kernel(a_ref, b_ref, wa_ref, wb_ref, x_ref, o_ref):
    acc = jnp.dot(a_ref[...], wa_ref[...], preferred_element_type=F32)
    acc = acc + jnp.dot(b_ref[...], wb_ref[...], preferred_element_type=F32)
    o_ref[...] = x_ref[...] + acc


def _mix_out(o_gla, o_dil, w, x, bm=1024, bn=1024):
    m, ka = o_gla.shape
    kb = o_dil.shape[1]
    n = w.shape[1]
    assert ka == kb and w.shape[0] == ka + kb
    return pl.pallas_call(
        _mix_out_kernel,
        grid=(m // bm, n // bn),
        in_specs=[pl.BlockSpec((bm, ka), lambda i, j: (i, 0)),
                  pl.BlockSpec((bm, kb), lambda i, j: (i, 0)),
                  pl.BlockSpec((ka, bn), lambda i, j: (0, j)),
                  pl.BlockSpec((kb, bn), lambda i, j: (1, j)),
                  pl.BlockSpec((bm, bn), lambda i, j: (i, j))],
        out_specs=pl.BlockSpec((bm, bn), lambda i, j: (i, j)),
        out_shape=jax.ShapeDtypeStruct((m, n), F32),
        compiler_params=_cparams(("parallel", "parallel")),
    )(o_gla, o_dil, w, w, x)


XATTN_SLAB = 128


def _xattn_kernel(h_ref, gx_ref, wq_ref, k_ref, v_ref, wo_ref, gf_ref, h2_ref, hn_ref):
    e = XATTN_HEAD_DIM
    nt = (((1,), (1,)), ((), ()))
    gx = gx_ref[...]
    gf = gf_ref[...]
    slabs = [slice(r0, r0 + XATTN_SLAB) for r0 in range(0, h_ref.shape[0], XATTN_SLAB)]
    hs = [h_ref[rows, :] for rows in slabs]
    hns = []
    for h in hs:
        ms = jnp.mean(h * h, axis=-1, keepdims=True)
        hns.append(((h * lax.rsqrt(ms + RMS_EPS)) * gx).astype(BF16))
    qs = [jnp.dot(hn, wq_ref[...], preferred_element_type=F32).astype(BF16) for hn in hns]
    os_ = []
    for q in qs:
        outs = []
        for hh in range(XATTN_HEADS):
            qh = q[:, hh * e:(hh + 1) * e]
            kh = k_ref[0, :, hh * e:(hh + 1) * e]
            vh = v_ref[0, :, hh * e:(hh + 1) * e]
            s = lax.dot_general(qh, kh, nt, preferred_element_type=F32) * (e ** -0.5)
            m = jnp.max(s, axis=-1, keepdims=True)
            p = jnp.exp(s - m)
            p = p / jnp.sum(p, axis=-1, keepdims=True)
            outs.append(jnp.dot(p.astype(BF16), vh, preferred_element_type=F32))
        os_.append(jnp.concatenate(outs, axis=1).astype(BF16))
    h2s = [h + jnp.dot(o, wo_ref[...], preferred_element_type=F32) for h, o in zip(hs, os_)]
    hn2s = []
    for h2 in h2s:
        ms2 = jnp.mean(h2 * h2, axis=-1, keepdims=True)
        hn2s.append(((h2 * lax.rsqrt(ms2 + RMS_EPS)) * gf).astype(hn_ref.dtype))
    for rows, h2, hn2 in zip(slabs, h2s, hn2s):
        h2_ref[rows, :] = h2
        hn_ref[rows, :] = hn2


def _xattn(h1, gx, wq, kx, vx, wo, gf, seq, tm=512):
    m, d = h1.shape
    mem_len = kx.shape[1]
    per_seq = seq // tm
    return pl.pallas_call(
        _xattn_kernel,
        grid=(m // tm,),
        in_specs=[pl.BlockSpec((tm, d), lambda i: (i, 0)),
                  pl.BlockSpec((1, d), lambda i: (0, 0)),
                  pl.BlockSpec((d, XATTN_WIDTH), lambda i: (0, 0),
                               pipeline_mode=pl.Buffered(1)),
                  pl.BlockSpec((1, mem_len, XATTN_WIDTH), lambda i: (i // per_seq, 0, 0)),
                  pl.BlockSpec((1, mem_len, XATTN_WIDTH), lambda i: (i // per_seq, 0, 0)),
                  pl.BlockSpec((XATTN_WIDTH, d), lambda i: (0, 0),
                               pipeline_mode=pl.Buffered(1)),
                  pl.BlockSpec((1, d), lambda i: (0, 0))],
        out_specs=[pl.BlockSpec((tm, d), lambda i: (i, 0)),
                   pl.BlockSpec((tm, d), lambda i: (i, 0))],
        out_shape=[jax.ShapeDtypeStruct((m, d), F32),
                   jax.ShapeDtypeStruct((m, d), BF16)],
        compiler_params=_cparams(("parallel",)),
    )(h1, gx.reshape(1, d), wq, kx, vx, wo, gf.reshape(1, d))


def _ffn_in_kernel(x_ref, wg_ref, wu_ref, halo_ref, cw_ref, cb_ref, a_ref):
    x = x_ref[...]
    g = jnp.dot(x, wg_ref[...], preferred_element_type=F32)
    u = jnp.dot(x, wu_ref[...], preferred_element_type=F32)
    halo = halo_ref[0]
    prev1 = halo[SUBLANES - 1:SUBLANES, :]
    prev2 = halo[SUBLANES - 2:SUBLANES - 1, :]
    row = lax.broadcasted_iota(jnp.int32, (SUBLANES, g.shape[1]), 0)
    r1 = pltpu.roll(g, 1, axis=0)
    r2 = pltpu.roll(g, 2, axis=0)
    head1 = jnp.where(row == 0, prev1, r1[:SUBLANES])
    head2 = jnp.where(row == 0, prev2, jnp.where(row == 1, prev1, r2[:SUBLANES]))
    g_m1 = jnp.concatenate([head1, r1[SUBLANES:]], axis=0)
    g_m2 = jnp.concatenate([head2, r2[SUBLANES:]], axis=0)
    cw = cw_ref[...]
    y = cb_ref[...] + g_m2 * cw[0:1, :]
    y = y + g_m1 * cw[1:2, :]
    y = y + g * cw[2:3, :]
    a_ref[...] = ((y / (1.0 + jnp.exp(-y))) * u).astype(a_ref.dtype)


def _ffn_in(x, wg, wu, halo_g, cw, cb, bm=FFN_BM, bn=512):
    m, k = x.shape
    n = wg.shape[1]
    return pl.pallas_call(
        _ffn_in_kernel,
        grid=(m // bm, n // bn),
        in_specs=[pl.BlockSpec((bm, k), lambda i, j: (i, 0)),
                  pl.BlockSpec((k, bn), lambda i, j: (0, j)),
                  pl.BlockSpec((k, bn), lambda i, j: (0, j)),
                  pl.BlockSpec((1, SUBLANES, bn), lambda i, j: (i, 0, j)),
                  pl.BlockSpec((CONV_WIDTH, bn), lambda i, j: (0, j)),
                  pl.BlockSpec((1, bn), lambda i, j: (0, j))],
        out_specs=pl.BlockSpec((bm, bn), lambda i, j: (i, j)),
        out_shape=jax.ShapeDtypeStruct((m, n), BF16),
        compiler_params=_cparams(("parallel", "parallel")),
    )(x, wg, wu, halo_g, cw, cb)


def _ffn_halo_rows(hn, batch, seq, bm=FFN_BM):
    d = hn.shape[1]
    tiles = seq // bm
    tail = hn.reshape(batch, tiles, bm, d)[:, :, bm - SUBLANES:, :]
    prev = jnp.concatenate([jnp.zeros_like(tail[:, :1]), tail[:, :-1]], axis=1)
    return prev.reshape(batch * tiles * SUBLANES, d)


FFN_OUT_COLS = 1024
FFN_RES_COLS = 512
FFN_OUT_VMEM_LIMIT = 63 * 1024 * 1024
FFN_NORM_ROWS = 256


def _ffn_out_kernel(a_ref, wd_ref, h_ref, fg_ref, o_ref, *, n_res):
    kk = pl.program_id(1)
    d = o_ref.shape[1]

    @pl.when(kk == 0)
    def _():
        o_ref[...] = jnp.zeros_like(o_ref)

    a = a_ref[...]
    for c0 in range(0, d, FFN_OUT_COLS):
        cols = slice(c0, c0 + FFN_OUT_COLS)
        o_ref[:, cols] += jnp.dot(a, wd_ref[:, cols], preferred_element_type=F32)

    for c in range(n_res):
        @pl.when(kk == c)
        def _(c=c):
            cols = slice(c * FFN_RES_COLS, (c + 1) * FFN_RES_COLS)
            o_ref[:, cols] += h_ref[...]

    @pl.when(kk == pl.num_programs(1) - 1)
    def _():
        fg = fg_ref[...]
        for r0 in range(0, o_ref.shape[0], FFN_NORM_ROWS):
            rows = slice(r0, r0 + FFN_NORM_ROWS)
            h3 = o_ref[rows, :]
            ms = jnp.mean(h3 * h3, axis=-1, keepdims=True)
            o_ref[rows, :] = (h3 * lax.rsqrt(ms + RMS_EPS)) * fg


def _ffn_out(a, wd, h2, fg, tm=1024, tk=1024):
    m, ff = a.shape
    d = wd.shape[1]
    n_res = d // FFN_RES_COLS
    assert ff // tk >= n_res, "one residual slab per contraction step"
    return pl.pallas_call(
        functools.partial(_ffn_out_kernel, n_res=n_res),
        grid=(m // tm, ff // tk),
        in_specs=[pl.BlockSpec((tm, tk), lambda i, k: (i, k)),
                  pl.BlockSpec((tk, d), lambda i, k: (k, 0)),
                  pl.BlockSpec((tm, FFN_RES_COLS), lambda i, k: (i, jnp.minimum(k, n_res - 1))),
                  pl.BlockSpec((1, d), lambda i, k: (0, 0))],
        out_specs=pl.BlockSpec((tm, d), lambda i, k: (i, 0)),
        out_shape=jax.ShapeDtypeStruct((m, d), F32),
        compiler_params=pltpu.CompilerParams(dimension_semantics=("parallel", "arbitrary"),
                                             vmem_limit_bytes=FFN_OUT_VMEM_LIMIT),
    )(a, wd, h2, fg.reshape(1, d))


def _layer(x2, mem2, rel_bias, batch, seq, mem_len, norm_mix_g, w_in, gla_w_gate2,
           gla_b_gate, gla_norm_g, w_out, norm_xattn_g, mem_norm_g, w_xq, w_xk, w_xv,
           w_xo, norm_ffn_g, w_ffn_gate, w_ffn_up, ffn_conv_w, ffn_conv_b, w_ffn_down,
           out_g):
    w_in_t = w_in.T.astype(BF16)
    w_glr = jnp.pad(w_in_t[_W_GLR:_W_GR], ((0, LANES - GLA_LOWRANK), (0, 0)))
    w2_pad = jnp.pad(gla_w_gate2, ((0, LANES - GLA_LOWRANK), (0, 0)))
    ff_pad = D_FF_PAD - D_FF
    wg = _cast_pad_cols(w_ffn_gate, D_FF_PAD)
    wu = _cast_pad_cols(w_ffn_up, D_FF_PAD)
    wd = _cast_pad_rows(w_ffn_down, D_FF_PAD)
    cw = jnp.pad(ffn_conv_w, ((0, 0), (0, ff_pad)))
    cb = jnp.pad(ffn_conv_b, ((0, ff_pad),)).reshape(1, D_FF_PAD)
    w_out_b = w_out.astype(BF16)
    w_kv = jnp.concatenate([w_xk, w_xv], axis=1).astype(BF16)

    hn = _rmsnorm(x2, norm_mix_g)
    d_qkv = _matmul_nt(hn, w_in_t, 1024, 1024, F32, row0=_W_DQKV, n=_W_END - _W_DQKV)
    g_qkv = _matmul_nt(hn, w_in_t, 1024, 1024, BF16, row0=_W_GQKV, n=_W_GLR - _W_GQKV)
    g_r = _matmul_nt(hn, w_in_t, 1024, 1024, BF16, row0=_W_GR, n=_W_DQKV - _W_GR)
    glr = _matmul_nt(hn, w_glr, 1024, LANES, F32)
    b_cum = _gla_gate(glr, w2_pad, gla_b_gate.reshape(1, -1))
    o_gla = _gla(g_qkv, g_r, b_cum, gla_norm_g.reshape(1, -1), batch, seq)
    o_dil = _dilated(d_qkv.reshape(batch, seq, 3 * DIL_HEADS * DIL_HEAD_DIM),
                     _dil_bias(rel_bias), batch, seq)
    o_dil = o_dil.reshape(batch * seq, DIL_HEADS * DIL_HEAD_DIM)
    h1 = _mix_out(o_gla, o_dil, w_out_b, x2)

    memn = _rmsnorm(mem2, mem_norm_g)
    kv = _matmul(memn, w_kv, 512, 512, BF16)
    kx = kv[:, :XATTN_WIDTH].reshape(batch, mem_len, XATTN_WIDTH)
    vx = kv[:, XATTN_WIDTH:].reshape(batch, mem_len, XATTN_WIDTH)
    h2, hn3 = _xattn(h1, norm_xattn_g, w_xq.astype(BF16), kx, vx, w_xo.astype(BF16),
                     norm_ffn_g, seq)

    halo_x = _ffn_halo_rows(hn3, batch, seq)
    halo_g = _matmul(halo_x, wg, halo_x.shape[0], 512, F32)
    halo_g = halo_g.reshape(-1, SUBLANES, D_FF_PAD)
    act = _ffn_in(hn3, wg, wu, halo_g, cw, cb)
    return _ffn_out(act, wd, h2, out_g)


def kernel(x, mem, rel_bias, norm_mix_g, w_in, gla_w_gate2, gla_b_gate, gla_norm_g, w_out,
           norm_xattn_g, mem_norm_g, w_xq, w_xk, w_xv, w_xo, norm_ffn_g, w_ffn_gate,
           w_ffn_up, ffn_conv_w, ffn_conv_b, w_ffn_down, final_norm_g):
    batch, seq, d = x.shape
    mem_len = mem.shape[1]
    depth = w_in.shape[0]
    assert depth == 1, "the fused final rmsnorm assumes a single layer"
    out = _layer(x.reshape(batch * seq, d), mem.reshape(batch * mem_len, d), rel_bias,
                 batch, seq, mem_len, norm_mix_g[0], w_in[0], gla_w_gate2[0],
                 gla_b_gate[0], gla_norm_g[0], w_out[0], norm_xattn_g[0], mem_norm_g[0],
                 w_xq[0], w_xk[0], w_xv[0], w_xo[0], norm_ffn_g[0], w_ffn_gate[0],
                 w_ffn_up[0], ffn_conv_w[0], ffn_conv_b[0], w_ffn_down[0], final_norm_g)
    return out.reshape(batch, seq, d)
```

```python
import functools
import math

import numpy as np
import jax
import jax.numpy as jnp
from jax import lax
from jax.experimental import pallas as pl
from jax.experimental.pallas import tpu as pltpu

F32 = jnp.float32
BF16 = jnp.bfloat16

D_MODEL = 4096
RMS_EPS = 1e-6
GLA_HEADS = 4
GLA_DV = 512
GLA_DK = 256
GLA_LOWRANK = 16
GLA_TAU = 16.0
GLA_CHUNK = 64
DIL_HEAD_DIM = 128
DIL_HEADS = 16
DIL_CONFIGS = ((128, 1), (512, 4), (2048, 16))
DIL_STEPS = 128
REL_BUCKETS = 32
REL_MAX_DIST = 2048
XATTN_HEADS = 4
XATTN_HEAD_DIM = 128
XATTN_WIDTH = 512
D_FF = 11008
CONV_WIDTH = 3
NEG_INF = -1e30
LOG2E = math.log2(math.e)

LANES = 128
SUBLANES = 8
ROW_ALIGN = 16
VMEM_LIMIT = 56 * 1024 * 1024

D_FF_PAD = 11264
FFN_BM = 1024

_W_GQKV = 0
_W_GLR = 4096
_W_GR = 4112
_W_DQKV = 6160
_W_END = 12304


def _cparams(sem):
    return pltpu.CompilerParams(dimension_semantics=sem, vmem_limit_bytes=VMEM_LIMIT)


def _rmsnorm_kernel(x_ref, g_ref, o_ref):
    x = x_ref[...]
    ms = jnp.mean(x * x, axis=-1, keepdims=True)
    o_ref[...] = ((x * lax.rsqrt(ms + RMS_EPS)) * g_ref[...]).astype(o_ref.dtype)


def _rmsnorm(x, g, tm=256):
    m, d = x.shape
    return pl.pallas_call(
        _rmsnorm_kernel,
        grid=(m // tm,),
        in_specs=[pl.BlockSpec((tm, d), lambda i: (i, 0)),
                  pl.BlockSpec((1, d), lambda i: (0, 0))],
        out_specs=pl.BlockSpec((tm, d), lambda i: (i, 0)),
        out_shape=jax.ShapeDtypeStruct((m, d), BF16),
        compiler_params=_cparams(("parallel",)),
    )(x, g.reshape(1, d))


def _cast_pad_cols_kernel(w_ref, o_ref):
    n = w_ref.shape[1]
    o_ref[:, :n] = w_ref[...].astype(o_ref.dtype)
    o_ref[:, n:] = jnp.zeros((o_ref.shape[0], o_ref.shape[1] - n), o_ref.dtype)


def _cast_pad_cols(w, n_pad, tm=256):
    k, n = w.shape
    return pl.pallas_call(
        _cast_pad_cols_kernel,
        grid=(k // tm,),
        in_specs=[pl.BlockSpec((tm, n), lambda i: (i, 0))],
        out_specs=pl.BlockSpec((tm, n_pad), lambda i: (i, 0)),
        out_shape=jax.ShapeDtypeStruct((k, n_pad), BF16),
        compiler_params=_cparams(("parallel",)),
    )(w)


def _cast_pad_rows_kernel(w_ref, o_ref, *, n_valid):
    valid = pl.program_id(0) < n_valid
    o_ref[...] = jnp.where(valid, w_ref[...], 0.0).astype(o_ref.dtype)


def _cast_pad_rows(w, k_pad, tm=256):
    k, n = w.shape
    n_valid = k // tm
    return pl.pallas_call(
        functools.partial(_cast_pad_rows_kernel, n_valid=n_valid),
        grid=(k_pad // tm,),
        in_specs=[pl.BlockSpec((tm, n), lambda i: (jnp.minimum(i, n_valid - 1), 0))],
        out_specs=pl.BlockSpec((tm, n), lambda i: (i, 0)),
        out_shape=jax.ShapeDtypeStruct((k_pad, n), BF16),
        compiler_params=_cparams(("parallel",)),
    )(w)


def _mm_nt_kernel(x_ref, wt_ref, o_ref):
    o_ref[...] = lax.dot_general(x_ref[...], wt_ref[...], (((1,), (1,)), ((), ())),
                                 preferred_element_type=F32).astype(o_ref.dtype)


def _matmul_nt(x, wt, bm, bn, out_dtype, row0=0, n=None):
    m, k = x.shape
    n = wt.shape[0] if n is None else n
    return pl.pallas_call(
        _mm_nt_kernel,
        grid=(m // bm, n // bn),
        in_specs=[pl.BlockSpec((bm, k), lambda i, j: (i, 0)),
                  pl.BlockSpec((pl.Element(bn), pl.Element(k)),
                               lambda i, j: (pl.multiple_of(row0 + j * bn, ROW_ALIGN), 0))],
        out_specs=pl.BlockSpec((bm, bn), lambda i, j: (i, j)),
        out_shape=jax.ShapeDtypeStruct((m, n), out_dtype),
        compiler_params=_cparams(("parallel", "parallel")),
    )(x, wt)


def _mm_kernel(x_ref, w_ref, o_ref):
    o_ref[...] = jnp.dot(x_ref[...], w_ref[...],
                         preferred_element_type=F32).astype(o_ref.dtype)


def _matmul(x, w, bm, bn, out_dtype):
    m, k = x.shape
    n = w.shape[1]
    return pl.pallas_call(
        _mm_kernel,
        grid=(m // bm, n // bn),
        in_specs=[pl.BlockSpec((bm, k), lambda i, j: (i, 0)),
                  pl.BlockSpec((k, bn), lambda i, j: (0, j))],
        out_specs=pl.BlockSpec((bm, bn), lambda i, j: (i, j)),
        out_shape=jax.ShapeDtypeStruct((m, n), out_dtype),
        compiler_params=_cparams(("parallel", "parallel")),
    )(x, w)


GLA_T = 512
GLA_GROUP = 4
GLA_CUMSUM_ROWS = 256


def _split_bf16(x):
    hi = x.astype(BF16)
    lo = (x - hi.astype(F32)).astype(BF16)
    return hi, lo


def _gla_gate_kernel(glr_ref, w2_ref, b2_ref, tri_ref, b_ref):
    x_hi, x_lo = _split_bf16(glr_ref[...])
    w_hi, w_lo = _split_bf16(w2_ref[...])
    z = (jnp.dot(x_hi, w_hi, preferred_element_type=F32)
         + jnp.dot(x_lo, w_hi, preferred_element_type=F32)
         + jnp.dot(x_hi, w_lo, preferred_element_type=F32)) + b2_ref[...]
    log_sig = jnp.minimum(z, 0.0) - jnp.log(1.0 + jnp.exp(-jnp.abs(z)))
    g_hi, g_lo = _split_bf16(log_sig / GLA_TAU)
    tri = tri_ref[...]
    t_sub = tri.shape[0]
    for r0 in range(0, g_hi.shape[0], t_sub):
        rows = slice(r0, r0 + t_sub)
        b_ref[rows, :] = (jnp.dot(tri, g_hi[rows], preferred_element_type=F32)
                          + jnp.dot(tri, g_lo[rows], preferred_element_type=F32))


def _chunk_tril(t):
    idx = np.arange(t)
    same = (idx[:, None] // GLA_CHUNK) == (idx[None, :] // GLA_CHUNK)
    return jnp.asarray((same & (idx[:, None] >= idx[None, :])).astype(np.float32), BF16)


def _gla_gate(glr, w2_pad, b2, t=GLA_T):
    m = glr.shape[0]
    n = w2_pad.shape[1]
    return pl.pallas_call(
        _gla_gate_kernel,
        grid=(m // t,),
        in_specs=[pl.BlockSpec((t, LANES), lambda i: (i, 0)),
                  pl.BlockSpec((LANES, n), lambda i: (0, 0)),
                  pl.BlockSpec((1, n), lambda i: (0, 0)),
                  pl.BlockSpec((GLA_CUMSUM_ROWS, GLA_CUMSUM_ROWS), lambda i: (0, 0))],
        out_specs=pl.BlockSpec((t, n), lambda i: (i, 0)),
        out_shape=jax.ShapeDtypeStruct((m, n), F32),
        compiler_params=_cparams(("parallel",)),
    )(glr, w2_pad, b2, _chunk_tril(GLA_CUMSUM_ROWS))


def _gla_kernel(q_ref, k_ref, v_ref, r_ref, b_ref, gn_ref, o_ref, state_ref):
    c_sz = GLA_CHUNK
    g_sz = GLA_GROUP * c_sz
    assert GLA_GROUP == 4

    @pl.when(pl.program_id(2) == 0)
    def _():
        state_ref[...] = jnp.zeros_like(state_ref)

    row = lax.broadcasted_iota(jnp.int32, (g_sz, g_sz), 0)
    col = lax.broadcasted_iota(jnp.int32, (g_sz, g_sz), 1)
    causal = row >= col
    gn = gn_ref[...]
    nt = (((1,), (1,)), ((), ()))
    tn = (((0,), (0,)), ((), ()))

    def cat(parts):
        return jnp.concatenate(parts, axis=0)

    def group(gi, carry):
        rows = pl.ds(pl.multiple_of(gi * g_sz, g_sz), g_sz)
        b_all = b_ref[rows, :]
        q_all = q_ref[rows, :].astype(F32) * (GLA_DK ** -0.5)
        k_all = k_ref[rows, :].astype(F32)
        v = v_ref[rows, :]
        st = state_ref[...]

        sl = [slice(c * c_sz, (c + 1) * c_sz) for c in range(GLA_GROUP)]
        b = [b_all[s] for s in sl]
        bl = [x[c_sz - 1:c_sz, :] for x in b]
        bm = [x[c_sz // 2:c_sz // 2 + 1, :] for x in b]
        q_start = [q_all[s] * jnp.exp(x) for s, x in zip(sl, b)]
        k_end = [k_all[s] * jnp.exp(t - x) for s, x, t in zip(sl, b, bl)]
        q_mid = [(q_all[s] * jnp.exp(x - m)).astype(BF16) for s, x, m in zip(sl, b, bm)]
        k_mid = [(k_all[s] * jnp.exp(m - x)).astype(BF16) for s, x, m in zip(sl, b, bm)]

        e1, e2, e3 = jnp.exp(bl[1]), jnp.exp(bl[2]), jnp.exp(bl[3])
        e0 = jnp.exp(bl[0])
        d01, d12, d23 = e0 * e1, e1 * e2, e2 * e3
        d012, d123 = d01 * e2, d12 * e3
        d_all = d012 * e3

        qs = cat([q_start[0], q_start[1] * e0, q_start[2] * d01, q_start[3] * d012]).astype(BF16)
        o = lax.dot_general(qs, st.astype(BF16), nt, preferred_element_type=F32)

        k_end_b = [x.astype(BF16) for x in k_end]
        a_r0 = lax.dot_general(q_mid[0], cat([k_mid[0], k_mid[1]]), nt,
                               preferred_element_type=F32)
        a_r1 = lax.dot_general(q_mid[1], cat([(k_end[0] * jnp.exp(bm[1])).astype(BF16),
                                              k_mid[1]]), nt, preferred_element_type=F32)
        a_r2 = lax.dot_general(q_mid[2], cat([k_mid[2], k_mid[3]]), nt,
                               preferred_element_type=F32)
        a_r3 = lax.dot_general(q_mid[3], cat([(k_end[2] * jnp.exp(bm[3])).astype(BF16),
                                              k_mid[3]]), nt, preferred_element_type=F32)
        a_off = lax.dot_general(cat([q_start[2], q_start[3] * e2]).astype(BF16),
                                cat([(k_end[0] * e1).astype(BF16), k_end_b[1]]), nt,
                                preferred_element_type=F32)
        zeros = jnp.zeros((2 * c_sz, 2 * c_sz), F32)
        att = jnp.concatenate([cat([a_r0, a_r1, a_off]), cat([zeros, a_r2, a_r3])], axis=1)
        att = jnp.where(causal, att, 0.0).astype(BF16)
        o = o + jnp.dot(att, v, preferred_element_type=F32)

        k_fin = cat([k_end[0] * d123, k_end[1] * d23, k_end[2] * e3, k_end[3]]).astype(BF16)
        kv_t = lax.dot_general(v, k_fin, tn, preferred_element_type=F32)
        state_new = st * d_all + kv_t

        ms = jnp.mean(o * o, axis=-1, keepdims=True)
        on = (o * lax.rsqrt(ms + RMS_EPS)) * gn
        r = r_ref[rows, :].astype(F32)
        gate = r / (1.0 + jnp.exp(-r))
        o_ref[rows, :] = (on * gate).astype(o_ref.dtype)
        state_ref[...] = state_new
        return carry

    lax.fori_loop(0, GLA_T // g_sz, group, 0, unroll=True)


def _gla(qkv, gr, b_cum, gn, batch, seq):
    t = GLA_T
    nt_ = seq // t
    m = batch * seq
    kb = (GLA_HEADS * GLA_DK) // GLA_DK
    vb = (2 * GLA_HEADS * GLA_DK) // GLA_DV
    return pl.pallas_call(
        _gla_kernel,
        grid=(batch, GLA_HEADS, nt_),
        in_specs=[
            pl.BlockSpec((t, GLA_DK), lambda b, h, s: (b * nt_ + s, h)),
            pl.BlockSpec((t, GLA_DK), lambda b, h, s: (b * nt_ + s, kb + h)),
            pl.BlockSpec((t, GLA_DV), lambda b, h, s: (b * nt_ + s, vb + h)),
            pl.BlockSpec((t, GLA_DV), lambda b, h, s: (b * nt_ + s, h)),
            pl.BlockSpec((t, GLA_DK), lambda b, h, s: (b * nt_ + s, h)),
            pl.BlockSpec((1, GLA_DV), lambda b, h, s: (0, 0)),
        ],
        out_specs=pl.BlockSpec((t, GLA_DV), lambda b, h, s: (b * nt_ + s, h)),
        out_shape=jax.ShapeDtypeStruct((m, GLA_HEADS * GLA_DV), BF16),
        scratch_shapes=[pltpu.VMEM((GLA_DV, GLA_DK), F32)],
        compiler_params=_cparams(("parallel", "parallel", "arbitrary")),
    )(qkv, qkv, qkv, gr, b_cum, gn)


def _t5_bucket_np(dist):
    max_exact = REL_BUCKETS // 2
    d_f = np.maximum(dist, 1).astype(np.float32)
    large = max_exact + (np.log(d_f / np.float32(max_exact))
                         / np.float32(math.log(REL_MAX_DIST / max_exact))
                         * np.float32(REL_BUCKETS - max_exact)).astype(np.int32)
    large = np.minimum(large, REL_BUCKETS - 1)
    return np.where(dist < max_exact, dist, large)


def _dil_bucket_index():
    steps = DIL_STEPS
    qi = np.arange(steps)[:, None]
    kj = np.arange(2 * steps)[None, :]
    rel = qi + steps - kj
    band = (rel >= 0) & (rel <= steps)
    out = []
    for _, dil in DIL_CONFIGS:
        bucket = _t5_bucket_np(np.clip(rel, 0, steps) * dil)
        out.append(np.where(band, bucket, REL_BUCKETS))
    return np.stack(out).astype(np.int32)


def _dil_bias_kernel(relb_ref, idx_ref, o_ref):
    head = pl.program_id(1)
    idx = idx_ref[0]
    bias = jnp.full(idx.shape, NEG_INF, F32)
    for bkt in range(REL_BUCKETS):
        bias = jnp.where(idx == bkt, relb_ref[bkt, head] * LOG2E, bias)
    o_ref[0, 0] = bias


def _dil_bias(rel_bias):
    steps = DIL_STEPS
    ncfg = len(DIL_CONFIGS)
    idx = jnp.asarray(_dil_bucket_index())
    return pl.pallas_call(
        _dil_bias_kernel,
        grid=(ncfg, DIL_HEADS),
        in_specs=[pl.BlockSpec(memory_space=pltpu.SMEM),
                  pl.BlockSpec((1, steps, 2 * steps), lambda c, h: (c, 0, 0))],
        out_specs=pl.BlockSpec((1, 1, steps, 2 * steps), lambda c, h: (c, h, 0, 0)),
        out_shape=jax.ShapeDtypeStruct((ncfg, DIL_HEADS, steps, 2 * steps), F32),
        compiler_params=_cparams(("parallel", "parallel")),
    )(rel_bias, idx)


DIL_GROUP = 8
DIL_SPLIT = 4
DIL_COPY_ROWS = 256


def _dil_kernel(bias_ref, q_ref, k_ref, v_ref, o_ref, m_sc, l_sc, acc_sc, qd, kd, vd, *, seq):
    steps = DIL_STEPS
    e = DIL_HEAD_DIM
    scale = e ** -0.5 * LOG2E
    nt = (((1,), (1,)), ((), ()))
    n_cfg = len(DIL_CONFIGS)
    split = DIL_SPLIT

    def rows_of(start, size, stride):
        if stride == 1:
            return pl.ds(start, size)
        return pl.ds(start, size, stride=stride)

    for src, dst in ((q_ref, qd), (k_ref, kd), (v_ref, vd)):
        for r in range(split):
            for c0 in range(0, seq // split, DIL_COPY_ROWS):
                dst[r, c0:c0 + DIL_COPY_ROWS, :] = src[
                    0, pl.ds(r + split * c0, DIL_COPY_ROWS, stride=split), :]

    def load(nat_ref, split_ref, dil, r, n, span, size):
        if dil % split:
            return nat_ref[0, rows_of(n * span + r, size, dil), :].astype(BF16)
        rows = rows_of(n * (span // split) + r // split, size, dil // split)
        return split_ref[r % split, rows, :].astype(BF16)

    def attend(c, dil, blocks, merge):
        span = steps * dil

        def q_start(r, n_q):
            start = n_q * span + r
            if dil == 1 and not isinstance(start, int):
                start = pl.multiple_of(start, span)
            return start

        q_rows = [rows_of(q_start(r, n_q), steps, dil) for r, n_q, _, _, _ in blocks]
        logits = []
        for r, n_q, n_k, n_keys, bias in blocks:
            q = load(q_ref, qd, dil, r, n_q, span, steps)
            k = load(k_ref, kd, dil, r, n_k, span, n_keys)
            logits.append(lax.dot_general(q, k, nt, preferred_element_type=F32) * scale + bias)
        stats = []
        for s in logits:
            m = jnp.max(s, axis=-1, keepdims=True)
            p = jnp.exp2(s - m)
            stats.append((m, jnp.sum(p, axis=-1, keepdims=True), p.astype(BF16)))
        pvs = [jnp.dot(p, load(v_ref, vd, dil, r, n_k, span, n_keys), preferred_element_type=F32)
               for (_, _, p), (r, _, n_k, n_keys, _) in zip(stats, blocks)]
        if not merge:
            for qr, (m, l, _), pv in zip(q_rows, stats, pvs):
                m_sc[c - 1, qr, :] = jnp.broadcast_to(m, (steps, e))
                l_sc[c - 1, qr, :] = jnp.broadcast_to(l, (steps, e))
                acc_sc[c - 1, qr, :] = pv
            return
        outs = []
        for qr, (m, l, _), pv in zip(q_rows, stats, pvs):
            ms = [m] + [m_sc[i, qr, :] for i in range(n_cfg - 1)]
            ls = [l] + [l_sc[i, qr, :] for i in range(n_cfg - 1)]
            accs = [pv] + [acc_sc[i, qr, :] for i in range(n_cfg - 1)]
            m_max = functools.reduce(jnp.maximum, ms)
            num = None
            den = None
            for m_i, l_i, acc_i in zip(ms, ls, accs):
                wgt = jnp.exp2(m_i - m_max)
                num = wgt * acc_i if num is None else num + wgt * acc_i
                den = wgt * l_i if den is None else den + wgt * l_i
            outs.append((qr, (num / den).astype(o_ref.dtype)))
        for qr, o in outs:
            o_ref[0, qr, :] = o

    for c in reversed(range(n_cfg)):
        dil = DIL_CONFIGS[c][1]
        nb = seq // dil // steps

        def first(r, c=c):
            return (r, 0, 0, steps, bias_ref[c, 0, :, steps:2 * steps])

        def later(j, c=c, dil=dil, nb=nb):
            if dil == 1:
                return (0, 1 + j, j, 2 * steps, bias_ref[c, 0])
            r = j // (nb - 1)
            n = 1 + j % (nb - 1)
            return (r, n, n - 1, 2 * steps, bias_ref[c, 0])

        for make, count in ((first, dil), (later, dil * (nb - 1))):
            full, rest = divmod(count, DIL_GROUP)

            def group(gi, carry, c=c, dil=dil, make=make):
                attend(c, dil, [make(gi * DIL_GROUP + u) for u in range(DIL_GROUP)], c == 0)
                return carry

            if full:
                lax.fori_loop(0, full, group, 0)
            if rest:
                attend(c, dil, [make(full * DIL_GROUP + u) for u in range(rest)], c == 0)


def _dilated(qkv3, bias, batch, seq):
    e = DIL_HEAD_DIM
    steps = DIL_STEPS
    ncfg = len(DIL_CONFIGS)
    assert DIL_CONFIGS[0][1] == 1, "the config that writes the output rows must be undilated"
    return pl.pallas_call(
        functools.partial(_dil_kernel, seq=seq),
        grid=(batch, DIL_HEADS),
        in_specs=[
            pl.BlockSpec((ncfg, 1, steps, 2 * steps), lambda b, h: (0, h, 0, 0)),
            pl.BlockSpec((1, seq, e), lambda b, h: (b, 0, h)),
            pl.BlockSpec((1, seq, e), lambda b, h: (b, 0, DIL_HEADS + h)),
            pl.BlockSpec((1, seq, e), lambda b, h: (b, 0, 2 * DIL_HEADS + h)),
        ],
        out_specs=pl.BlockSpec((1, seq, e), lambda b, h: (b, 0, h)),
        out_shape=jax.ShapeDtypeStruct((batch, seq, DIL_HEADS * e), BF16),
        scratch_shapes=[pltpu.VMEM((ncfg - 1, seq, e), F32)] * 3
        + [pltpu.VMEM((DIL_SPLIT, seq // DIL_SPLIT, e), F32)] * 3,
        compiler_params=_cparams(("parallel", "parallel")),
    )(bias, qkv3, qkv3, qkv3)


def _mix_out_kernel(a_ref, b_ref, wa_ref, wb_ref, x_ref, o_ref):
    acc = jnp.dot(a_ref[...], wa_ref[...], preferred_element_type=F32)
    acc = acc + jnp.dot(b_ref[...], wb_ref[...], preferred_element_type=F32)
    o_ref[...] = x_ref[...] + acc


def _mix_out(o_gla, o_dil, w, x, bm=1024, bn=1024):
    m, ka = o_gla.shape
    kb = o_dil.shape[1]
    n = w.shape[1]
    assert ka == kb and w.shape[0] == ka + kb
    return pl.pallas_call(
        _mix_out_kernel,
        grid=(m // bm, n // bn),
        in_specs=[pl.BlockSpec((bm, ka), lambda i, j: (i, 0)),
                  pl.BlockSpec((bm, kb), lambda i, j: (i, 0)),
                  pl.BlockSpec((ka, bn), lambda i, j: (0, j)),
                  pl.BlockSpec((kb, bn), lambda i, j: (1, j)),
                  pl.BlockSpec((bm, bn), lambda i, j: (i, j))],
        out_specs=pl.BlockSpec((bm, bn), lambda i, j: (i, j)),
        out_shape=jax.ShapeDtypeStruct((m, n), F32),
        compiler_params=_cparams(("parallel", "parallel")),
    )(o_gla, o_dil, w, w, x)


XATTN_SLAB = 128


def _xattn_kernel(h_ref, gx_ref, wq_ref, k_ref, v_ref, wo_ref, gf_ref, h2_ref, hn_ref):
    e = XATTN_HEAD_DIM
    nt = (((1,), (1,)), ((), ()))
    gx = gx_ref[...]
    gf = gf_ref[...]
    slabs = [slice(r0, r0 + XATTN_SLAB) for r0 in range(0, h_ref.shape[0], XATTN_SLAB)]
    hs = [h_ref[rows, :] for rows in slabs]
    hns = []
    for h in hs:
        ms = jnp.mean(h * h, axis=-1, keepdims=True)
        hns.append(((h * lax.rsqrt(ms + RMS_EPS)) * gx).astype(BF16))
    qs = [jnp.dot(hn, wq_ref[...], preferred_element_type=F32).astype(BF16) for hn in hns]
    os_ = []
    for q in qs:
        outs = []
        for hh in range(XATTN_HEADS):
            qh = q[:, hh * e:(hh + 1) * e]
            kh = k_ref[0, :, hh * e:(hh + 1) * e]
            vh = v_ref[0, :, hh * e:(hh + 1) * e]
            s = lax.dot_general(qh, kh, nt, preferred_element_type=F32) * (e ** -0.5)
            m = jnp.max(s, axis=-1, keepdims=True)
            p = jnp.exp(s - m)
            p = p / jnp.sum(p, axis=-1, keepdims=True)
            outs.append(jnp.dot(p.astype(BF16), vh, preferred_element_type=F32))
        os_.append(jnp.concatenate(outs, axis=1).astype(BF16))
    h2s = [h + jnp.dot(o, wo_ref[...], preferred_element_type=F32) for h, o in zip(hs, os_)]
    hn2s = []
    for h2 in h2s:
        ms2 = jnp.mean(h2 * h2, axis=-1, keepdims=True)
        hn2s.append(((h2 * lax.rsqrt(ms2 + RMS_EPS)) * gf).astype(hn_ref.dtype))
    for rows, h2, hn2 in zip(slabs, h2s, hn2s):
        h2_ref[rows, :] = h2
        hn_ref[rows, :] = hn2


def _xattn(h1, gx, wq, kx, vx, wo, gf, seq, tm=512):
    m, d = h1.shape
    mem_len = kx.shape[1]
    per_seq = seq // tm
    return pl.pallas_call(
        _xattn_kernel,
        grid=(m // tm,),
        in_specs=[pl.BlockSpec((tm, d), lambda i: (i, 0)),
                  pl.BlockSpec((1, d), lambda i: (0, 0)),
                  pl.BlockSpec((d, XATTN_WIDTH), lambda i: (0, 0),
                               pipeline_mode=pl.Buffered(1)),
                  pl.BlockSpec((1, mem_len, XATTN_WIDTH), lambda i: (i // per_seq, 0, 0)),
                  pl.BlockSpec((1, mem_len, XATTN_WIDTH), lambda i: (i // per_seq, 0, 0)),
                  pl.BlockSpec((XATTN_WIDTH, d), lambda i: (0, 0),
                               pipeline_mode=pl.Buffered(1)),
                  pl.BlockSpec((1, d), lambda i: (0, 0))],
        out_specs=[pl.BlockSpec((tm, d), lambda i: (i, 0)),
                   pl.BlockSpec((tm, d), lambda i: (i, 0))],
        out_shape=[jax.ShapeDtypeStruct((m, d), F32),
                   jax.ShapeDtypeStruct((m, d), BF16)],
        compiler_params=_cparams(("parallel",)),
    )(h1, gx.reshape(1, d), wq, kx, vx, wo, gf.reshape(1, d))


def _ffn_in_kernel(x_ref, wg_ref, wu_ref, halo_ref, cw_ref, cb_ref, a_ref):
    x = x_ref[...]
    g = jnp.dot(x, wg_ref[...], preferred_element_type=F32)
    u = jnp.dot(x, wu_ref[...], preferred_element_type=F32)
    halo = halo_ref[0]
    prev1 = halo[SUBLANES - 1:SUBLANES, :]
    prev2 = halo[SUBLANES - 2:SUBLANES - 1, :]
    row = lax.broadcasted_iota(jnp.int32, (SUBLANES, g.shape[1]), 0)
    r1 = pltpu.roll(g, 1, axis=0)
    r2 = pltpu.roll(g, 2, axis=0)
    head1 = jnp.where(row == 0, prev1, r1[:SUBLANES])
    head2 = jnp.where(row == 0, prev2, jnp.where(row == 1, prev1, r2[:SUBLANES]))
    g_m1 = jnp.concatenate([head1, r1[SUBLANES:]], axis=0)
    g_m2 = jnp.concatenate([head2, r2[SUBLANES:]], axis=0)
    cw = cw_ref[...]
    y = cb_ref[...] + g_m2 * cw[0:1, :]
    y = y + g_m1 * cw[1:2, :]
    y = y + g * cw[2:3, :]
    act = ((y / (1.0 + jnp.exp(-y))) * u).astype(a_ref.dtype)
    n_act = act.shape[1]
    a_ref[:, :n_act] = act
    if a_ref.shape[1] > n_act:
        a_ref[:, n_act:] = jnp.zeros((a_ref.shape[0], a_ref.shape[1] - n_act), a_ref.dtype)


def _ffn_in_main_kernel(x_ref, wg_ref, wu_ref, halo_ref, cw_ref, cb_ref, a_ref, *, n_full):
    @pl.when(pl.program_id(1) < n_full)
    def _():
        _ffn_in_kernel(x_ref, wg_ref, wu_ref, halo_ref, cw_ref, cb_ref, a_ref)

    @pl.when(pl.program_id(1) == n_full)
    def _():
        a_ref[...] = jnp.zeros_like(a_ref)


def _ffn_in_tail_kernel(x_ref, wg_ref, wu_ref, halo_ref, cw_ref, cb_ref, act_hbm, a_ref):
    del act_hbm
    _ffn_in_kernel(x_ref, wg_ref, wu_ref, halo_ref, cw_ref, cb_ref, a_ref)


def _ffn_in(x, wg, wu, halo_g, cw, cb, d_ff=D_FF, bm=FFN_BM, bn=512):
    m, k = x.shape
    n_pad = wg.shape[1]
    full = d_ff // bn
    tail = d_ff - full * bn
    assert n_pad == (full + 1) * bn and tail % LANES == 0 and (full * bn) % tail == 0
    last = full - 1
    act = pl.pallas_call(
        functools.partial(_ffn_in_main_kernel, n_full=full),
        grid=(m // bm, full + 1),
        in_specs=[pl.BlockSpec((bm, k), lambda i, j: (i, 0)),
                  pl.BlockSpec((k, bn), lambda i, j: (0, jnp.minimum(j, last))),
                  pl.BlockSpec((k, bn), lambda i, j: (0, jnp.minimum(j, last))),
                  pl.BlockSpec((1, SUBLANES, bn), lambda i, j: (i, 0, jnp.minimum(j, last))),
                  pl.BlockSpec((CONV_WIDTH, bn), lambda i, j: (0, jnp.minimum(j, last))),
                  pl.BlockSpec((1, bn), lambda i, j: (0, jnp.minimum(j, last)))],
        out_specs=pl.BlockSpec((bm, bn), lambda i, j: (i, j)),
        out_shape=jax.ShapeDtypeStruct((m, n_pad), BF16),
        compiler_params=_cparams(("parallel", "parallel")),
    )(x, wg, wu, halo_g, cw, cb)
    t0 = full * bn // tail
    return pl.pallas_call(
        _ffn_in_tail_kernel,
        grid=(m // bm,),
        in_specs=[pl.BlockSpec((bm, k), lambda i: (i, 0)),
                  pl.BlockSpec((k, tail), lambda i: (0, t0)),
                  pl.BlockSpec((k, tail), lambda i: (0, t0)),
                  pl.BlockSpec((1, SUBLANES, tail), lambda i: (i, 0, t0)),
                  pl.BlockSpec((CONV_WIDTH, tail), lambda i: (0, t0)),
                  pl.BlockSpec((1, tail), lambda i: (0, t0)),
                  pl.BlockSpec(memory_space=pl.ANY)],
        out_specs=pl.BlockSpec((bm, bn), lambda i: (i, full)),
        out_shape=jax.ShapeDtypeStruct((m, n_pad), BF16),
        input_output_aliases={6: 0},
        compiler_params=_cparams(("parallel",)),
    )(x, wg, wu, halo_g, cw, cb, act)


def _ffn_halo_rows(hn, batch, seq, bm=FFN_BM):
    d = hn.shape[1]
    tiles = seq // bm
    tail = hn.reshape(batch, tiles, bm, d)[:, :, bm - SUBLANES:, :]
    prev = jnp.concatenate([jnp.zeros_like(tail[:, :1]), tail[:, :-1]], axis=1)
    return prev.reshape(batch * tiles * SUBLANES, d)


FFN_OUT_COLS = 1024
FFN_RES_COLS = 512
FFN_OUT_VMEM_LIMIT = 63 * 1024 * 1024
FFN_NORM_ROWS = 256


def _ffn_out_kernel(a_ref, wd_ref, h_ref, fg_ref, o_ref, *, n_res):
    kk = pl.program_id(1)
    d = o_ref.shape[1]

    @pl.when(kk == 0)
    def _():
        o_ref[...] = jnp.zeros_like(o_ref)

    a = a_ref[...]
    for c0 in range(0, d, FFN_OUT_COLS):
        cols = slice(c0, c0 + FFN_OUT_COLS)
        o_ref[:, cols] += jnp.dot(a, wd_ref[:, cols], preferred_element_type=F32)

    for c in range(n_res):
        @pl.when(kk == c)
        def _(c=c):
            cols = slice(c * FFN_RES_COLS, (c + 1) * FFN_RES_COLS)
            o_ref[:, cols] += h_ref[...]

    @pl.when(kk == pl.num_programs(1) - 1)
    def _():
        fg = fg_ref[...]
        for r0 in range(0, o_ref.shape[0], FFN_NORM_ROWS):
            rows = slice(r0, r0 + FFN_NORM_ROWS)
            h3 = o_ref[rows, :]
            ms = jnp.mean(h3 * h3, axis=-1, keepdims=True)
            o_ref[rows, :] = (h3 * lax.rsqrt(ms + RMS_EPS)) * fg


def _ffn_out(a, wd, h2, fg, tm=1024, tk=1024):
    m, ff = a.shape
    d = wd.shape[1]
    n_res = d // FFN_RES_COLS
    assert ff // tk >= n_res, "one residual slab per contraction step"
    return pl.pallas_call(
        functools.partial(_ffn_out_kernel, n_res=n_res),
        grid=(m // tm, ff // tk),
        in_specs=[pl.BlockSpec((tm, tk), lambda i, k: (i, k)),
                  pl.BlockSpec((tk, d), lambda i, k: (k, 0)),
                  pl.BlockSpec((tm, FFN_RES_COLS), lambda i, k: (i, jnp.minimum(k, n_res - 1))),
                  pl.BlockSpec((1, d), lambda i, k: (0, 0))],
        out_specs=pl.BlockSpec((tm, d), lambda i, k: (i, 0)),
        out_shape=jax.ShapeDtypeStruct((m, d), F32),
        compiler_params=pltpu.CompilerParams(dimension_semantics=("parallel", "arbitrary"),
                                             vmem_limit_bytes=FFN_OUT_VMEM_LIMIT),
    )(a, wd, h2, fg.reshape(1, d))


def _layer(x2, mem2, rel_bias, batch, seq, mem_len, norm_mix_g, w_in, gla_w_gate2,
           gla_b_gate, gla_norm_g, w_out, norm_xattn_g, mem_norm_g, w_xq, w_xk, w_xv,
           w_xo, norm_ffn_g, w_ffn_gate, w_ffn_up, ffn_conv_w, ffn_conv_b, w_ffn_down,
           out_g):
    w_in_t = w_in.T.astype(BF16)
    w_glr = jnp.pad(w_in_t[_W_GLR:_W_GR], ((0, LANES - GLA_LOWRANK), (0, 0)))
    w2_pad = jnp.pad(gla_w_gate2, ((0, LANES - GLA_LOWRANK), (0, 0)))
    ff_pad = D_FF_PAD - D_FF
    wg = _cast_pad_cols(w_ffn_gate, D_FF_PAD)
    wu = _cast_pad_cols(w_ffn_up, D_FF_PAD)
    wd = _cast_pad_rows(w_ffn_down, D_FF_PAD)
    cw = jnp.pad(ffn_conv_w, ((0, 0), (0, ff_pad)))
    cb = jnp.pad(ffn_conv_b, ((0, ff_pad),)).reshape(1, D_FF_PAD)
    w_out_b = w_out.astype(BF16)
    w_kv = jnp.concatenate([w_xk, w_xv], axis=1).astype(BF16)

    hn = _rmsnorm(x2, norm_mix_g)
    d_qkv = _matmul_nt(hn, w_in_t, 1024, 1024, F32, row0=_W_DQKV, n=_W_END - _W_DQKV)
    g_qkv = _matmul_nt(hn, w_in_t, 1024, 1024, BF16, row0=_W_GQKV, n=_W_GLR - _W_GQKV)
    g_r = _matmul_nt(hn, w_in_t, 1024, 1024, BF16, row0=_W_GR, n=_W_DQKV - _W_GR)
    glr = _matmul_nt(hn, w_glr, 1024, LANES, F32)
    b_cum = _gla_gate(glr, w2_pad, gla_b_gate.reshape(1, -1))
    o_gla = _gla(g_qkv, g_r, b_cum, gla_norm_g.reshape(1, -1), batch, seq)
    o_dil = _dilated(d_qkv.reshape(batch, seq, 3 * DIL_HEADS * DIL_HEAD_DIM),
                     _dil_bias(rel_bias), batch, seq)
    o_dil = o_dil.reshape(batch * seq, DIL_HEADS * DIL_HEAD_DIM)
    h1 = _mix_out(o_gla, o_dil, w_out_b, x2)

    memn = _rmsnorm(mem2, mem_norm_g)
    kv = _matmul(memn, w_kv, 512, 512, BF16)
    kx = kv[:, :XATTN_WIDTH].reshape(batch, mem_len, XATTN_WIDTH)
    vx = kv[:, XATTN_WIDTH:].reshape(batch, mem_len, XATTN_WIDTH)
    h2, hn3 = _xattn(h1, norm_xattn_g, w_xq.astype(BF16), kx, vx, w_xo.astype(BF16),
                     norm_ffn_g, seq)

    halo_x = _ffn_halo_rows(hn3, batch, seq)
    halo_g = _matmul(halo_x, wg, halo_x.shape[0], 512, F32)
    halo_g = halo_g.reshape(-1, SUBLANES, D_FF_PAD)
    act = _ffn_in(hn3, wg, wu, halo_g, cw, cb)
    return _ffn_out(act, wd, h2, out_g)


def kernel(x, mem, rel_bias, norm_mix_g, w_in, gla_w_gate2, gla_b_gate, gla_norm_g, w_out,
           norm_xattn_g, mem_norm_g, w_xq, w_xk, w_xv, w_xo, norm_ffn_g, w_ffn_gate,
           w_ffn_up, ffn_conv_w, ffn_conv_b, w_ffn_down, final_norm_g):
    batch, seq, d = x.shape
    mem_len = mem.shape[1]
    depth = w_in.shape[0]
    assert depth == 1, "the fused final rmsnorm assumes a single layer"
    out = _layer(x.reshape(batch * seq, d), mem.reshape(batch * mem_len, d), rel_bias,
                 batch, seq, mem_len, norm_mix_g[0], w_in[0], gla_w_gate2[0],
                 gla_b_gate[0], gla_norm_g[0], w_out[0], norm_xattn_g[0], mem_norm_g[0],
                 w_xq[0], w_xk[0], w_xv[0], w_xo[0], norm_ffn_g[0], w_ffn_gate[0],
                 w_ffn_up[0], ffn_conv_w[0], ffn_conv_b[0], w_ffn_down[0], final_norm_g)
    return out.reshape(batch, seq, d)
```

```python
import functools
import math

import numpy as np
import jax
import jax.numpy as jnp
from jax import lax
from jax.experimental import pallas as pl
from jax.experimental.pallas import tpu as pltpu

F32 = jnp.float32
BF16 = jnp.bfloat16

D_MODEL = 4096
RMS_EPS = 1e-6
GLA_HEADS = 4
GLA_DV = 512
GLA_DK = 256
GLA_LOWRANK = 16
GLA_TAU = 16.0
GLA_CHUNK = 64
DIL_HEAD_DIM = 128
DIL_HEADS = 16
DIL_CONFIGS = ((128, 1), (512, 4), (2048, 16))
DIL_STEPS = 128
REL_BUCKETS = 32
REL_MAX_DIST = 2048
XATTN_HEADS = 4
XATTN_HEAD_DIM = 128
XATTN_WIDTH = 512
D_FF = 11008
CONV_WIDTH = 3
NEG_INF = -1e30
LOG2E = math.log2(math.e)

LANES = 128
SUBLANES = 8
ROW_ALIGN = 16
VMEM_LIMIT = 56 * 1024 * 1024

D_FF_PAD = 11264
FFN_BM = 1024

_W_GQKV = 0
_W_GLR = 4096
_W_GR = 4112
_W_DQKV = 6160
_W_END = 12304


def _cparams(sem):
    return pltpu.CompilerParams(dimension_semantics=sem, vmem_limit_bytes=VMEM_LIMIT)


def _rmsnorm_kernel(x_ref, g_ref, o_ref):
    x = x_ref[...]
    ms = jnp.mean(x * x, axis=-1, keepdims=True)
    o_ref[...] = ((x * lax.rsqrt(ms + RMS_EPS)) * g_ref[...]).astype(o_ref.dtype)


def _rmsnorm(x, g, tm=256):
    m, d = x.shape
    return pl.pallas_call(
        _rmsnorm_kernel,
        grid=(m // tm,),
        in_specs=[pl.BlockSpec((tm, d), lambda i: (i, 0)),
                  pl.BlockSpec((1, d), lambda i: (0, 0))],
        out_specs=pl.BlockSpec((tm, d), lambda i: (i, 0)),
        out_shape=jax.ShapeDtypeStruct((m, d), BF16),
        compiler_params=_cparams(("parallel",)),
    )(x, g.reshape(1, d))


def _cast_pad_cols_kernel(w_ref, o_ref):
    n = w_ref.shape[1]
    o_ref[:, :n] = w_ref[...].astype(o_ref.dtype)
    o_ref[:, n:] = jnp.zeros((o_ref.shape[0], o_ref.shape[1] - n), o_ref.dtype)


def _cast_pad_cols(w, n_pad, tm=256):
    k, n = w.shape
    return pl.pallas_call(
        _cast_pad_cols_kernel,
        grid=(k // tm,),
        in_specs=[pl.BlockSpec((tm, n), lambda i: (i, 0))],
        out_specs=pl.BlockSpec((tm, n_pad), lambda i: (i, 0)),
        out_shape=jax.ShapeDtypeStruct((k, n_pad), BF16),
        compiler_params=_cparams(("parallel",)),
    )(w)


def _cast_pad_rows_kernel(w_ref, o_ref, *, n_valid):
    valid = pl.program_id(0) < n_valid
    o_ref[...] = jnp.where(valid, w_ref[...], 0.0).astype(o_ref.dtype)


def _cast_pad_rows(w, k_pad, tm=256):
    k, n = w.shape
    n_valid = k // tm
    return pl.pallas_call(
        functools.partial(_cast_pad_rows_kernel, n_valid=n_valid),
        grid=(k_pad // tm,),
        in_specs=[pl.BlockSpec((tm, n), lambda i: (jnp.minimum(i, n_valid - 1), 0))],
        out_specs=pl.BlockSpec((tm, n), lambda i: (i, 0)),
        out_shape=jax.ShapeDtypeStruct((k_pad, n), BF16),
        compiler_params=_cparams(("parallel",)),
    )(w)


def _mm_nt_kernel(x_ref, wt_ref, o_ref):
    o_ref[...] = lax.dot_general(x_ref[...], wt_ref[...], (((1,), (1,)), ((), ())),
                                 preferred_element_type=F32).astype(o_ref.dtype)


def _matmul_nt(x, wt, bm, bn, out_dtype, row0=0, n=None):
    m, k = x.shape
    n = wt.shape[0] if n is None else n
    return pl.pallas_call(
        _mm_nt_kernel,
        grid=(m // bm, n // bn),
        in_specs=[pl.BlockSpec((bm, k), lambda i, j: (i, 0)),
                  pl.BlockSpec((pl.Element(bn), pl.Element(k)),
                               lambda i, j: (pl.multiple_of(row0 + j * bn, ROW_ALIGN), 0))],
        out_specs=pl.BlockSpec((bm, bn), lambda i, j: (i, j)),
        out_shape=jax.ShapeDtypeStruct((m, n), out_dtype),
        compiler_params=_cparams(("parallel", "parallel")),
    )(x, wt)


def _mm_kernel(x_ref, w_ref, o_ref):
    o_ref[...] = jnp.dot(x_ref[...], w_ref[...],
                         preferred_element_type=F32).astype(o_ref.dtype)


def _matmul(x, w, bm, bn, out_dtype):
    m, k = x.shape
    n = w.shape[1]
    return pl.pallas_call(
        _mm_kernel,
        grid=(m // bm, n // bn),
        in_specs=[pl.BlockSpec((bm, k), lambda i, j: (i, 0)),
                  pl.BlockSpec((k, bn), lambda i, j: (0, j))],
        out_specs=pl.BlockSpec((bm, bn), lambda i, j: (i, j)),
        out_shape=jax.ShapeDtypeStruct((m, n), out_dtype),
        compiler_params=_cparams(("parallel", "parallel")),
    )(x, w)


GLA_T = 1024
GLA_GROUP = 4
GLA_CUMSUM_ROWS = 256


def _split_bf16(x):
    hi = x.astype(BF16)
    lo = (x - hi.astype(F32)).astype(BF16)
    return hi, lo


def _gla_gate_kernel(glr_ref, w2_ref, b2_ref, tri_ref, b_ref):
    x_hi, x_lo = _split_bf16(glr_ref[...])
    w_hi, w_lo = _split_bf16(w2_ref[...])
    z = (jnp.dot(x_hi, w_hi, preferred_element_type=F32)
         + jnp.dot(x_lo, w_hi, preferred_element_type=F32)
         + jnp.dot(x_hi, w_lo, preferred_element_type=F32)) + b2_ref[...]
    log_sig = jnp.minimum(z, 0.0) - jnp.log(1.0 + jnp.exp(-jnp.abs(z)))
    g_hi, g_lo = _split_bf16(log_sig / GLA_TAU)
    tri = tri_ref[...]
    t_sub = tri.shape[0]
    for r0 in range(0, g_hi.shape[0], t_sub):
        rows = slice(r0, r0 + t_sub)
        b_ref[rows, :] = (jnp.dot(tri, g_hi[rows], preferred_element_type=F32)
                          + jnp.dot(tri, g_lo[rows], preferred_element_type=F32))


def _chunk_tril(t):
    idx = np.arange(t)
    same = (idx[:, None] // GLA_CHUNK) == (idx[None, :] // GLA_CHUNK)
    return jnp.asarray((same & (idx[:, None] >= idx[None, :])).astype(np.float32), BF16)


def _gla_gate(glr, w2_pad, b2, t=GLA_T):
    m = glr.shape[0]
    n = w2_pad.shape[1]
    return pl.pallas_call(
        _gla_gate_kernel,
        grid=(m // t,),
        in_specs=[pl.BlockSpec((t, LANES), lambda i: (i, 0)),
                  pl.BlockSpec((LANES, n), lambda i: (0, 0)),
                  pl.BlockSpec((1, n), lambda i: (0, 0)),
                  pl.BlockSpec((GLA_CUMSUM_ROWS, GLA_CUMSUM_ROWS), lambda i: (0, 0))],
        out_specs=pl.BlockSpec((t, n), lambda i: (i, 0)),
        out_shape=jax.ShapeDtypeStruct((m, n), F32),
        compiler_params=_cparams(("parallel",)),
    )(glr, w2_pad, b2, _chunk_tril(GLA_CUMSUM_ROWS))


def _gla_kernel(q_ref, k_ref, v_ref, r_ref, b_ref, gn_ref, o_ref, state_ref):
    c_sz = GLA_CHUNK
    g_sz = GLA_GROUP * c_sz
    assert GLA_GROUP == 4

    @pl.when(pl.program_id(2) == 0)
    def _():
        state_ref[...] = jnp.zeros_like(state_ref)

    row = lax.broadcasted_iota(jnp.int32, (g_sz, g_sz), 0)
    col = lax.broadcasted_iota(jnp.int32, (g_sz, g_sz), 1)
    causal = row >= col
    gn = gn_ref[...]
    nt = (((1,), (1,)), ((), ()))
    tn = (((0,), (0,)), ((), ()))

    def cat(parts):
        return jnp.concatenate(parts, axis=0)

    def group(gi, carry):
        rows = pl.ds(pl.multiple_of(gi * g_sz, g_sz), g_sz)
        b_all = b_ref[rows, :]
        q_all = q_ref[rows, :].astype(F32) * (GLA_DK ** -0.5)
        k_all = k_ref[rows, :].astype(F32)
        v = v_ref[rows, :]
        st = state_ref[...]

        sl = [slice(c * c_sz, (c + 1) * c_sz) for c in range(GLA_GROUP)]
        b = [b_all[s] for s in sl]
        bl = [x[c_sz - 1:c_sz, :] for x in b]
        bm = [x[c_sz // 2:c_sz // 2 + 1, :] for x in b]
        q_start = [q_all[s] * jnp.exp(x) for s, x in zip(sl, b)]
        k_end = [k_all[s] * jnp.exp(t - x) for s, x, t in zip(sl, b, bl)]
        q_mid = [(q_all[s] * jnp.exp(x - m)).astype(BF16) for s, x, m in zip(sl, b, bm)]
        k_mid = [(k_all[s] * jnp.exp(m - x)).astype(BF16) for s, x, m in zip(sl, b, bm)]

        e1, e2, e3 = jnp.exp(bl[1]), jnp.exp(bl[2]), jnp.exp(bl[3])
        e0 = jnp.exp(bl[0])
        d01, d12, d23 = e0 * e1, e1 * e2, e2 * e3
        d012, d123 = d01 * e2, d12 * e3
        d_all = d012 * e3

        qs = cat([q_start[0], q_start[1] * e0, q_start[2] * d01, q_start[3] * d012]).astype(BF16)
        o = lax.dot_general(qs, st.astype(BF16), nt, preferred_element_type=F32)

        k_end_b = [x.astype(BF16) for x in k_end]
        a_r0 = lax.dot_general(q_mid[0], cat([k_mid[0], k_mid[1]]), nt,
                               preferred_element_type=F32)
        a_r1 = lax.dot_general(q_mid[1], cat([(k_end[0] * jnp.exp(bm[1])).astype(BF16),
                                              k_mid[1]]), nt, preferred_element_type=F32)
        a_r2 = lax.dot_general(q_mid[2], cat([k_mid[2], k_mid[3]]), nt,
                               preferred_element_type=F32)
        a_r3 = lax.dot_general(q_mid[3], cat([(k_end[2] * jnp.exp(bm[3])).astype(BF16),
                                              k_mid[3]]), nt, preferred_element_type=F32)
        a_off = lax.dot_general(cat([q_start[2], q_start[3] * e2]).astype(BF16),
                                cat([(k_end[0] * e1).astype(BF16), k_end_b[1]]), nt,
                                preferred_element_type=F32)
        zeros = jnp.zeros((2 * c_sz, 2 * c_sz), F32)
        att = jnp.concatenate([cat([a_r0, a_r1, a_off]), cat([zeros, a_r2, a_r3])], axis=1)
        att = jnp.where(causal, att, 0.0).astype(BF16)
        o = o + jnp.dot(att, v, preferred_element_type=F32)

        k_fin = cat([k_end[0] * d123, k_end[1] * d23, k_end[2] * e3, k_end[3]]).astype(BF16)
        kv_t = lax.dot_general(v, k_fin, tn, preferred_element_type=F32)
        state_new = st * d_all + kv_t

        ms = jnp.mean(o * o, axis=-1, keepdims=True)
        on = (o * lax.rsqrt(ms + RMS_EPS)) * gn
        r = r_ref[rows, :].astype(F32)
        gate = r / (1.0 + jnp.exp(-r))
        o_ref[rows, :] = (on * gate).astype(o_ref.dtype)
        state_ref[...] = state_new
        return carry

    lax.fori_loop(0, GLA_T // g_sz, group, 0, unroll=True)


def _gla(qkv, gr, b_cum, gn, batch, seq):
    t = GLA_T
    nt_ = seq // t
    m = batch * seq
    kb = (GLA_HEADS * GLA_DK) // GLA_DK
    vb = (2 * GLA_HEADS * GLA_DK) // GLA_DV
    return pl.pallas_call(
        _gla_kernel,
        grid=(batch, GLA_HEADS, nt_),
        in_specs=[
            pl.BlockSpec((t, GLA_DK), lambda b, h, s: (b * nt_ + s, h)),
            pl.BlockSpec((t, GLA_DK), lambda b, h, s: (b * nt_ + s, kb + h)),
            pl.BlockSpec((t, GLA_DV), lambda b, h, s: (b * nt_ + s, vb + h)),
            pl.BlockSpec((t, GLA_DV), lambda b, h, s: (b * nt_ + s, h)),
            pl.BlockSpec((t, GLA_DK), lambda b, h, s: (b * nt_ + s, h)),
            pl.BlockSpec((1, GLA_DV), lambda b, h, s: (0, 0)),
        ],
        out_specs=pl.BlockSpec((t, GLA_DV), lambda b, h, s: (b * nt_ + s, h)),
        out_shape=jax.ShapeDtypeStruct((m, GLA_HEADS * GLA_DV), BF16),
        scratch_shapes=[pltpu.VMEM((GLA_DV, GLA_DK), F32)],
        compiler_params=_cparams(("parallel", "parallel", "arbitrary")),
    )(qkv, qkv, qkv, gr, b_cum, gn)


def _t5_bucket_np(dist):
    max_exact = REL_BUCKETS // 2
    d_f = np.maximum(dist, 1).astype(np.float32)
    large = max_exact + (np.log(d_f / np.float32(max_exact))
                         / np.float32(math.log(REL_MAX_DIST / max_exact))
                         * np.float32(REL_BUCKETS - max_exact)).astype(np.int32)
    large = np.minimum(large, REL_BUCKETS - 1)
    return np.where(dist < max_exact, dist, large)


def _dil_bucket_index():
    steps = DIL_STEPS
    qi = np.arange(steps)[:, None]
    kj = np.arange(2 * steps)[None, :]
    rel = qi + steps - kj
    band = (rel >= 0) & (rel <= steps)
    out = []
    for _, dil in DIL_CONFIGS:
        bucket = _t5_bucket_np(np.clip(rel, 0, steps) * dil)
        out.append(np.where(band, bucket, REL_BUCKETS))
    return np.stack(out).astype(np.int32)


def _dil_bias_kernel(relb_ref, idx_ref, o_ref):
    head = pl.program_id(1)
    idx = idx_ref[0]
    bias = jnp.full(idx.shape, NEG_INF, F32)
    for bkt in range(REL_BUCKETS):
        bias = jnp.where(idx == bkt, relb_ref[bkt, head] * LOG2E, bias)
    o_ref[0, 0] = bias


def _dil_bias(rel_bias):
    steps = DIL_STEPS
    ncfg = len(DIL_CONFIGS)
    idx = jnp.asarray(_dil_bucket_index())
    return pl.pallas_call(
        _dil_bias_kernel,
        grid=(ncfg, DIL_HEADS),
        in_specs=[pl.BlockSpec(memory_space=pltpu.SMEM),
                  pl.BlockSpec((1, steps, 2 * steps), lambda c, h: (c, 0, 0))],
        out_specs=pl.BlockSpec((1, 1, steps, 2 * steps), lambda c, h: (c, h, 0, 0)),
        out_shape=jax.ShapeDtypeStruct((ncfg, DIL_HEADS, steps, 2 * steps), F32),
        compiler_params=_cparams(("parallel", "parallel")),
    )(rel_bias, idx)


DIL_GROUP = 8
DIL_SPLIT = 4
DIL_COPY_ROWS = 256


def _dil_kernel(bias_ref, q_ref, k_ref, v_ref, o_ref, m_sc, l_sc, acc_sc, qd, kd, vd, *, seq):
    steps = DIL_STEPS
    e = DIL_HEAD_DIM
    scale = e ** -0.5 * LOG2E
    nt = (((1,), (1,)), ((), ()))
    n_cfg = len(DIL_CONFIGS)
    split = DIL_SPLIT

    def rows_of(start, size, stride):
        if stride == 1:
            return pl.ds(start, size)
        return pl.ds(start, size, stride=stride)

    for src, dst in ((q_ref, qd), (k_ref, kd), (v_ref, vd)):
        for r in range(split):
            for c0 in range(0, seq // split, DIL_COPY_ROWS):
                dst[r, c0:c0 + DIL_COPY_ROWS, :] = src[
                    0, pl.ds(r + split * c0, DIL_COPY_ROWS, stride=split), :]

    def load(nat_ref, split_ref, dil, r, n, span, size):
        if dil % split:
            return nat_ref[0, rows_of(n * span + r, size, dil), :].astype(BF16)
        rows = rows_of(n * (span // split) + r // split, size, dil // split)
        return split_ref[r % split, rows, :].astype(BF16)

    def attend(c, dil, blocks, merge):
        span = steps * dil

        def q_start(r, n_q):
            start = n_q * span + r
            if dil == 1 and not isinstance(start, int):
                start = pl.multiple_of(start, span)
            return start

        q_rows = [rows_of(q_start(r, n_q), steps, dil) for r, n_q, _, _, _ in blocks]
        logits = []
        for r, n_q, n_k, n_keys, bias in blocks:
            q = load(q_ref, qd, dil, r, n_q, span, steps)
            k = load(k_ref, kd, dil, r, n_k, span, n_keys)
            logits.append(lax.dot_general(q, k, nt, preferred_element_type=F32) * scale + bias)
        stats = []
        for s in logits:
            m = jnp.max(s, axis=-1, keepdims=True)
            p = jnp.exp2(s - m)
            stats.append((m, jnp.sum(p, axis=-1, keepdims=True), p.astype(BF16)))
        pvs = [jnp.dot(p, load(v_ref, vd, dil, r, n_k, span, n_keys), preferred_element_type=F32)
               for (_, _, p), (r, _, n_k, n_keys, _) in zip(stats, blocks)]
        if not merge:
            for qr, (m, l, _), pv in zip(q_rows, stats, pvs):
                m_sc[c - 1, qr, :] = jnp.broadcast_to(m, (steps, e))
                l_sc[c - 1, qr, :] = jnp.broadcast_to(l, (steps, e))
                acc_sc[c - 1, qr, :] = pv
            return
        outs = []
        for qr, (m, l, _), pv in zip(q_rows, stats, pvs):
            ms = [m] + [m_sc[i, qr, :] for i in range(n_cfg - 1)]
            ls = [l] + [l_sc[i, qr, :] for i in range(n_cfg - 1)]
            accs = [pv] + [acc_sc[i, qr, :] for i in range(n_cfg - 1)]
            m_max = functools.reduce(jnp.maximum, ms)
            num = None
            den = None
            for m_i, l_i, acc_i in zip(ms, ls, accs):
                wgt = jnp.exp2(m_i - m_max)
                num = wgt * acc_i if num is None else num + wgt * acc_i
                den = wgt * l_i if den is None else den + wgt * l_i
            outs.append((qr, (num / den).astype(o_ref.dtype)))
        for qr, o in outs:
            o_ref[0, qr, :] = o

    for c in reversed(range(n_cfg)):
        dil = DIL_CONFIGS[c][1]
        nb = seq // dil // steps

        def first(r, c=c):
            return (r, 0, 0, steps, bias_ref[c, 0, :, steps:2 * steps])

        def later(j, c=c, dil=dil, nb=nb):
            if dil == 1:
                return (0, 1 + j, j, 2 * steps, bias_ref[c, 0])
            r = j // (nb - 1)
            n = 1 + j % (nb - 1)
            return (r, n, n - 1, 2 * steps, bias_ref[c, 0])

        for make, count in ((first, dil), (later, dil * (nb - 1))):
            full, rest = divmod(count, DIL_GROUP)

            def group(gi, carry, c=c, dil=dil, make=make):
                attend(c, dil, [make(gi * DIL_GROUP + u) for u in range(DIL_GROUP)], c == 0)
                return carry

            if full:
                lax.fori_loop(0, full, group, 0)
            if rest:
                attend(c, dil, [make(full * DIL_GROUP + u) for u in range(rest)], c == 0)


def _dilated(qkv3, bias, batch, seq):
    e = DIL_HEAD_DIM
    steps = DIL_STEPS
    ncfg = len(DIL_CONFIGS)
    assert DIL_CONFIGS[0][1] == 1, "the config that writes the output rows must be undilated"
    return pl.pallas_call(
        functools.partial(_dil_kernel, seq=seq),
        grid=(batch, DIL_HEADS),
        in_specs=[
            pl.BlockSpec((ncfg, 1, steps, 2 * steps), lambda b, h: (0, h, 0, 0)),
            pl.BlockSpec((1, seq, e), lambda b, h: (b, 0, h)),
            pl.BlockSpec((1, seq, e), lambda b, h: (b, 0, DIL_HEADS + h)),
            pl.BlockSpec((1, seq, e), lambda b, h: (b, 0, 2 * DIL_HEADS + h)),
        ],
        out_specs=pl.BlockSpec((1, seq, e), lambda b, h: (b, 0, h)),
        out_shape=jax.ShapeDtypeStruct((batch, seq, DIL_HEADS * e), BF16),
        scratch_shapes=[pltpu.VMEM((ncfg - 1, seq, e), F32)] * 3
        + [pltpu.VMEM((DIL_SPLIT, seq // DIL_SPLIT, e), F32)] * 3,
        compiler_params=_cparams(("parallel", "parallel")),
    )(bias, qkv3, qkv3, qkv3)


def _mix_out_kernel(a_ref, b_ref, wa_ref, wb_ref, x_ref, o_ref):
    acc = jnp.dot(a_ref[...], wa_ref[...], preferred_element_type=F32)
    acc = acc + jnp.dot(b_ref[...], wb_ref[...], preferred_element_type=F32)
    o_ref[...] = x_ref[...] + acc


def _mix_out(o_gla, o_dil, w, x, bm=1024, bn=1024):
    m, ka = o_gla.shape
    kb = o_dil.shape[1]
    n = w.shape[1]
    assert ka == kb and w.shape[0] == ka + kb
    return pl.pallas_call(
        _mix_out_kernel,
        grid=(m // bm, n // bn),
        in_specs=[pl.BlockSpec((bm, ka), lambda i, j: (i, 0)),
                  pl.BlockSpec((bm, kb), lambda i, j: (i, 0)),
                  pl.BlockSpec((ka, bn), lambda i, j: (0, j)),
                  pl.BlockSpec((kb, bn), lambda i, j: (1, j)),
                  pl.BlockSpec((bm, bn), lambda i, j: (i, j))],
        out_specs=pl.BlockSpec((bm, bn), lambda i, j: (i, j)),
        out_shape=jax.ShapeDtypeStruct((m, n), F32),
        compiler_params=_cparams(("parallel", "parallel")),
    )(o_gla, o_dil, w, w, x)


XATTN_SLAB = 128


def _xattn_kernel(h_ref, gx_ref, wq_ref, k_ref, v_ref, wo_ref, gf_ref, h2_ref, hn_ref):
    e = XATTN_HEAD_DIM
    nt = (((1,), (1,)), ((), ()))
    gx = gx_ref[...]
    gf = gf_ref[...]
    slabs = [slice(r0, r0 + XATTN_SLAB) for r0 in range(0, h_ref.shape[0], XATTN_SLAB)]
    hs = [h_ref[rows, :] for rows in slabs]
    hns = []
    for h in hs:
        ms = jnp.mean(h * h, axis=-1, keepdims=True)
        hns.append(((h * lax.rsqrt(ms + RMS_EPS)) * gx).astype(BF16))
    qs = [jnp.dot(hn, wq_ref[...], preferred_element_type=F32).astype(BF16) for hn in hns]
    os_ = []
    for q in qs:
        outs = []
        for hh in range(XATTN_HEADS):
            qh = q[:, hh * e:(hh + 1) * e]
            kh = k_ref[0, :, hh * e:(hh + 1) * e]
            vh = v_ref[0, :, hh * e:(hh + 1) * e]
            s = lax.dot_general(qh, kh, nt, preferred_element_type=F32) * (e ** -0.5)
            m = jnp.max(s, axis=-1, keepdims=True)
            p = jnp.exp(s - m)
            p = p / jnp.sum(p, axis=-1, keepdims=True)
            outs.append(jnp.dot(p.astype(BF16), vh, preferred_element_type=F32))
        os_.append(jnp.concatenate(outs, axis=1).astype(BF16))
    h2s = [h + jnp.dot(o, wo_ref[...], preferred_element_type=F32) for h, o in zip(hs, os_)]
    hn2s = []
    for h2 in h2s:
        ms2 = jnp.mean(h2 * h2, axis=-1, keepdims=True)
        hn2s.append(((h2 * lax.rsqrt(ms2 + RMS_EPS)) * gf).astype(hn_ref.dtype))
    for rows, h2, hn2 in zip(slabs, h2s, hn2s):
        h2_ref[rows, :] = h2
        hn_ref[rows, :] = hn2


def _xattn(h1, gx, wq, kx, vx, wo, gf, seq, tm=512):
    m, d = h1.shape
    mem_len = kx.shape[1]
    per_seq = seq // tm
    return pl.pallas_call(
        _xattn_kernel,
        grid=(m // tm,),
        in_specs=[pl.BlockSpec((tm, d), lambda i: (i, 0)),
                  pl.BlockSpec((1, d), lambda i: (0, 0)),
                  pl.BlockSpec((d, XATTN_WIDTH), lambda i: (0, 0),
                               pipeline_mode=pl.Buffered(1)),
                  pl.BlockSpec((1, mem_len, XATTN_WIDTH), lambda i: (i // per_seq, 0, 0)),
                  pl.BlockSpec((1, mem_len, XATTN_WIDTH), lambda i: (i // per_seq, 0, 0)),
                  pl.BlockSpec((XATTN_WIDTH, d), lambda i: (0, 0),
                               pipeline_mode=pl.Buffered(1)),
                  pl.BlockSpec((1, d), lambda i: (0, 0))],
        out_specs=[pl.BlockSpec((tm, d), lambda i: (i, 0)),
                   pl.BlockSpec((tm, d), lambda i: (i, 0))],
        out_shape=[jax.ShapeDtypeStruct((m, d), F32),
                   jax.ShapeDtypeStruct((m, d), BF16)],
        compiler_params=_cparams(("parallel",)),
    )(h1, gx.reshape(1, d), wq, kx, vx, wo, gf.reshape(1, d))


def _ffn_in_kernel(x_ref, wg_ref, wu_ref, halo_ref, cw_ref, cb_ref, a_ref):
    x = x_ref[...]
    g = jnp.dot(x, wg_ref[...], preferred_element_type=F32)
    u = jnp.dot(x, wu_ref[...], preferred_element_type=F32)
    halo = halo_ref[0]
    prev1 = halo[SUBLANES - 1:SUBLANES, :]
    prev2 = halo[SUBLANES - 2:SUBLANES - 1, :]
    row = lax.broadcasted_iota(jnp.int32, (SUBLANES, g.shape[1]), 0)
    r1 = pltpu.roll(g, 1, axis=0)
    r2 = pltpu.roll(g, 2, axis=0)
    head1 = jnp.where(row == 0, prev1, r1[:SUBLANES])
    head2 = jnp.where(row == 0, prev2, jnp.where(row == 1, prev1, r2[:SUBLANES]))
    g_m1 = jnp.concatenate([head1, r1[SUBLANES:]], axis=0)
    g_m2 = jnp.concatenate([head2, r2[SUBLANES:]], axis=0)
    cw = cw_ref[...]
    y = cb_ref[...] + g_m2 * cw[0:1, :]
    y = y + g_m1 * cw[1:2, :]
    y = y + g * cw[2:3, :]
    a_ref[...] = ((y / (1.0 + jnp.exp(-y))) * u).astype(a_ref.dtype)


def _ffn_in(x, wg, wu, halo_g, cw, cb, bm=FFN_BM, bn=512):
    m, k = x.shape
    n = wg.shape[1]
    return pl.pallas_call(
        _ffn_in_kernel,
        grid=(m // bm, n // bn),
        in_specs=[pl.BlockSpec((bm, k), lambda i, j: (i, 0)),
                  pl.BlockSpec((k, bn), lambda i, j: (0, j)),
                  pl.BlockSpec((k, bn), lambda i, j: (0, j)),
                  pl.BlockSpec((1, SUBLANES, bn), lambda i, j: (i, 0, j)),
                  pl.BlockSpec((CONV_WIDTH, bn), lambda i, j: (0, j)),
                  pl.BlockSpec((1, bn), lambda i, j: (0, j))],
        out_specs=pl.BlockSpec((bm, bn), lambda i, j: (i, j)),
        out_shape=jax.ShapeDtypeStruct((m, n), BF16),
        compiler_params=_cparams(("parallel", "parallel")),
    )(x, wg, wu, halo_g, cw, cb)


def _ffn_halo_rows(hn, batch, seq, bm=FFN_BM):
    d = hn.shape[1]
    tiles = seq // bm
    tail = hn.reshape(batch, tiles, bm, d)[:, :, bm - SUBLANES:, :]
    prev = jnp.concatenate([jnp.zeros_like(tail[:, :1]), tail[:, :-1]], axis=1)
    return prev.reshape(batch * tiles * SUBLANES, d)


FFN_OUT_COLS = 1024
FFN_RES_COLS = 512
FFN_OUT_VMEM_LIMIT = 63 * 1024 * 1024
FFN_NORM_ROWS = 256


def _ffn_out_kernel(a_ref, wd_ref, h_ref, fg_ref, o_ref, *, n_res):
    kk = pl.program_id(1)
    d = o_ref.shape[1]

    @pl.when(kk == 0)
    def _():
        o_ref[...] = jnp.zeros_like(o_ref)

    a = a_ref[...]
    for c0 in range(0, d, FFN_OUT_COLS):
        cols = slice(c0, c0 + FFN_OUT_COLS)
        o_ref[:, cols] += jnp.dot(a, wd_ref[:, cols], preferred_element_type=F32)

    for c in range(n_res):
        @pl.when(kk == c)
        def _(c=c):
            cols = slice(c * FFN_RES_COLS, (c + 1) * FFN_RES_COLS)
            o_ref[:, cols] += h_ref[...]

    @pl.when(kk == pl.num_programs(1) - 1)
    def _():
        fg = fg_ref[...]
        for r0 in range(0, o_ref.shape[0], FFN_NORM_ROWS):
            rows = slice(r0, r0 + FFN_NORM_ROWS)
            h3 = o_ref[rows, :]
            ms = jnp.mean(h3 * h3, axis=-1, keepdims=True)
            o_ref[rows, :] = (h3 * lax.rsqrt(ms + RMS_EPS)) * fg


def _ffn_out(a, wd, h2, fg, tm=1024, tk=1024):
    m, ff = a.shape
    d = wd.shape[1]
    n_res = d // FFN_RES_COLS
    assert ff // tk >= n_res, "one residual slab per contraction step"
    return pl.pallas_call(
        functools.partial(_ffn_out_kernel, n_res=n_res),
        grid=(m // tm, ff // tk),
        in_specs=[pl.BlockSpec((tm, tk), lambda i, k: (i, k)),
                  pl.BlockSpec((tk, d), lambda i, k: (k, 0)),
                  pl.BlockSpec((tm, FFN_RES_COLS), lambda i, k: (i, jnp.minimum(k, n_res - 1))),
                  pl.BlockSpec((1, d), lambda i, k: (0, 0))],
        out_specs=pl.BlockSpec((tm, d), lambda i, k: (i, 0)),
        out_shape=jax.ShapeDtypeStruct((m, d), F32),
        compiler_params=pltpu.CompilerParams(dimension_semantics=("parallel", "arbitrary"),
                                             vmem_limit_bytes=FFN_OUT_VMEM_LIMIT),
    )(a, wd, h2, fg.reshape(1, d))


def _layer(x2, mem2, rel_bias, batch, seq, mem_len, norm_mix_g, w_in, gla_w_gate2,
           gla_b_gate, gla_norm_g, w_out, norm_xattn_g, mem_norm_g, w_xq, w_xk, w_xv,
           w_xo, norm_ffn_g, w_ffn_gate, w_ffn_up, ffn_conv_w, ffn_conv_b, w_ffn_down,
           out_g):
    w_in_t = w_in.T.astype(BF16)
    w_glr = jnp.pad(w_in_t[_W_GLR:_W_GR], ((0, LANES - GLA_LOWRANK), (0, 0)))
    w2_pad = jnp.pad(gla_w_gate2, ((0, LANES - GLA_LOWRANK), (0, 0)))
    ff_pad = D_FF_PAD - D_FF
    wg = _cast_pad_cols(w_ffn_gate, D_FF_PAD)
    wu = _cast_pad_cols(w_ffn_up, D_FF_PAD)
    wd = _cast_pad_rows(w_ffn_down, D_FF_PAD)
    cw = jnp.pad(ffn_conv_w, ((0, 0), (0, ff_pad)))
    cb = jnp.pad(ffn_conv_b, ((0, ff_pad),)).reshape(1, D_FF_PAD)
    w_out_b = w_out.astype(BF16)
    w_kv = jnp.concatenate([w_xk, w_xv], axis=1).astype(BF16)

    hn = _rmsnorm(x2, norm_mix_g)
    d_qkv = _matmul_nt(hn, w_in_t, 1024, 1024, F32, row0=_W_DQKV, n=_W_END - _W_DQKV)
    g_qkv = _matmul_nt(hn, w_in_t, 1024, 1024, BF16, row0=_W_GQKV, n=_W_GLR - _W_GQKV)
    g_r = _matmul_nt(hn, w_in_t, 1024, 1024, BF16, row0=_W_GR, n=_W_DQKV - _W_GR)
    glr = _matmul_nt(hn, w_glr, 1024, LANES, F32)
    b_cum = _gla_gate(glr, w2_pad, gla_b_gate.reshape(1, -1))
    o_gla = _gla(g_qkv, g_r, b_cum, gla_norm_g.reshape(1, -1), batch, seq)
    o_dil = _dilated(d_qkv.reshape(batch, seq, 3 * DIL_HEADS * DIL_HEAD_DIM),
                     _dil_bias(rel_bias), batch, seq)
    o_dil = o_dil.reshape(batch * seq, DIL_HEADS * DIL_HEAD_DIM)
    h1 = _mix_out(o_gla, o_dil, w_out_b, x2)

    memn = _rmsnorm(mem2, mem_norm_g)
    kv = _matmul(memn, w_kv, 512, 512, BF16)
    kx = kv[:, :XATTN_WIDTH].reshape(batch, mem_len, XATTN_WIDTH)
    vx = kv[:, XATTN_WIDTH:].reshape(batch, mem_len, XATTN_WIDTH)
    h2, hn3 = _xattn(h1, norm_xattn_g, w_xq.astype(BF16), kx, vx, w_xo.astype(BF16),
                     norm_ffn_g, seq)

    halo_x = _ffn_halo_rows(hn3, batch, seq)
    halo_g = _matmul(halo_x, wg, halo_x.shape[0], 512, F32)
    halo_g = halo_g.reshape(-1, SUBLANES, D_FF_PAD)
    act = _ffn_in(hn3, wg, wu, halo_g, cw, cb)
    return _ffn_out(act, wd, h2, out_g)


def kernel(x, mem, rel_bias, norm_mix_g, w_in, gla_w_gate2, gla_b_gate, gla_norm_g, w_out,
           norm_xattn_g, mem_norm_g, w_xq, w_xk, w_xv, w_xo, norm_ffn_g, w_ffn_gate,
           w_ffn_up, ffn_conv_w, ffn_conv_b, w_ffn_down, final_norm_g):
    batch, seq, d = x.shape
    mem_len = mem.shape[1]
    depth = w_in.shape[0]
    assert depth == 1, "the fused final rmsnorm assumes a single layer"
    out = _layer(x.reshape(batch * seq, d), mem.reshape(batch * mem_len, d), rel_bias,
                 batch, seq, mem_len, norm_mix_g[0], w_in[0], gla_w_gate2[0],
                 gla_b_gate[0], gla_norm_g[0], w_out[0], norm_xattn_g[0], mem_norm_g[0],
                 w_xq[0], w_xk[0], w_xv[0], w_xo[0], norm_ffn_g[0], w_ffn_gate[0],
                 w_ffn_up[0], ffn_conv_w[0], ffn_conv_b[0], w_ffn_down[0], final_norm_g)
    return out.reshape(batch, seq, d)
```
